```python
import jax, jax.numpy as jnp
from jax import lax
import numpy as np

D_MODEL = 1024
BATCH = 8
SEQ = 4096
DEPTH = 1

MLA_HEADS = 8
MLA_NOPE_DIM = 64
MLA_ROPE_DIM = 32
MLA_V_DIM = 64
MLA_Q_RANK = 256
MLA_KV_RANK = 128
MOBA_HEADS = 8
MOBA_HEAD_DIM = 64
MOBA_BLOCK = 256
MOBA_TOPK = 3
MOBA_Q_CHUNK = 32
D_FF = 4 * D_MODEL
ROPE_THETA = 10000.0
NORM_EPS = 1e-6
Q_BLOCK = 128
NEG_INF = -1e30

MLA_WIDTH = MLA_HEADS * MLA_V_DIM
MOBA_WIDTH = MOBA_HEADS * MOBA_HEAD_DIM
MIX_WIDTH = MLA_WIDTH + MOBA_WIDTH
IN_COLS = MLA_Q_RANK + MLA_KV_RANK + MLA_ROPE_DIM + 3 * MOBA_WIDTH

kernel_name = 'hymba_mla_moba_sandwich_layer'


def rms_norm(x, g):
    xf = x.astype(jnp.float32)
    y = xf * lax.rsqrt(jnp.mean(xf * xf, axis=-1, keepdims=True) + NORM_EPS)
    return (y * g.astype(jnp.float32)).astype(x.dtype)


def rope(x, pos):
    half = x.shape[-1] // 2
    inv_freq = 1.0 / (ROPE_THETA ** (jnp.arange(half, dtype=jnp.float32) / half))
    ang = pos.astype(jnp.float32)[:, None] * inv_freq[None, :]
    cos = jnp.cos(ang).astype(x.dtype)
    sin = jnp.sin(ang).astype(x.dtype)
    x1 = x[..., :half]
    x2 = x[..., half:]
    return jnp.concatenate([x1 * cos - x2 * sin, x2 * cos + x1 * sin], axis=-1)


def causal_dense_attention(q, k, v, scale):
    B, H, S, Dq = q.shape
    Dv = v.shape[-1]
    nq = S // Q_BLOCK
    qb = q.reshape(B, H, nq, Q_BLOCK, Dq).transpose(2, 0, 1, 3, 4)
    k_pos = jnp.arange(S)

    def one(args):
        q_blk, start = args
        s = jnp.einsum('bhqd,bhkd->bhqk', q_blk, k).astype(jnp.float32) * scale
        q_pos = start + jnp.arange(Q_BLOCK)
        s = jnp.where(k_pos[None, :] <= q_pos[:, None], s, NEG_INF)
        p = jax.nn.softmax(s, axis=-1).astype(v.dtype)
        return jnp.einsum('bhqk,bhkd->bhqd', p, v)

    out = lax.map(one, (qb, jnp.arange(nq, dtype=jnp.int32) * Q_BLOCK))
    return out.transpose(1, 2, 0, 3, 4).reshape(B, H, S, Dv)


def mla_attention(c_q, c_kv, k_r, q_norm_g, w_q_up, kv_norm_g, w_kv_up, pos):
    B, S, _ = c_q.shape
    q = (rms_norm(c_q, q_norm_g) @ w_q_up).reshape(B, S, MLA_HEADS, MLA_NOPE_DIM + MLA_ROPE_DIM)
    q = q.transpose(0, 2, 1, 3)
    q_nope = q[..., :MLA_NOPE_DIM]
    q_rope = rope(q[..., MLA_NOPE_DIM:], pos)
    kv = (rms_norm(c_kv, kv_norm_g) @ w_kv_up).reshape(B, S, MLA_HEADS, MLA_NOPE_DIM + MLA_V_DIM)
    kv = kv.transpose(0, 2, 1, 3)
    k_nope = kv[..., :MLA_NOPE_DIM]
    v = kv[..., MLA_NOPE_DIM:]
    k_rope = rope(k_r[:, None, :, :], pos)
    k_rope = jnp.broadcast_to(k_rope, (B, MLA_HEADS, S, MLA_ROPE_DIM))
    q_full = jnp.concatenate([q_nope, q_rope], axis=-1)
    k_full = jnp.concatenate([k_nope, k_rope], axis=-1)
    scale = (MLA_NOPE_DIM + MLA_ROPE_DIM) ** -0.5
    return causal_dense_attention(q_full, k_full, v, scale)


def moba_attention(q, k, v):
    B, H, S, D = q.shape
    nb = -(-S // MOBA_BLOCK)
    s_pad = nb * MOBA_BLOCK
    pad = ((0, 0), (0, 0), (0, s_pad - S), (0, 0))
    k_p = jnp.pad(k, pad)
    v_p = jnp.pad(v, pad)
    k_blocks = k_p.reshape(B, H, nb, MOBA_BLOCK, D)
    v_blocks = v_p.reshape(B, H, nb, MOBA_BLOCK, D)
    k_mean = jnp.mean(k_blocks.astype(jnp.float32), axis=3)
    top = min(MOBA_TOPK, nb)
    n_chunks = S // MOBA_Q_CHUNK
    qc = q.reshape(B, H, n_chunks, MOBA_Q_CHUNK, D).transpose(2, 0, 1, 3, 4)
    b_idx = jnp.arange(B)[:, None, None, None]
    h_idx = jnp.arange(H)[None, :, None, None]
    blk_ids = jnp.arange(nb)
    offs = jnp.arange(MOBA_BLOCK)
    scale = D ** -0.5

    def one(args):
        q_blk, start = args
        cur = start // MOBA_BLOCK
        gate = jnp.einsum('bhqd,bhnd->bhqn', q_blk.astype(jnp.float32), k_mean)
        gate = jnp.where(blk_ids < cur, gate, NEG_INF)
        _, sel = lax.top_k(gate, top)
        sel_valid = sel < cur
        k_sel = k_blocks[b_idx, h_idx, sel]
        v_sel = v_blocks[b_idx, h_idx, sel]
        s_sel = jnp.einsum('bhqd,bhqnkd->bhqnk', q_blk, k_sel).astype(jnp.float32) * scale
        s_sel = jnp.where(sel_valid[..., None], s_sel, NEG_INF)
        s_sel = s_sel.reshape(B, H, MOBA_Q_CHUNK, top * MOBA_BLOCK)
        k_own = lax.dynamic_slice_in_dim(k_p, cur * MOBA_BLOCK, MOBA_BLOCK, axis=2)
        v_own = lax.dynamic_slice_in_dim(v_p, cur * MOBA_BLOCK, MOBA_BLOCK, axis=2)
        s_own = jnp.einsum('bhqd,bhkd->bhqk', q_blk, k_own).astype(jnp.float32) * scale
        q_pos = start + jnp.arange(MOBA_Q_CHUNK)
        k_pos = cur * MOBA_BLOCK + offs
        s_own = jnp.where(k_pos[None, :] <= q_pos[:, None], s_own, NEG_INF)
        p = jax.nn.softmax(jnp.concatenate([s_sel, s_own], axis=-1), axis=-1).astype(v.dtype)
        p_sel = p[..., :top * MOBA_BLOCK].reshape(B, H, MOBA_Q_CHUNK, top, MOBA_BLOCK)
        p_own = p[..., top * MOBA_BLOCK:]
        return (jnp.einsum('bhqnk,bhqnkd->bhqd', p_sel, v_sel)
                + jnp.einsum('bhqk,bhkd->bhqd', p_own, v_own))

    out = lax.map(one, (qc, jnp.arange(n_chunks, dtype=jnp.int32) * MOBA_Q_CHUNK))
    return out.transpose(1, 2, 0, 3, 4).reshape(B, H, S, D)


def setup_inputs(seed: int = 0) -> dict:
    key = jax.random.key(seed)
    ks = jax.random.split(key, 14)

    def nrm(k, shape, scale):
        return jax.random.normal(k, shape, jnp.float32) * scale

    def gain(k, n):
        return 1.0 + 0.05 * jax.random.normal(k, (DEPTH, n), jnp.float32)

    return {
        'x': nrm(ks[0], (BATCH, SEQ, D_MODEL), 1.0),
        'attn_pre_g': gain(ks[1], D_MODEL),
        'w_in': nrm(ks[2], (DEPTH, D_MODEL, IN_COLS), D_MODEL ** -0.5),
        'mla_q_norm_g': gain(ks[3], MLA_Q_RANK),
        'w_mla_q_up': nrm(ks[4], (DEPTH, MLA_Q_RANK, MLA_HEADS * (MLA_NOPE_DIM + MLA_ROPE_DIM)), MLA_Q_RANK ** -0.5),
        'mla_kv_norm_g': gain(ks[5], MLA_KV_RANK),
        'w_mla_kv_up': nrm(ks[6], (DEPTH, MLA_KV_RANK, MLA_HEADS * (MLA_NOPE_DIM + MLA_V_DIM)), MLA_KV_RANK ** -0.5),
        'w_out': nrm(ks[7], (DEPTH, MIX_WIDTH, D_MODEL), MIX_WIDTH ** -0.5),
        'attn_post_g': gain(ks[8], D_MODEL),
        'mlp_pre_g': gain(ks[9], D_MODEL),
        'w_up': nrm(ks[10], (DEPTH, D_MODEL, D_FF), D_MODEL ** -0.5),
        'w_down': nrm(ks[11], (DEPTH, D_FF, D_MODEL), D_FF ** -0.5),
        'mlp_post_g': gain(ks[12], D_MODEL),
    }


def reference(x, attn_pre_g, w_in, mla_q_norm_g, w_mla_q_up, mla_kv_norm_g, w_mla_kv_up,
              w_out, attn_post_g, mlp_pre_g, w_up, w_down, mlp_post_g):
    B, S, _ = x.shape
    pos = jnp.arange(S, dtype=jnp.int32)
    s1 = MLA_Q_RANK
    s2 = s1 + MLA_KV_RANK
    s3 = s2 + MLA_ROPE_DIM
    h = x
    for l in range(DEPTH):
        u = rms_norm(h, attn_pre_g[l])
        proj = u @ w_in[l]
        c_q = proj[..., :s1]
        c_kv = proj[..., s1:s2]
        k_r = proj[..., s2:s3]
        moba_qkv = proj[..., s3:].reshape(B, S, 3, MOBA_HEADS, MOBA_HEAD_DIM)
        moba_qkv = moba_qkv.transpose(2, 0, 3, 1, 4)
        mla_o = mla_attention(c_q, c_kv, k_r, mla_q_norm_g[l], w_mla_q_up[l],
                              mla_kv_norm_g[l], w_mla_kv_up[l], pos)
        moba_o = moba_attention(rope(moba_qkv[0], pos), rope(moba_qkv[1], pos), moba_qkv[2])
        mix = jnp.concatenate([
            mla_o.transpose(0, 2, 1, 3).reshape(B, S, MLA_WIDTH),
            moba_o.transpose(0, 2, 1, 3).reshape(B, S, MOBA_WIDTH)], axis=-1)
        h = h + rms_norm(mix @ w_out[l], attn_post_g[l])
        u = rms_norm(h, mlp_pre_g[l])
        a = jnp.square(jax.nn.relu(u @ w_up[l]))
        h = h + rms_norm(a @ w_down[l], mlp_post_g[l])
    return h
```

```python
import functools

import numpy as np
import jax
import jax.numpy as jnp
from jax import lax
from jax.experimental import pallas as pl
from jax.experimental.pallas import tpu as pltpu

MLA_HEADS = 8
MLA_NOPE_DIM = 64
MLA_ROPE_DIM = 32
MLA_V_DIM = 64
MLA_Q_RANK = 256
MLA_KV_RANK = 128
MOBA_HEADS = 8
MOBA_HEAD_DIM = 64
MOBA_BLOCK = 256
MOBA_TOPK = 3
ROPE_THETA = 10000.0
NORM_EPS = 1e-6
NEG_INF = -1e30

LANES = 128
MOBA_WIDTH = MOBA_HEADS * MOBA_HEAD_DIM
MLA_WIDTH = MLA_HEADS * MLA_V_DIM
MLA_PAD_WIDTH = MLA_HEADS * LANES
PROJ_TILE = MOBA_BLOCK
ATTN_TILE = 256
MLP_TILE = 256
FF_CHUNK = 1024
VMEM_LIMIT = 56 * 1024 * 1024

F32 = jnp.float32
BF16 = jnp.bfloat16
_NT = (((1,), (1,)), ((), ()))


def _w_in_columns():
    s2 = MLA_Q_RANK + MLA_KV_RANK
    s3 = s2 + MLA_ROPE_DIM
    half = MLA_ROPE_DIM // 2
    cols = list(range(0, s2))
    shared = [-1] * LANES
    for i in range(half):
        shared[32 + i] = s2 + i
        shared[96 + i] = s2 + half + i
    cols += shared
    for base in (s3, s3 + MOBA_WIDTH):
        for p in range(MOBA_HEADS // 2):
            for off in (0, MOBA_HEAD_DIM // 2):
                for h in (2 * p, 2 * p + 1):
                    cols += [base + h * MOBA_HEAD_DIM + off + i for i in range(MOBA_HEAD_DIM // 2)]
    cols += list(range(s3 + 2 * MOBA_WIDTH, s3 + 3 * MOBA_WIDTH))
    return np.asarray(cols, np.int32)


def _mla_q_columns():
    cols = []
    half = MLA_ROPE_DIM // 2
    for h in range(MLA_HEADS):
        b = h * (MLA_NOPE_DIM + MLA_ROPE_DIM)
        cols += [b + i for i in range(32)] + [b + MLA_NOPE_DIM + i for i in range(half)] + [-1] * 16
        cols += [b + 32 + i for i in range(32)] + [b + MLA_NOPE_DIM + half + i for i in range(half)] + [-1] * 16
    return np.asarray(cols, np.int32)


def _mla_k_columns():
    cols = []
    for h in range(MLA_HEADS):
        b = h * (MLA_NOPE_DIM + MLA_V_DIM)
        cols += [b + i for i in range(32)] + [-1] * 32 + [b + 32 + i for i in range(32)] + [-1] * 32
    return np.asarray(cols, np.int32)


def _mla_v_columns():
    cols = []
    for h in range(MLA_HEADS):
        b = h * (MLA_NOPE_DIM + MLA_V_DIM) + MLA_NOPE_DIM
        cols += [b + i for i in range(MLA_V_DIM)]
    return np.asarray(cols, np.int32)


def _gather_columns(w, cols):
    picked = jnp.take(w, jnp.asarray(np.maximum(cols, 0)), axis=1)
    return jnp.where(jnp.asarray(cols >= 0)[None, :], picked, 0.0).astype(BF16)


def _rope_tables(seq):
    pos = jnp.arange(seq, dtype=F32)

    def cs(half):
        inv_freq = 1.0 / (ROPE_THETA ** (jnp.arange(half, dtype=F32) / half))
        ang = pos[:, None] * inv_freq[None, :]
        return jnp.cos(ang), jnp.sin(ang)

    c, s = cs(MLA_ROPE_DIM // 2)
    one = lambda n: jnp.ones((seq, n), F32)
    zero = lambda n: jnp.zeros((seq, n), F32)
    cos_mla = jnp.concatenate([one(32), c, one(16), one(32), c, one(16)], axis=1)
    sin_mla = jnp.concatenate([zero(32), -s, zero(16), zero(32), s, zero(16)], axis=1)
    c, s = cs(MOBA_HEAD_DIM // 2)
    cos_moba = jnp.concatenate([c, c, c, c], axis=1)
    sin_moba = jnp.concatenate([-s, -s, s, s], axis=1)
    return cos_mla, sin_mla, cos_moba, sin_moba


def _rms(x, g):
    ms = jnp.mean(x * x, axis=-1, keepdims=True)
    return x * lax.rsqrt(ms + NORM_EPS) * g


def _rope(x, cos, sin):
    return x * cos + pltpu.roll(x, LANES // 2, 1) * sin


def _proj_kernel(x_ref, g_ref, win_ref, gq_ref, wq_ref, gkv_ref, wk_ref, wv_ref,
                 cm_ref, sm_ref, co_ref, so_ref,
                 mq_ref, mk_ref, mv_ref, oq_ref, oqf_ref, ok_ref, ov_ref, kmean_ref,
                 *, mla_scale, moba_scale):
    u = _rms(x_ref[...], g_ref[...]).astype(BF16)

    def proj(lo, hi):
        return jnp.dot(u, win_ref[:, lo:hi], preferred_element_type=F32)

    cm, sm = cm_ref[...], sm_ref[...]
    co, so = co_ref[...], so_ref[...]
    c0 = MLA_Q_RANK
    c1 = c0 + MLA_KV_RANK
    c2 = c1 + LANES
    c3 = c2 + MOBA_WIDTH
    c4 = c3 + MOBA_WIDTH
    c5 = c4 + MOBA_WIDTH

    cq = _rms(proj(0, c0), gq_ref[...]).astype(BF16)
    q = jnp.dot(cq, wq_ref[...], preferred_element_type=F32)
    for h in range(MLA_HEADS):
        sl = slice(h * LANES, (h + 1) * LANES)
        mq_ref[:, sl] = (_rope(q[:, sl], cm, sm) * mla_scale).astype(BF16)

    ckv = _rms(proj(c0, c1), gkv_ref[...]).astype(BF16)
    k_shared = _rope(proj(c1, c2), cm, sm)
    k_nope = jnp.dot(ckv, wk_ref[...], preferred_element_type=F32)
    for h in range(MLA_HEADS):
        sl = slice(h * LANES, (h + 1) * LANES)
        mk_ref[:, sl] = (k_nope[:, sl] + k_shared).astype(BF16)
    mv_ref[...] = jnp.dot(ckv, wv_ref[...], preferred_element_type=F32).astype(BF16)

    oq = proj(c2, c3)
    ok = proj(c3, c4)
    for p in range(MOBA_WIDTH // LANES):
        sl = slice(p * LANES, (p + 1) * LANES)
        q_rot = _rope(oq[:, sl], co, so) * moba_scale
        oq_ref[:, sl] = q_rot.astype(BF16)
        oqf_ref[:, sl] = q_rot
        k_rot = _rope(ok[:, sl], co, so)
        ok_ref[:, sl] = k_rot.astype(BF16)
        kmean_ref[0, :, sl] = jnp.mean(k_rot, axis=0, keepdims=True)
    ov_ref[...] = proj(c4, c5).astype(BF16)


def _softmax_step(carry, s, v):
    m, l, acc = carry
    m_new = jnp.maximum(m, jnp.max(s, axis=-1, keepdims=True))
    alpha = jnp.exp(m - m_new)
    p = jnp.exp(s - m_new)
    l = alpha * l + jnp.sum(p, axis=-1, keepdims=True)
    acc = alpha * acc + jnp.dot(p.astype(BF16), v, preferred_element_type=F32)
    return m_new, l, acc


def _softmax_init(tq):
    return (jnp.full((tq, 1), NEG_INF, F32), jnp.zeros((tq, 1), F32), jnp.zeros((tq, LANES), F32))


def _mla_attn_kernel(q_ref, k_ref, v_ref, o_ref, *, tile):
    qi = pl.program_id(2)
    lane = lax.broadcasted_iota(jnp.int32, (tile, LANES), 1)
    causal = (lax.broadcasted_iota(jnp.int32, (tile, tile), 1)
              <= lax.broadcasted_iota(jnp.int32, (tile, tile), 0))
    outs = []
    for h in range(2):
        hs = slice(h * LANES, (h + 1) * LANES)
        q = q_ref[0, :, hs]

        def scores(j):
            start = pl.multiple_of(j * tile, tile)
            k = k_ref[0, pl.ds(start, tile), hs]
            v = v_ref[0, pl.ds(start, tile), :]
            return lax.dot_general(q, k, _NT, preferred_element_type=F32), v

        def past(j, carry):
            s, v = scores(j)
            return _softmax_step(carry, s, v)

        carry = lax.fori_loop(0, qi, past, _softmax_init(tile))
        s, v = scores(qi)
        _, l, acc = _softmax_step(carry, jnp.where(causal, s, NEG_INF), v)
        outs.append(acc * (1.0 / l))
    o_ref[0] = jnp.where(lane < MLA_V_DIM, outs[0], outs[1]).astype(BF16)


def _moba_attn_kernel(q_ref, qf_ref, k_ref, v_ref, km_ref, o_ref, *, tile, n_blocks):
    cur = pl.program_id(2)
    lane = lax.broadcasted_iota(jnp.int32, (tile, LANES), 1)
    causal = (lax.broadcasted_iota(jnp.int32, (tile, tile), 1)
              <= lax.broadcasted_iota(jnp.int32, (tile, tile), 0))
    blk = lax.broadcasted_iota(jnp.int32, (tile, n_blocks), 1)
    outs = []
    for h in range(2):
        mine = (lane & (MOBA_HEAD_DIM // 2)) == h * (MOBA_HEAD_DIM // 2)
        q = jnp.where(mine, q_ref[0], jnp.zeros((), BF16))
        qf = jnp.where(mine, qf_ref[0], 0.0)

        gate = lax.dot_general(qf, km_ref[0], _NT, preferred_element_type=F32,
                               precision=lax.Precision.HIGHEST)
        ahead_count = jnp.zeros((tile, n_blocks), F32)
        for jp in range(n_blocks):
            g = gate[:, jp:jp + 1]
            ahead = (g > gate) | ((g == gate) & (jp < blk))
            ahead_count = ahead_count + jnp.where(ahead, (jp < cur).astype(F32), 0.0)
        keep = (blk < cur) & (ahead_count < MOBA_TOPK)
        bias = jnp.where(keep, 0.0, NEG_INF)

        def scores(j):
            start = pl.multiple_of(j * tile, tile)
            k = k_ref[0, pl.ds(start, tile), :]
            v = v_ref[0, pl.ds(start, tile), :]
            return lax.dot_general(q, k, _NT, preferred_element_type=F32), v

        def past(j, carry):
            s, v = scores(j)
            col = jnp.sum(jnp.where(blk == j, bias, 0.0), axis=-1, keepdims=True)
            return _softmax_step(carry, s + col, v)

        s, v = scores(cur)
        carry = _softmax_step(_softmax_init(tile), jnp.where(causal, s, NEG_INF), v)
        _, l, acc = lax.fori_loop(0, cur, past, carry)
        outs.append(acc * (1.0 / l))
    o_ref[0] = jnp.where(lane < MOBA_HEAD_DIM, outs[0], outs[1]).astype(BF16)


def _mlp_kernel(x_ref, a_ref, b_ref, wo_ref, g1_ref, g2_ref, wu_ref, wd_ref, g3_ref, o_ref,
                *, ff_chunk):
    n_a = a_ref.shape[1]
    y = (jnp.dot(a_ref[...], wo_ref[0:n_a, :], preferred_element_type=F32)
         + jnp.dot(b_ref[...], wo_ref[n_a:, :], preferred_element_type=F32))
    h = x_ref[...] + _rms(y, g1_ref[...])
    u = _rms(h, g2_ref[...]).astype(BF16)
    acc = jnp.zeros(h.shape, F32)
    for c in range(wu_ref.shape[1] // ff_chunk):
        sl = slice(c * ff_chunk, (c + 1) * ff_chunk)
        a = jnp.maximum(jnp.dot(u, wu_ref[:, sl], preferred_element_type=F32), 0.0)
        acc = acc + jnp.dot((a * a).astype(BF16), wd_ref[sl, :], preferred_element_type=F32)
    o_ref[...] = h + _rms(acc, g3_ref[...])


def _const_spec(shape):
    return pl.BlockSpec(shape, lambda *_: (0,) * len(shape))


def _layer(x, attn_pre_g, w_in, mla_q_norm_g, w_mla_q_up, mla_kv_norm_g, w_mla_kv_up,
           w_out, attn_post_g, mlp_pre_g, w_up, w_down, mlp_post_g):
    B, S, D = x.shape
    T = B * S
    assert S % MOBA_BLOCK == 0 and PROJ_TILE == MOBA_BLOCK and ATTN_TILE == MOBA_BLOCK
    n_blocks = S // MOBA_BLOCK
    row = lambda g: g.reshape(1, -1).astype(F32)

    win = _gather_columns(w_in, _w_in_columns())
    wq = _gather_columns(w_mla_q_up, _mla_q_columns())
    wk = _gather_columns(w_mla_kv_up, _mla_k_columns())
    wv = _gather_columns(w_mla_kv_up, _mla_v_columns())
    cos_mla, sin_mla, cos_moba, sin_moba = _rope_tables(S)
    n_in = win.shape[1]

    tm = PROJ_TILE
    tok = lambda w: pl.BlockSpec((tm, w), lambda i: (i, 0))
    pos = pl.BlockSpec((tm, LANES), lambda i: (i % (S // tm), 0))
    x2 = x.reshape(T, D)
    mq, mk, mv, oq, oqf, ok, ov, kmean = pl.pallas_call(
        functools.partial(_proj_kernel,
                          mla_scale=float((MLA_NOPE_DIM + MLA_ROPE_DIM) ** -0.5),
                          moba_scale=float(MOBA_HEAD_DIM ** -0.5)),
        grid=(T // tm,),
        in_specs=[tok(D), _const_spec((1, D)), _const_spec((D, n_in)),
                  _const_spec((1, MLA_Q_RANK)), _const_spec((MLA_Q_RANK, MLA_PAD_WIDTH)),
                  _const_spec((1, MLA_KV_RANK)), _const_spec((MLA_KV_RANK, MLA_PAD_WIDTH)),
                  _const_spec((MLA_KV_RANK, MLA_WIDTH)), pos, pos, pos, pos],
        out_specs=[tok(MLA_PAD_WIDTH), tok(MLA_PAD_WIDTH), tok(MLA_WIDTH),
                   tok(MOBA_WIDTH), tok(MOBA_WIDTH), tok(MOBA_WIDTH), tok(MOBA_WIDTH),
                   pl.BlockSpec((1, 1, MOBA_WIDTH), lambda i: (i, 0, 0))],
        out_shape=[jax.ShapeDtypeStruct((T, MLA_PAD_WIDTH), BF16),
                   jax.ShapeDtypeStruct((T, MLA_PAD_WIDTH), BF16),
                   jax.ShapeDtypeStruct((T, MLA_WIDTH), BF16),
                   jax.ShapeDtypeStruct((T, MOBA_WIDTH), BF16),
                   jax.ShapeDtypeStruct((T, MOBA_WIDTH), F32),
                   jax.ShapeDtypeStruct((T, MOBA_WIDTH), BF16),
                   jax.ShapeDtypeStruct((T, MOBA_WIDTH), BF16),
                   jax.ShapeDtypeStruct((T // tm, 1, MOBA_WIDTH), F32)],
        compiler_params=pltpu.CompilerParams(vmem_limit_bytes=VMEM_LIMIT),
        name="token_projection",
    )(x2, row(attn_pre_g), win, row(mla_q_norm_g), wq, row(mla_kv_norm_g), wk, wv,
      cos_mla, sin_mla, cos_moba, sin_moba)

    tq = ATTN_TILE
    seq3 = lambda a: a.reshape(B, S, a.shape[-1])
    q_tile = lambda w: pl.BlockSpec((1, tq, w), lambda b, p, i: (b, i, p))
    resident = lambda w: pl.BlockSpec((1, S, w), lambda b, p, i: (b, 0, p))
    attn_params = pltpu.CompilerParams(vmem_limit_bytes=VMEM_LIMIT)

    mla_o = pl.pallas_call(
        functools.partial(_mla_attn_kernel, tile=tq),
        grid=(B, MLA_HEADS // 2, S // tq),
        in_specs=[q_tile(2 * LANES), resident(2 * LANES), resident(LANES)],
        out_specs=q_tile(LANES),
        out_shape=jax.ShapeDtypeStruct((B, S, MLA_WIDTH), BF16),
        compiler_params=attn_params,
        name="mla_attention",
    )(seq3(mq), seq3(mk), seq3(mv))

    moba_o = pl.pallas_call(
        functools.partial(_moba_attn_kernel, tile=tq, n_blocks=n_blocks),
        grid=(B, MOBA_HEADS // 2, S // tq),
        in_specs=[q_tile(LANES), q_tile(LANES), resident(LANES), resident(LANES),
                  pl.BlockSpec((1, n_blocks, LANES), lambda b, p, i: (b, 0, p))],
        out_specs=q_tile(LANES),
        out_shape=jax.ShapeDtypeStruct((B, S, MOBA_WIDTH), BF16),
        compiler_params=attn_params,
        name="moba_attention",
    )(seq3(oq), seq3(oqf), seq3(ok), seq3(ov), kmean.reshape(B, n_blocks, MOBA_WIDTH))

    tm = MLP_TILE
    d_ff = w_up.shape[1]
    tok = lambda w: pl.BlockSpec((tm, w), lambda i: (i, 0))
    out = pl.pallas_call(
        functools.partial(_mlp_kernel, ff_chunk=FF_CHUNK),
        grid=(T // tm,),
        in_specs=[tok(D), tok(MLA_WIDTH), tok(MOBA_WIDTH),
                  _const_spec((MLA_WIDTH + MOBA_WIDTH, D)), _const_spec((1, D)), _const_spec((1, D)),
                  _const_spec((D, d_ff)), _const_spec((d_ff, D)), _const_spec((1, D))],
        out_specs=tok(D),
        out_shape=jax.ShapeDtypeStruct((T, D), F32),
        compiler_params=pltpu.CompilerParams(vmem_limit_bytes=VMEM_LIMIT),
        name="out_proj_mlp",
    )(x2, mla_o.reshape(T, MLA_WIDTH), moba_o.reshape(T, MOBA_WIDTH),
      w_out.astype(BF16), row(attn_post_g), row(mlp_pre_g),
      w_up.astype(BF16), w_down.astype(BF16), row(mlp_post_g))
    return out.reshape(B, S, D)


def kernel(x, attn_pre_g, w_in, mla_q_norm_g, w_mla_q_up, mla_kv_norm_g, w_mla_kv_up, w_out,
           attn_post_g, mlp_pre_g, w_up, w_down, mlp_post_g):
    h = x
    for l in range(w_in.shape[0]):
        h = _layer(h, attn_pre_g[l], w_in[l], mla_q_norm_g[l], w_mla_q_up[l], mla_kv_norm_g[l],
                   w_mla_kv_up[l], w_out[l], attn_post_g[l], mlp_pre_g[l], w_up[l], w_down[l],
                   mlp_post_g[l])
    return h
```

```python
import functools
import math

import numpy as np
import jax
import jax.numpy as jnp
from jax import lax
from jax.experimental import pallas as pl
from jax.experimental.pallas import tpu as pltpu

MLA_HEADS = 8
MLA_NOPE_DIM = 64
MLA_ROPE_DIM = 32
MLA_V_DIM = 64
MLA_Q_RANK = 256
MLA_KV_RANK = 128
MOBA_HEADS = 8
MOBA_HEAD_DIM = 64
MOBA_BLOCK = 256
MOBA_TOPK = 3
ROPE_THETA = 10000.0
NORM_EPS = 1e-6
NEG_INF = -1e30

LANES = 128
HALF = LANES // 2
MOBA_WIDTH = MOBA_HEADS * MOBA_HEAD_DIM
MLA_WIDTH = MLA_HEADS * MLA_V_DIM
MLA_PAD_WIDTH = MLA_HEADS * LANES
PROJ_TILE = MOBA_BLOCK
ATTN_TILE = MOBA_BLOCK
MLP_TILE = 256
FF_CHUNK = 1024
VMEM_LIMIT = 56 * 1024 * 1024

F32 = jnp.float32
BF16 = jnp.bfloat16
_NT = (((1,), (1,)), ((), ()))


def _moba_pair_columns(base):
    cols = []
    for p in range(MOBA_HEADS // 2):
        for off in (0, MOBA_HEAD_DIM // 2):
            for h in (2 * p, 2 * p + 1):
                cols += [base + h * MOBA_HEAD_DIM + off + i for i in range(MOBA_HEAD_DIM // 2)]
    return cols


def _w_in_columns():
    s2 = MLA_Q_RANK + MLA_KV_RANK
    s3 = s2 + MLA_ROPE_DIM
    half = MLA_ROPE_DIM // 2
    shared = [-1] * LANES
    for i in range(half):
        shared[32 + i] = s2 + i
        shared[96 + i] = s2 + half + i
    row_major = list(range(0, s2)) + shared + _moba_pair_columns(s3 + MOBA_WIDTH)
    feature_major = _moba_pair_columns(s3) + list(range(s3 + 2 * MOBA_WIDTH, s3 + 3 * MOBA_WIDTH))
    return np.asarray(row_major, np.int32), np.asarray(feature_major, np.int32)


def _mla_q_columns():
    cols = []
    half = MLA_ROPE_DIM // 2
    for h in range(MLA_HEADS):
        b = h * (MLA_NOPE_DIM + MLA_ROPE_DIM)
        cols += [b + i for i in range(32)] + [b + MLA_NOPE_DIM + i for i in range(half)] + [-1] * 16
        cols += [b + 32 + i for i in range(32)] + [b + MLA_NOPE_DIM + half + i for i in range(half)] + [-1] * 16
    return np.asarray(cols, np.int32)


def _mla_k_columns():
    cols = []
    for h in range(MLA_HEADS):
        b = h * (MLA_NOPE_DIM + MLA_V_DIM)
        cols += [b + i for i in range(32)] + [-1] * 32 + [b + 32 + i for i in range(32)] + [-1] * 32
    return np.asarray(cols, np.int32)


def _mla_v_columns():
    cols = []
    for h in range(MLA_HEADS):
        b = h * (MLA_NOPE_DIM + MLA_V_DIM) + MLA_NOPE_DIM
        cols += [b + i for i in range(MLA_V_DIM)]
    return np.asarray(cols, np.int32)


def _gather_columns(w, cols):
    picked = jnp.take(w, jnp.asarray(np.maximum(cols, 0)), axis=1)
    return jnp.where(jnp.asarray(cols >= 0)[None, :], picked, 0.0).astype(BF16)


def _rope_tables(seq):
    pos = jnp.arange(seq, dtype=F32)

    def cs(half):
        inv_freq = 1.0 / (ROPE_THETA ** (jnp.arange(half, dtype=F32) / half))
        ang = pos[:, None] * inv_freq[None, :]
        return jnp.cos(ang), jnp.sin(ang)

    c, s = cs(MLA_ROPE_DIM // 2)
    one = lambda n: jnp.ones((seq, n), F32)
    zero = lambda n: jnp.zeros((seq, n), F32)
    cos_mla = jnp.concatenate([one(32), c, one(16), one(32), c, one(16)], axis=1)
    sin_mla = jnp.concatenate([zero(32), -s, zero(16), zero(32), s, zero(16)], axis=1)
    c, s = cs(MOBA_HEAD_DIM // 2)
    cos_moba = jnp.concatenate([c, c, c, c], axis=1)
    sin_moba = jnp.concatenate([-s, -s, s, s], axis=1)
    return cos_mla, sin_mla, cos_moba, sin_moba


def _rms(x, g):
    ms = jnp.mean(x * x, axis=-1, keepdims=True)
    return x * lax.rsqrt(ms + NORM_EPS) * g


def _rope_rows(x, cos, sin):
    return x * cos + pltpu.roll(x, HALF, 1) * sin


def _rope_cols(x, cos, sin):
    return x * cos + jnp.concatenate([x[HALF:], x[:HALF]], axis=0) * sin


def _proj_kernel(x_ref, g_ref, win_ref, wint_ref, gq_ref, wqt_ref, gkv_ref, wk_ref, wvt_ref,
                 cm_ref, sm_ref, co_ref, so_ref, cmt_ref, smt_ref, cot_ref, sot_ref,
                 mqt_ref, mk_ref, mvt_ref, oqt_ref, oqft_ref, ok_ref, ovt_ref, kmean_ref,
                 *, mla_scale, moba_scale):
    u = _rms(x_ref[...], g_ref[...]).astype(BF16)

    def proj(lo, hi):
        return jnp.dot(u, win_ref[:, lo:hi], preferred_element_type=F32)

    def proj_t(lo, hi):
        return lax.dot_general(wint_ref[lo:hi, :], u, _NT, preferred_element_type=F32)

    c0 = MLA_Q_RANK
    c1 = c0 + MLA_KV_RANK
    c2 = c1 + LANES
    c3 = c2 + MOBA_WIDTH

    cq = _rms(proj(0, c0), gq_ref[...]).astype(BF16)
    qt = lax.dot_general(wqt_ref[...], cq, _NT, preferred_element_type=F32)
    cmt, smt = cmt_ref[...], smt_ref[...]
    for h in range(MLA_HEADS):
        sl = slice(h * LANES, (h + 1) * LANES)
        mqt_ref[0, sl, :] = (_rope_cols(qt[sl], cmt, smt) * mla_scale).astype(BF16)

    ckv = _rms(proj(c0, c1), gkv_ref[...]).astype(BF16)
    k_shared = _rope_rows(proj(c1, c2), cm_ref[...], sm_ref[...])
    k_nope = jnp.dot(ckv, wk_ref[...], preferred_element_type=F32)
    for h in range(MLA_HEADS):
        sl = slice(h * LANES, (h + 1) * LANES)
        mk_ref[:, sl] = (k_nope[:, sl] + k_shared).astype(BF16)
    mvt_ref[0] = lax.dot_general(wvt_ref[...], ckv, _NT, preferred_element_type=F32).astype(BF16)

    oqt = proj_t(0, MOBA_WIDTH)
    ok = proj(c2, c3)
    co, so = co_ref[...], so_ref[...]
    cot, sot = cot_ref[...], sot_ref[...]
    for p in range(MOBA_WIDTH // LANES):
        sl = slice(p * LANES, (p + 1) * LANES)
        q_rot = _rope_cols(oqt[sl], cot, sot) * moba_scale
        oqt_ref[0, sl, :] = q_rot.astype(BF16)
        oqft_ref[0, sl, :] = q_rot
        k_rot = _rope_rows(ok[:, sl], co, so)
        ok_ref[:, sl] = k_rot.astype(BF16)
        kmean_ref[0, :, sl] = jnp.mean(k_rot, axis=0, keepdims=True)
    ovt_ref[0] = proj_t(MOBA_WIDTH, 2 * MOBA_WIDTH).astype(BF16)


def _flash_heads(q_cat, k_ref, vt_ref, qi, tq, n_heads, bias_ref=None):
    tk = 2 * tq
    n_full = qi // 2
    width = n_heads * tq
    d_v = vt_ref.shape[1] // n_heads

    def scores(c):
        start = pl.multiple_of(c * tk, tk)
        s = jnp.dot(k_ref[0, pl.ds(start, tk), :], q_cat, preferred_element_type=F32)
        if bias_ref is not None:
            b0 = bias_ref[pl.ds(2 * c, 1), :]
            b1 = bias_ref[pl.ds(2 * c + 1, 1), :]
            s = jnp.concatenate([s[:tq] + b0, s[tq:] + b1], axis=0)
        return s

    def col_max(s):
        return jnp.max(s, axis=0, keepdims=True)

    def accumulate(c, state, s, s_max):
        m, l, acc = state
        m_new = jnp.maximum(m, s_max)
        alpha = jnp.exp2(m - m_new)
        p = jnp.exp2(s - m_new)
        l = alpha * l + jnp.sum(p, axis=0, keepdims=True)
        p = p.astype(BF16)
        start = pl.multiple_of(c * tk, tk)
        pv = [jnp.dot(vt_ref[0, h * d_v:(h + 1) * d_v, pl.ds(start, tk)],
                      p[:, h * tq:(h + 1) * tq], preferred_element_type=F32)
              for h in range(n_heads)]
        return m_new, l, alpha * acc + jnp.concatenate(pv, axis=1)

    def body(c, carry):
        state, s, s_max = carry
        s_next = scores(c + 1)
        return accumulate(c, state, s, s_max), s_next, col_max(s_next)

    init = (jnp.full((1, width), NEG_INF, F32), jnp.zeros((1, width), F32),
            jnp.zeros((d_v, width), F32))
    s0 = scores(0)
    state, s, _ = lax.fori_loop(0, n_full, body, (init, s0, col_max(s0)))
    key = lax.broadcasted_iota(jnp.int32, (tk, tq), 0)
    qry = lax.broadcasted_iota(jnp.int32, (tk, tq), 1) + (qi - 2 * n_full) * tq
    visible = jnp.concatenate([key <= qry] * n_heads, axis=1)
    s = jnp.where(visible, s, NEG_INF)
    _, l, acc = accumulate(n_full, state, s, col_max(s))
    return acc * (1.0 / l)


def _heads_to_rows(out_t, n_heads, tq):
    return jnp.concatenate([out_t[:, h * tq:(h + 1) * tq] for h in range(n_heads)], axis=0).T


def _mla_attn_kernel(qt_ref, k_ref, vt_ref, o_ref, *, tq):
    qi = pl.program_id(2)
    zero = jnp.zeros((LANES, tq), BF16)
    q_cat = jnp.concatenate(
        [jnp.concatenate([qt_ref[0, :LANES, :], zero], axis=1),
         jnp.concatenate([zero, qt_ref[0, LANES:, :]], axis=1)], axis=0)
    out_t = _flash_heads(q_cat, k_ref, vt_ref, qi, tq, 2)
    o_ref[0] = _heads_to_rows(out_t, 2, tq).astype(BF16)


def _moba_attn_kernel(qt_ref, qft_ref, k_ref, vt_ref, km_ref, o_ref, bias_ref, *, tq, n_blocks):
    cur = pl.program_id(2)
    feat = lax.broadcasted_iota(jnp.int32, (LANES, tq), 0)
    blk = lax.broadcasted_iota(jnp.int32, (n_blocks, 2 * tq), 0)
    mine = [(feat & (MOBA_HEAD_DIM // 2)) == h * (MOBA_HEAD_DIM // 2) for h in range(2)]
    q_cat = jnp.concatenate([jnp.where(m, qt_ref[0], jnp.zeros((), BF16)) for m in mine], axis=1)
    qf_cat = jnp.concatenate([jnp.where(m, qft_ref[0], 0.0) for m in mine], axis=1)
    gate = jnp.dot(km_ref[0], qf_cat, preferred_element_type=F32,
                   precision=lax.Precision.HIGHEST)
    ahead_count = jnp.zeros((n_blocks, 2 * tq), F32)
    for jp in range(n_blocks):
        g = gate[jp:jp + 1, :]
        ahead = (g > gate) | ((g == gate) & (jp < blk))
        ahead_count = ahead_count + jnp.where(ahead, (jp < cur).astype(F32), 0.0)
    keep = ((blk < cur) & (ahead_count < MOBA_TOPK)) | (blk == cur)
    bias_ref[...] = jnp.where(keep, 0.0, NEG_INF)
    out_t = _flash_heads(q_cat, k_ref, vt_ref, cur, tq, 2, bias_ref)
    o_ref[0] = _heads_to_rows(out_t, 2, tq).astype(BF16)


def _mlp_kernel(x_ref, a_ref, b_ref, wo_ref, g1_ref, g2_ref, wu_ref, wd_ref, g3_ref, o_ref,
                *, ff_chunk):
    n_a = a_ref.shape[1]
    y = (jnp.dot(a_ref[...], wo_ref[0:n_a, :], preferred_element_type=F32)
         + jnp.dot(b_ref[...], wo_ref[n_a:, :], preferred_element_type=F32))
    h = x_ref[...] + _rms(y, g1_ref[...])
    u = _rms(h, g2_ref[...]).astype(BF16)
    acc = jnp.zeros(h.shape, F32)
    for c in range(wu_ref.shape[1] // ff_chunk):
        sl = slice(c * ff_chunk, (c + 1) * ff_chunk)
        a = jnp.maximum(jnp.dot(u, wu_ref[:, sl], preferred_element_type=F32), 0.0)
        acc = acc + jnp.dot((a * a).astype(BF16), wd_ref[sl, :], preferred_element_type=F32)
    o_ref[...] = h + _rms(acc, g3_ref[...])


def _const_spec(shape):
    return pl.BlockSpec(shape, lambda *_: (0,) * len(shape))


def _layer(x, attn_pre_g, w_in, mla_q_norm_g, w_mla_q_up, mla_kv_norm_g, w_mla_kv_up,
           w_out, attn_post_g, mlp_pre_g, w_up, w_down, mlp_post_g):
    B, S, D = x.shape
    T = B * S
    assert S % (2 * MOBA_BLOCK) == 0
    n_blocks = S // MOBA_BLOCK
    row = lambda g: g.reshape(1, -1).astype(F32)

    cols_rm, cols_fm = _w_in_columns()
    win = _gather_columns(w_in, cols_rm)
    wint = _gather_columns(w_in, cols_fm).T
    wqt = _gather_columns(w_mla_q_up, _mla_q_columns()).T
    wk = _gather_columns(w_mla_kv_up, _mla_k_columns())
    wvt = _gather_columns(w_mla_kv_up, _mla_v_columns()).T
    tables = _rope_tables(S)
    tables_t = tuple(t.T for t in tables)

    tm = PROJ_TILE
    n_st = S // tm
    tok = lambda w: pl.BlockSpec((tm, w), lambda i: (i, 0))
    tok_t = lambda w: pl.BlockSpec((1, w, tm), lambda i: (i // n_st, 0, i % n_st))
    pos = pl.BlockSpec((tm, LANES), lambda i: (i % n_st, 0))
    pos_t = pl.BlockSpec((LANES, tm), lambda i: (0, i % n_st))
    x2 = x.reshape(T, D)
    log2e = math.log2(math.e)
    mqt, mk, mvt, oqt, oqft, ok, ovt, kmean = pl.pallas_call(
        functools.partial(_proj_kernel,
                          mla_scale=float((MLA_NOPE_DIM + MLA_ROPE_DIM) ** -0.5 * log2e),
                          moba_scale=float(MOBA_HEAD_DIM ** -0.5 * log2e)),
        grid=(T // tm,),
        in_specs=[tok(D), _const_spec((1, D)), _const_spec(win.shape), _const_spec(wint.shape),
                  _const_spec((1, MLA_Q_RANK)), _const_spec(wqt.shape),
                  _const_spec((1, MLA_KV_RANK)), _const_spec(wk.shape), _const_spec(wvt.shape),
                  pos, pos, pos, pos, pos_t, pos_t, pos_t, pos_t],
        out_specs=[tok_t(MLA_PAD_WIDTH), tok(MLA_PAD_WIDTH), tok_t(MLA_WIDTH),
                   tok_t(MOBA_WIDTH), tok_t(MOBA_WIDTH), tok(MOBA_WIDTH), tok_t(MOBA_WIDTH),
                   pl.BlockSpec((1, 1, MOBA_WIDTH), lambda i: (i, 0, 0))],
        out_shape=[jax.ShapeDtypeStruct((B, MLA_PAD_WIDTH, S), BF16),
                   jax.ShapeDtypeStruct((T, MLA_PAD_WIDTH), BF16),
                   jax.ShapeDtypeStruct((B, MLA_WIDTH, S), BF16),
                   jax.ShapeDtypeStruct((B, MOBA_WIDTH, S), BF16),
                   jax.ShapeDtypeStruct((B, MOBA_WIDTH, S), F32),
                   jax.ShapeDtypeStruct((T, MOBA_WIDTH), BF16),
                   jax.ShapeDtypeStruct((B, MOBA_WIDTH, S), BF16),
                   jax.ShapeDtypeStruct((T // tm, 1, MOBA_WIDTH), F32)],
        compiler_params=pltpu.CompilerParams(vmem_limit_bytes=VMEM_LIMIT),
        name="token_projection",
    )(x2, row(attn_pre_g), win, wint, row(mla_q_norm_g), wqt, row(mla_kv_norm_g), wk, wvt,
      *tables, *tables_t)

    tq = ATTN_TILE
    q_tile_t = lambda w: pl.BlockSpec((1, w, tq), lambda b, p, i: (b, p, i))
    keys = lambda w: pl.BlockSpec((1, S, w), lambda b, p, i: (b, 0, p))
    values_t = pl.BlockSpec((1, LANES, S), lambda b, p, i: (b, p, 0))
    out_tile = pl.BlockSpec((1, tq, LANES), lambda b, p, i: (b, i, p))
    attn_params = pltpu.CompilerParams(vmem_limit_bytes=VMEM_LIMIT)

    mla_o = pl.pallas_call(
        functools.partial(_mla_attn_kernel, tq=tq),
        grid=(B, MLA_HEADS // 2, S // tq),
        in_specs=[q_tile_t(2 * LANES), keys(2 * LANES), values_t],
        out_specs=out_tile,
        out_shape=jax.ShapeDtypeStruct((B, S, MLA_WIDTH), BF16),
        compiler_params=attn_params,
        name="mla_attention",
    )(mqt, mk.reshape(B, S, MLA_PAD_WIDTH), mvt)

    moba_o = pl.pallas_call(
        functools.partial(_moba_attn_kernel, tq=tq, n_blocks=n_blocks),
        grid=(B, MOBA_HEADS // 2, S // tq),
        in_specs=[q_tile_t(LANES), q_tile_t(LANES), keys(LANES), values_t,
                  pl.BlockSpec((1, n_blocks, LANES), lambda b, p, i: (b, 0, p))],
        out_specs=out_tile,
        out_shape=jax.ShapeDtypeStruct((B, S, MOBA_WIDTH), BF16),
        scratch_shapes=[pltpu.VMEM((n_blocks, 2 * tq), F32)],
        compiler_params=attn_params,
        name="moba_attention",
    )(oqt, oqft, ok.reshape(B, S, MOBA_WIDTH), ovt, kmean.reshape(B, n_blocks, MOBA_WIDTH))

    tm = MLP_TILE
    d_ff = w_up.shape[1]
    tok = lambda w: pl.BlockSpec((tm, w), lambda i: (i, 0))
    out = pl.pallas_call(
        functools.partial(_mlp_kernel, ff_chunk=FF_CHUNK),
        grid=(T // tm,),
        in_specs=[tok(D), tok(MLA_WIDTH), tok(MOBA_WIDTH),
                  _const_spec((MLA_WIDTH + MOBA_WIDTH, D)), _const_spec((1, D)), _const_spec((1, D)),
                  _const_spec((D, d_ff)), _const_spec((d_ff, D)), _const_spec((1, D))],
        out_specs=tok(D),
        out_shape=jax.ShapeDtypeStruct((T, D), F32),
        compiler_params=pltpu.CompilerParams(vmem_limit_bytes=VMEM_LIMIT),
        name="out_proj_mlp",
    )(x2, mla_o.reshape(T, MLA_WIDTH), moba_o.reshape(T, MOBA_WIDTH),
      w_out.astype(BF16), row(attn_post_g), row(mlp_pre_g),
      w_up.astype(BF16), w_down.astype(BF16), row(mlp_post_g))
    return out.reshape(B, S, D)


def kernel(x, attn_pre_g, w_in, mla_q_norm_g, w_mla_q_up, mla_kv_norm_g, w_mla_kv_up, w_out,
           attn_post_g, mlp_pre_g, w_up, w_down, mlp_post_g):
    h = x
    for l in range(w_in.shape[0]):
        h = _layer(h, attn_pre_g[l], w_in[l], mla_q_norm_g[l], w_mla_q_up[l], mla_kv_norm_g[l],
                   w_mla_kv_up[l], w_out[l], attn_post_g[l], mlp_pre_g[l], w_up[l], w_down[l],
                   mlp_post_g[l])
    return h
```

```python
import functools
import math

import numpy as np
import jax
import jax.numpy as jnp
from jax import lax
from jax.experimental import pallas as pl
from jax.experimental.pallas import tpu as pltpu

MLA_HEADS = 8
MLA_NOPE_DIM = 64
MLA_ROPE_DIM = 32
MLA_V_DIM = 64
MLA_Q_RANK = 256
MLA_KV_RANK = 128
MOBA_HEADS = 8
MOBA_HEAD_DIM = 64
MOBA_BLOCK = 256
MOBA_TOPK = 3
ROPE_THETA = 10000.0
NORM_EPS = 1e-6
NEG_INF = -1e30

LANES = 128
HALF = LANES // 2
MOBA_WIDTH = MOBA_HEADS * MOBA_HEAD_DIM
MLA_WIDTH = MLA_HEADS * MLA_V_DIM
MLA_PAD_WIDTH = MLA_HEADS * LANES
PROJ_TILE = MOBA_BLOCK
ATTN_TILE = 512
KEY_CHUNK = 512
MLP_TILE = 256
FF_CHUNK = 1024
VMEM_LIMIT = 56 * 1024 * 1024

F32 = jnp.float32
BF16 = jnp.bfloat16
_NT = (((1,), (1,)), ((), ()))


def _moba_pair_columns(base):
    cols = []
    for p in range(MOBA_HEADS // 2):
        for off in (0, MOBA_HEAD_DIM // 2):
            for h in (2 * p, 2 * p + 1):
                cols += [base + h * MOBA_HEAD_DIM + off + i for i in range(MOBA_HEAD_DIM // 2)]
    return cols


def _w_in_columns():
    s2 = MLA_Q_RANK + MLA_KV_RANK
    s3 = s2 + MLA_ROPE_DIM
    half = MLA_ROPE_DIM // 2
    shared = [-1] * LANES
    for i in range(half):
        shared[32 + i] = s2 + i
        shared[96 + i] = s2 + half + i
    row_major = list(range(0, s2)) + shared + _moba_pair_columns(s3 + MOBA_WIDTH)
    feature_major = _moba_pair_columns(s3) + list(range(s3 + 2 * MOBA_WIDTH, s3 + 3 * MOBA_WIDTH))
    return np.asarray(row_major, np.int32), np.asarray(feature_major, np.int32)


def _mla_q_columns():
    cols = []
    half = MLA_ROPE_DIM // 2
    for h in range(MLA_HEADS):
        b = h * (MLA_NOPE_DIM + MLA_ROPE_DIM)
        cols += [b + i for i in range(32)] + [b + MLA_NOPE_DIM + i for i in range(half)] + [-1] * 16
        cols += [b + 32 + i for i in range(32)] + [b + MLA_NOPE_DIM + half + i for i in range(half)] + [-1] * 16
    return np.asarray(cols, np.int32)


def _mla_k_columns():
    cols = []
    for h in range(MLA_HEADS):
        b = h * (MLA_NOPE_DIM + MLA_V_DIM)
        cols += [b + i for i in range(32)] + [-1] * 32 + [b + 32 + i for i in range(32)] + [-1] * 32
    return np.asarray(cols, np.int32)


def _mla_v_columns():
    cols = []
    for h in range(MLA_HEADS):
        b = h * (MLA_NOPE_DIM + MLA_V_DIM) + MLA_NOPE_DIM
        cols += [b + i for i in range(MLA_V_DIM)]
    return np.asarray(cols, np.int32)


def _gather_columns(w, cols):
    picked = jnp.take(w, jnp.asarray(np.maximum(cols, 0)), axis=1)
    return jnp.where(jnp.asarray(cols >= 0)[None, :], picked, 0.0).astype(BF16)


def _rope_tables(seq):
    pos = jnp.arange(seq, dtype=F32)

    def cs(half):
        inv_freq = 1.0 / (ROPE_THETA ** (jnp.arange(half, dtype=F32) / half))
        ang = pos[:, None] * inv_freq[None, :]
        return jnp.cos(ang), jnp.sin(ang)

    c, s = cs(MLA_ROPE_DIM // 2)
    one = lambda n: jnp.ones((seq, n), F32)
    zero = lambda n: jnp.zeros((seq, n), F32)
    cos_mla = jnp.concatenate([one(32), c, one(16), one(32), c, one(16)], axis=1)
    sin_mla = jnp.concatenate([zero(32), -s, zero(16), zero(32), s, zero(16)], axis=1)
    c, s = cs(MOBA_HEAD_DIM // 2)
    cos_moba = jnp.concatenate([c, c, c, c], axis=1)
    sin_moba = jnp.concatenate([-s, -s, s, s], axis=1)
    return cos_mla, sin_mla, cos_moba, sin_moba


def _rms(x, g):
    ms = jnp.mean(x * x, axis=-1, keepdims=True)
    return x * lax.rsqrt(ms + NORM_EPS) * g


def _rope_rows(x, cos, sin):
    return x * cos + pltpu.roll(x, HALF, 1) * sin


def _rope_cols(x, cos, sin):
    return x * cos + jnp.concatenate([x[HALF:], x[:HALF]], axis=0) * sin


def _proj_kernel(x_ref, g_ref, win_ref, wint_ref, gq_ref, wqt_ref, gkv_ref, wk_ref, wvt_ref,
                 cm_ref, sm_ref, co_ref, so_ref, cmt_ref, smt_ref, cot_ref, sot_ref,
                 mqt_ref, mk_ref, mvt_ref, oqt_ref, oqft_ref, ok_ref, ovt_ref, kmean_ref,
                 *, mla_scale, moba_scale):
    u = _rms(x_ref[...], g_ref[...]).astype(BF16)

    def proj(lo, hi):
        return jnp.dot(u, win_ref[:, lo:hi], preferred_element_type=F32)

    def proj_t(lo, hi):
        return lax.dot_general(wint_ref[lo:hi, :], u, _NT, preferred_element_type=F32)

    c0 = MLA_Q_RANK
    c1 = c0 + MLA_KV_RANK
    c2 = c1 + LANES
    c3 = c2 + MOBA_WIDTH

    cq = _rms(proj(0, c0), gq_ref[...]).astype(BF16)
    qt = lax.dot_general(wqt_ref[...], cq, _NT, preferred_element_type=F32)
    cmt, smt = cmt_ref[...], smt_ref[...]
    for h in range(MLA_HEADS):
        sl = slice(h * LANES, (h + 1) * LANES)
        mqt_ref[0, sl, :] = (_rope_cols(qt[sl], cmt, smt) * mla_scale).astype(BF16)

    ckv = _rms(proj(c0, c1), gkv_ref[...]).astype(BF16)
    k_shared = _rope_rows(proj(c1, c2), cm_ref[...], sm_ref[...])
    k_nope = jnp.dot(ckv, wk_ref[...], preferred_element_type=F32)
    for h in range(MLA_HEADS):
        sl = slice(h * LANES, (h + 1) * LANES)
        mk_ref[:, sl] = (k_nope[:, sl] + k_shared).astype(BF16)
    mvt_ref[0] = lax.dot_general(wvt_ref[...], ckv, _NT, preferred_element_type=F32).astype(BF16)

    oqt = proj_t(0, MOBA_WIDTH)
    ok = proj(c2, c3)
    co, so = co_ref[...], so_ref[...]
    cot, sot = cot_ref[...], sot_ref[...]
    for p in range(MOBA_WIDTH // LANES):
        sl = slice(p * LANES, (p + 1) * LANES)
        q_rot = _rope_cols(oqt[sl], cot, sot) * moba_scale
        oqt_ref[0, sl, :] = q_rot.astype(BF16)
        oqft_ref[0, sl, :] = q_rot
        k_rot = _rope_rows(ok[:, sl], co, so)
        ok_ref[:, sl] = k_rot.astype(BF16)
        kmean_ref[0, :, sl] = jnp.mean(k_rot, axis=0, keepdims=True)
    ovt_ref[0] = proj_t(MOBA_WIDTH, 2 * MOBA_WIDTH).astype(BF16)


def _flash_heads(q_cat, k_ref, vt_ref, s_refs, qi, tq, n_heads, bias_ref=None):
    tk = KEY_CHUNK
    n_full = (qi * tq) // tk
    width = n_heads * tq
    d_v = vt_ref.shape[1] // n_heads
    s_even, s_odd = s_refs

    def scores(c, s_ref):
        start = pl.multiple_of(c * tk, tk)
        s = jnp.dot(k_ref[0, pl.ds(start, tk), :], q_cat, preferred_element_type=F32)
        if bias_ref is not None:
            per_chunk = tk // MOBA_BLOCK
            s = jnp.concatenate(
                [s[j * MOBA_BLOCK:(j + 1) * MOBA_BLOCK] + bias_ref[pl.ds(per_chunk * c + j, 1), :]
                 for j in range(per_chunk)], axis=0)
        s_ref[...] = s
        return jnp.max(s, axis=0, keepdims=True)

    def accumulate(c, state, s, s_max):
        m, l, acc = state
        m_new = jnp.maximum(m, s_max)
        alpha = jnp.exp2(m - m_new)
        p = jnp.exp2(s - m_new)
        l = alpha * l + jnp.sum(p, axis=0, keepdims=True)
        p = p.astype(BF16)
        start = pl.multiple_of(c * tk, tk)
        pv = [jnp.dot(vt_ref[0, h * d_v:(h + 1) * d_v, pl.ds(start, tk)],
                      p[:, h * tq:(h + 1) * tq], preferred_element_type=F32)
              for h in range(n_heads)]
        return m_new, l, alpha * acc + jnp.concatenate(pv, axis=1)

    def two_chunks(i, carry):
        state, max_even = carry
        c = 2 * i
        max_odd = scores(c + 1, s_odd)
        state = accumulate(c, state, s_even[...], max_even)
        max_even = scores(c + 2, s_even)
        state = accumulate(c + 1, state, s_odd[...], max_odd)
        return state, max_even

    def last_chunk(state, s_ref):
        key = lax.broadcasted_iota(jnp.int32, (tk, tq), 0)
        qry = lax.broadcasted_iota(jnp.int32, (tk, tq), 1) + (qi * tq - n_full * tk)
        visible = jnp.concatenate([key <= qry] * n_heads, axis=1)
        s = jnp.where(visible, s_ref[...], NEG_INF)
        _, l, acc = accumulate(n_full, state, s, jnp.max(s, axis=0, keepdims=True))
        return acc * (1.0 / l)

    def odd_finish(state, max_even):
        scores(n_full, s_odd)
        state = accumulate(n_full - 1, state, s_even[...], max_even)
        return last_chunk(state, s_odd)

    init = (jnp.full((1, width), NEG_INF, F32), jnp.zeros((1, width), F32),
            jnp.zeros((d_v, width), F32))
    state, max_even = lax.fori_loop(0, n_full // 2, two_chunks, (init, scores(0, s_even)))
    return lax.cond(n_full % 2 == 1, odd_finish, lambda st, _: last_chunk(st, s_even),
                    state, max_even)


def _heads_to_rows(out_t, n_heads, tq):
    return jnp.concatenate([out_t[:, h * tq:(h + 1) * tq] for h in range(n_heads)], axis=0).T


def _mla_attn_kernel(qt_ref, k_ref, vt_ref, o_ref, s_even, s_odd, *, tq):
    qi = pl.program_id(2)
    zero = jnp.zeros((LANES, tq), BF16)
    q_cat = jnp.concatenate(
        [jnp.concatenate([qt_ref[0, :LANES, :], zero], axis=1),
         jnp.concatenate([zero, qt_ref[0, LANES:, :]], axis=1)], axis=0)
    out_t = _flash_heads(q_cat, k_ref, vt_ref, (s_even, s_odd), qi, tq, 2)
    o_ref[0] = _heads_to_rows(out_t, 2, tq).astype(BF16)


def _moba_attn_kernel(qt_ref, qft_ref, k_ref, vt_ref, km_ref, o_ref, s_even, s_odd, bias_ref,
                      *, tq, n_blocks):
    qi = pl.program_id(2)
    feat = lax.broadcasted_iota(jnp.int32, (LANES, tq), 0)
    blk = lax.broadcasted_iota(jnp.int32, (n_blocks, 2 * tq), 0)
    col = lax.broadcasted_iota(jnp.int32, (1, 2 * tq), 1)
    cur = (qi * tq + (col & (tq - 1))) // MOBA_BLOCK
    mine = [(feat & (MOBA_HEAD_DIM // 2)) == h * (MOBA_HEAD_DIM // 2) for h in range(2)]
    q_cat = jnp.concatenate([jnp.where(m, qt_ref[0], jnp.zeros((), BF16)) for m in mine], axis=1)
    qf_cat = jnp.concatenate([jnp.where(m, qft_ref[0], 0.0) for m in mine], axis=1)
    gate = jnp.dot(km_ref[0], qf_cat, preferred_element_type=F32,
                   precision=lax.Precision.HIGHEST)
    ahead_count = jnp.zeros((n_blocks, 2 * tq), F32)
    for jp in range(n_blocks):
        g = gate[jp:jp + 1, :]
        ahead = (g > gate) | ((g == gate) & (jp < blk))
        ahead_count = ahead_count + jnp.where(ahead & (jp < cur), 1.0, 0.0)
    keep = ((blk < cur) & (ahead_count < MOBA_TOPK)) | (blk == cur)
    bias_ref[...] = jnp.where(keep, 0.0, NEG_INF)
    out_t = _flash_heads(q_cat, k_ref, vt_ref, (s_even, s_odd), qi, tq, 2, bias_ref)
    o_ref[0] = _heads_to_rows(out_t, 2, tq).astype(BF16)


def _mlp_kernel(x_ref, a_ref, b_ref, wo_ref, g1_ref, g2_ref, wu_ref, wd_ref, g3_ref, o_ref,
                *, ff_chunk):
    n_a = a_ref.shape[1]
    y = (jnp.dot(a_ref[...], wo_ref[0:n_a, :], preferred_element_type=F32)
         + jnp.dot(b_ref[...], wo_ref[n_a:, :], preferred_element_type=F32))
    h = x_ref[...] + _rms(y, g1_ref[...])
    u = _rms(h, g2_ref[...]).astype(BF16)
    acc = jnp.zeros(h.shape, F32)
    for c in range(wu_ref.shape[1] // ff_chunk):
        sl = slice(c * ff_chunk, (c + 1) * ff_chunk)
        a = jnp.maximum(jnp.dot(u, wu_ref[:, sl], preferred_element_type=F32), 0.0)
        acc = acc + jnp.dot((a * a).astype(BF16), wd_ref[sl, :], preferred_element_type=F32)
    o_ref[...] = h + _rms(acc, g3_ref[...])


def _const_spec(shape):
    return pl.BlockSpec(shape, lambda *_: (0,) * len(shape))


def _layer(x, attn_pre_g, w_in, mla_q_norm_g, w_mla_q_up, mla_kv_norm_g, w_mla_kv_up,
           w_out, attn_post_g, mlp_pre_g, w_up, w_down, mlp_post_g):
    B, S, D = x.shape
    T = B * S
    assert S % KEY_CHUNK == 0 and KEY_CHUNK % ATTN_TILE == 0 and ATTN_TILE % MOBA_BLOCK == 0
    n_blocks = S // MOBA_BLOCK
    row = lambda g: g.reshape(1, -1).astype(F32)

    cols_rm, cols_fm = _w_in_columns()
    win = _gather_columns(w_in, cols_rm)
    wint = _gather_columns(w_in, cols_fm).T
    wqt = _gather_columns(w_mla_q_up, _mla_q_columns()).T
    wk = _gather_columns(w_mla_kv_up, _mla_k_columns())
    wvt = _gather_columns(w_mla_kv_up, _mla_v_columns()).T
    tables = _rope_tables(S)
    tables_t = tuple(t.T for t in tables)

    tm = PROJ_TILE
    n_st = S // tm
    tok = lambda w: pl.BlockSpec((tm, w), lambda i: (i, 0))
    tok_t = lambda w: pl.BlockSpec((1, w, tm), lambda i: (i // n_st, 0, i % n_st))
    pos = pl.BlockSpec((tm, LANES), lambda i: (i % n_st, 0))
    pos_t = pl.BlockSpec((LANES, tm), lambda i: (0, i % n_st))
    x2 = x.reshape(T, D)
    log2e = math.log2(math.e)
    mqt, mk, mvt, oqt, oqft, ok, ovt, kmean = pl.pallas_call(
        functools.partial(_proj_kernel,
                          mla_scale=float((MLA_NOPE_DIM + MLA_ROPE_DIM) ** -0.5 * log2e),
                          moba_scale=float(MOBA_HEAD_DIM ** -0.5 * log2e)),
        grid=(T // tm,),
        in_specs=[tok(D), _const_spec((1, D)), _const_spec(win.shape), _const_spec(wint.shape),
                  _const_spec((1, MLA_Q_RANK)), _const_spec(wqt.shape),
                  _const_spec((1, MLA_KV_RANK)), _const_spec(wk.shape), _const_spec(wvt.shape),
                  pos, pos, pos, pos, pos_t, pos_t, pos_t, pos_t],
        out_specs=[tok_t(MLA_PAD_WIDTH), tok(MLA_PAD_WIDTH), tok_t(MLA_WIDTH),
                   tok_t(MOBA_WIDTH), tok_t(MOBA_WIDTH), tok(MOBA_WIDTH), tok_t(MOBA_WIDTH),
                   pl.BlockSpec((1, 1, MOBA_WIDTH), lambda i: (i, 0, 0))],
        out_shape=[jax.ShapeDtypeStruct((B, MLA_PAD_WIDTH, S), BF16),
                   jax.ShapeDtypeStruct((T, MLA_PAD_WIDTH), BF16),
                   jax.ShapeDtypeStruct((B, MLA_WIDTH, S), BF16),
                   jax.ShapeDtypeStruct((B, MOBA_WIDTH, S), BF16),
                   jax.ShapeDtypeStruct((B, MOBA_WIDTH, S), F32),
                   jax.ShapeDtypeStruct((T, MOBA_WIDTH), BF16),
                   jax.ShapeDtypeStruct((B, MOBA_WIDTH, S), BF16),
                   jax.ShapeDtypeStruct((T // tm, 1, MOBA_WIDTH), F32)],
        compiler_params=pltpu.CompilerParams(vmem_limit_bytes=VMEM_LIMIT),
        name="token_projection",
    )(x2, row(attn_pre_g), win, wint, row(mla_q_norm_g), wqt, row(mla_kv_norm_g), wk, wvt,
      *tables, *tables_t)

    tq = ATTN_TILE
    q_tile_t = lambda w: pl.BlockSpec((1, w, tq), lambda b, p, i: (b, p, i))
    keys = lambda w: pl.BlockSpec((1, S, w), lambda b, p, i: (b, 0, p))
    values_t = pl.BlockSpec((1, LANES, S), lambda b, p, i: (b, p, 0))
    out_tile = pl.BlockSpec((1, tq, LANES), lambda b, p, i: (b, i, p))
    attn_params = pltpu.CompilerParams(vmem_limit_bytes=VMEM_LIMIT)
    score_bufs = [pltpu.VMEM((KEY_CHUNK, 2 * tq), F32)] * 2

    mla_o = pl.pallas_call(
        functools.partial(_mla_attn_kernel, tq=tq),
        grid=(B, MLA_HEADS // 2, S // tq),
        in_specs=[q_tile_t(2 * LANES), keys(2 * LANES), values_t],
        out_specs=out_tile,
        out_shape=jax.ShapeDtypeStruct((B, S, MLA_WIDTH), BF16),
        scratch_shapes=score_bufs,
        compiler_params=attn_params,
        name="mla_attention",
    )(mqt, mk.reshape(B, S, MLA_PAD_WIDTH), mvt)

    moba_o = pl.pallas_call(
        functools.partial(_moba_attn_kernel, tq=tq, n_blocks=n_blocks),
        grid=(B, MOBA_HEADS // 2, S // tq),
        in_specs=[q_tile_t(LANES), q_tile_t(LANES), keys(LANES), values_t,
                  pl.BlockSpec((1, n_blocks, LANES), lambda b, p, i: (b, 0, p))],
        out_specs=out_tile,
        out_shape=jax.ShapeDtypeStruct((B, S, MOBA_WIDTH), BF16),
        scratch_shapes=score_bufs + [pltpu.VMEM((n_blocks, 2 * tq), F32)],
        compiler_params=attn_params,
        name="moba_attention",
    )(oqt, oqft, ok.reshape(B, S, MOBA_WIDTH), ovt, kmean.reshape(B, n_blocks, MOBA_WIDTH))

    tm = MLP_TILE
    d_ff = w_up.shape[1]
    tok = lambda w: pl.BlockSpec((tm, w), lambda i: (i, 0))
    out = pl.pallas_call(
        functools.partial(_mlp_kernel, ff_chunk=FF_CHUNK),
        grid=(T // tm,),
        in_specs=[tok(D), tok(MLA_WIDTH), tok(MOBA_WIDTH),
                  _const_spec((MLA_WIDTH + MOBA_WIDTH, D)), _const_spec((1, D)), _const_spec((1, D)),
                  _const_spec((D, d_ff)), _const_spec((d_ff, D)), _const_spec((1, D))],
        out_specs=tok(D),
        out_shape=jax.ShapeDtypeStruct((T, D), F32),
        compiler_params=pltpu.CompilerParams(vmem_limit_bytes=VMEM_LIMIT),
        name="out_proj_mlp",
    )(x2, mla_o.reshape(T, MLA_WIDTH), moba_o.reshape(T, MOBA_WIDTH),
      w_out.astype(BF16), row(attn_post_g), row(mlp_pre_g),
      w_up.astype(BF16), w_down.astype(BF16), row(mlp_post_g))
    return out.reshape(B, S, D)


def kernel(x, attn_pre_g, w_in, mla_q_norm_g, w_mla_q_up, mla_kv_norm_g, w_mla_kv_up, w_out,
           attn_post_g, mlp_pre_g, w_up, w_down, mlp_post_g):
    h = x
    for l in range(w_in.shape[0]):
        h = _layer(h, attn_pre_g[l], w_in[l], mla_q_norm_g[l], w_mla_q_up[l], mla_kv_norm_g[l],
                   w_mla_kv_up[l], w_out[l], attn_post_g[l], mlp_pre_g[l], w_up[l], w_down[l],
                   mlp_post_g[l])
    return h
```

```python
import functools
import math

import numpy as np
import jax
import jax.numpy as jnp
from jax import lax
from jax.experimental import pallas as pl
from jax.experimental.pallas import tpu as pltpu

MLA_HEADS = 8
MLA_NOPE_DIM = 64
MLA_ROPE_DIM = 32
MLA_V_DIM = 64
MLA_Q_RANK = 256
MLA_KV_RANK = 128
MOBA_HEADS = 8
MOBA_HEAD_DIM = 64
MOBA_BLOCK = 256
MOBA_TOPK = 3
ROPE_THETA = 10000.0
NORM_EPS = 1e-6
NEG_INF = -1e30

LANES = 128
SUBLANES_BF16 = 16
HALF = LANES // 2
MOBA_WIDTH = MOBA_HEADS * MOBA_HEAD_DIM
MLA_WIDTH = MLA_HEADS * MLA_V_DIM
MLA_PAD_WIDTH = MLA_HEADS * LANES
PROJ_TILE = MOBA_BLOCK
ATTN_TILE = 512
KEY_CHUNK = 512
MLP_TILE = 256
FF_CHUNK = 1024
VMEM_LIMIT = 56 * 1024 * 1024

F32 = jnp.float32
BF16 = jnp.bfloat16
_NT = (((1,), (1,)), ((), ()))


def _moba_pair_columns(base):
    cols = []
    for p in range(MOBA_HEADS // 2):
        for off in (0, MOBA_HEAD_DIM // 2):
            for h in (2 * p, 2 * p + 1):
                cols += [base + h * MOBA_HEAD_DIM + off + i for i in range(MOBA_HEAD_DIM // 2)]
    return cols


def _w_in_columns():
    s2 = MLA_Q_RANK + MLA_KV_RANK
    s3 = s2 + MLA_ROPE_DIM
    half = MLA_ROPE_DIM // 2
    shared = [-1] * LANES
    for i in range(half):
        shared[32 + i] = s2 + i
        shared[96 + i] = s2 + half + i
    row_major = list(range(0, s2)) + shared + _moba_pair_columns(s3 + MOBA_WIDTH)
    feature_major = _moba_pair_columns(s3) + list(range(s3 + 2 * MOBA_WIDTH, s3 + 3 * MOBA_WIDTH))
    return np.asarray(row_major, np.int32), np.asarray(feature_major, np.int32)


def _mla_q_columns():
    cols = []
    half = MLA_ROPE_DIM // 2
    for h in range(MLA_HEADS):
        b = h * (MLA_NOPE_DIM + MLA_ROPE_DIM)
        cols += [b + i for i in range(32)] + [b + MLA_NOPE_DIM + i for i in range(half)] + [-1] * 16
        cols += [b + 32 + i for i in range(32)] + [b + MLA_NOPE_DIM + half + i for i in range(half)] + [-1] * 16
    return np.asarray(cols, np.int32)


def _mla_k_columns():
    cols = []
    for h in range(MLA_HEADS):
        b = h * (MLA_NOPE_DIM + MLA_V_DIM)
        cols += [b + i for i in range(32)] + [-1] * 32 + [b + 32 + i for i in range(32)] + [-1] * 32
    return np.asarray(cols, np.int32)


def _mla_v_columns():
    cols = []
    for h in range(MLA_HEADS):
        b = h * (MLA_NOPE_DIM + MLA_V_DIM) + MLA_NOPE_DIM
        cols += [b + i for i in range(MLA_V_DIM)]
    return np.asarray(cols, np.int32)


def _gather_columns(w, cols):
    picked = jnp.take(w, jnp.asarray(np.maximum(cols, 0)), axis=1)
    return jnp.where(jnp.asarray(cols >= 0)[None, :], picked, 0.0).astype(BF16)


def _rope_tables(seq):
    pos = jnp.arange(seq, dtype=F32)

    def cs(half):
        inv_freq = 1.0 / (ROPE_THETA ** (jnp.arange(half, dtype=F32) / half))
        ang = pos[:, None] * inv_freq[None, :]
        return jnp.cos(ang), jnp.sin(ang)

    c, s = cs(MLA_ROPE_DIM // 2)
    one = lambda n: jnp.ones((seq, n), F32)
    zero = lambda n: jnp.zeros((seq, n), F32)
    cos_mla = jnp.concatenate([one(32), c, one(16), one(32), c, one(16)], axis=1)
    sin_mla = jnp.concatenate([zero(32), -s, zero(16), zero(32), s, zero(16)], axis=1)
    c, s = cs(MOBA_HEAD_DIM // 2)
    cos_moba = jnp.concatenate([c, c, c, c], axis=1)
    sin_moba = jnp.concatenate([-s, -s, s, s], axis=1)
    return cos_mla, sin_mla, cos_moba, sin_moba


def _rms(x, g):
    ms = jnp.mean(x * x, axis=-1, keepdims=True)
    return x * lax.rsqrt(ms + NORM_EPS) * g


def _rope_rows(x, cos, sin):
    return x * cos + pltpu.roll(x, HALF, 1) * sin


def _rope_cols(x, cos, sin):
    return x * cos + jnp.concatenate([x[HALF:], x[:HALF]], axis=0) * sin


def _proj_kernel(x_ref, g_ref, win_ref, wint_ref, gq_ref, wqt_ref, gkv_ref, wk_ref, wvt_ref,
                 cm_ref, sm_ref, co_ref, so_ref, cmt_ref, smt_ref, cot_ref, sot_ref,
                 mqt_ref, mk_ref, mvt_ref, oqt_ref, oqft_ref, ok_ref, ovt_ref, kmean_ref,
                 *, mla_scale, moba_scale):
    u = _rms(x_ref[...], g_ref[...]).astype(BF16)

    def proj(lo, hi):
        return jnp.dot(u, win_ref[:, lo:hi], preferred_element_type=F32)

    def proj_t(lo, hi):
        return lax.dot_general(wint_ref[lo:hi, :], u, _NT, preferred_element_type=F32)

    c0 = MLA_Q_RANK
    c1 = c0 + MLA_KV_RANK
    c2 = c1 + LANES
    c3 = c2 + MOBA_WIDTH

    cq = _rms(proj(0, c0), gq_ref[...]).astype(BF16)
    qt = lax.dot_general(wqt_ref[...], cq, _NT, preferred_element_type=F32)
    cmt, smt = cmt_ref[...], smt_ref[...]
    for h in range(MLA_HEADS):
        sl = slice(h * LANES, (h + 1) * LANES)
        mqt_ref[0, sl, :] = (_rope_cols(qt[sl], cmt, smt) * mla_scale).astype(BF16)

    ckv = _rms(proj(c0, c1), gkv_ref[...]).astype(BF16)
    k_shared = _rope_rows(proj(c1, c2), cm_ref[...], sm_ref[...])
    k_nope = jnp.dot(ckv, wk_ref[...], preferred_element_type=F32)
    for h in range(MLA_HEADS):
        sl = slice(h * LANES, (h + 1) * LANES)
        mk_ref[:, sl] = (k_nope[:, sl] + k_shared).astype(BF16)
    mvt_ref[0] = lax.dot_general(wvt_ref[...], ckv, _NT, preferred_element_type=F32).astype(BF16)

    oqt = proj_t(0, MOBA_WIDTH)
    ok = proj(c2, c3)
    co, so = co_ref[...], so_ref[...]
    cot, sot = cot_ref[...], sot_ref[...]
    for p in range(MOBA_WIDTH // LANES):
        sl = slice(p * LANES, (p + 1) * LANES)
        q_rot = _rope_cols(oqt[sl], cot, sot) * moba_scale
        oqt_ref[0, sl, :] = q_rot.astype(BF16)
        oqft_ref[0, sl, :] = q_rot
        k_rot = _rope_rows(ok[:, sl], co, so)
        ok_ref[:, sl] = k_rot.astype(BF16)
        kmean_ref[0, :, sl] = jnp.mean(k_rot, axis=0, keepdims=True)
    ovt_ref[0] = proj_t(MOBA_WIDTH, 2 * MOBA_WIDTH).astype(BF16)


def _scores(c, k_ref, q_cat, s_ref, bias_ref):
    start = pl.multiple_of(c * KEY_CHUNK, KEY_CHUNK)
    s = jnp.dot(k_ref[0, pl.ds(start, KEY_CHUNK), :], q_cat, preferred_element_type=F32)
    if bias_ref is not None:
        per_chunk = KEY_CHUNK // MOBA_BLOCK
        s = jnp.concatenate(
            [s[j * MOBA_BLOCK:(j + 1) * MOBA_BLOCK] + bias_ref[pl.ds(per_chunk * c + j, 1), :]
             for j in range(per_chunk)], axis=0)
    s_ref[...] = s
    return jnp.max(s, axis=0, keepdims=True)


def _flash_heads(q_cat, k_ref, vt_ref, s_refs, max0_ref, start_next, qi, tq, n_heads, bias_ref=None):
    tk = KEY_CHUNK
    n_full = (qi * tq) // tk
    width = n_heads * tq
    d_v = vt_ref.shape[1] // n_heads
    s_even, s_odd = s_refs
    ones = jnp.ones((SUBLANES_BF16, tk), BF16)

    def scores(c, s_ref):
        return _scores(c, k_ref, q_cat, s_ref, bias_ref)

    def accumulate(c, state, s, s_max):
        m, acc = state
        m_new = jnp.maximum(m, s_max)
        alpha = jnp.exp2(m - m_new)
        p = jnp.exp2(s - m_new).astype(BF16)
        start = pl.multiple_of(c * tk, tk)
        pv = [jnp.dot(jnp.concatenate([vt_ref[0, h * d_v:(h + 1) * d_v, pl.ds(start, tk)], ones], axis=0),
                      p[:, h * tq:(h + 1) * tq], preferred_element_type=F32)
              for h in range(n_heads)]
        return m_new, alpha * acc + jnp.concatenate(pv, axis=1)

    def two_chunks(i, carry):
        state, max_even = carry
        c = 2 * i
        max_odd = scores(c + 1, s_odd)
        state = accumulate(c, state, s_even[...], max_even)
        max_even = scores(c + 2, s_even)
        state = accumulate(c + 1, state, s_odd[...], max_odd)
        return state, max_even

    def last_chunk(state, s):
        key = lax.broadcasted_iota(jnp.int32, (tk, tq), 0)
        qry = lax.broadcasted_iota(jnp.int32, (tk, tq), 1) + (qi * tq - n_full * tk)
        visible = jnp.concatenate([key <= qry] * n_heads, axis=1)
        s = jnp.where(visible, s, NEG_INF)
        _, acc = accumulate(n_full, state, s, jnp.max(s, axis=0, keepdims=True))
        return acc[:d_v] * (1.0 / acc[d_v:d_v + 1])

    def even_finish(state, max_even):
        s = s_even[...]
        start_next()
        return last_chunk(state, s)

    def odd_finish(state, max_even):
        scores(n_full, s_odd)
        state = accumulate(n_full - 1, state, s_even[...], max_even)
        start_next()
        return last_chunk(state, s_odd[...])

    init = (jnp.full((1, width), NEG_INF, F32), jnp.zeros((d_v + SUBLANES_BF16, width), F32))
    state, max_even = lax.fori_loop(0, n_full // 2, two_chunks, (init, max0_ref[...]))
    return lax.cond(n_full % 2 == 1, odd_finish, even_finish, state, max_even)


def _heads_to_rows(out_t, n_heads, tq):
    return jnp.concatenate([out_t[:, h * tq:(h + 1) * tq] for h in range(n_heads)], axis=0).T


def _mla_q_cat(qt_ref, tq):
    zero = jnp.zeros((LANES, tq), BF16)
    return jnp.concatenate(
        [jnp.concatenate([qt_ref[0, :LANES, :], zero], axis=1),
         jnp.concatenate([zero, qt_ref[0, LANES:, :]], axis=1)], axis=0)


def _mla_attn_kernel(qt_ref, qt_next_ref, k_ref, vt_ref, o_ref, s_even, s_odd, max0_ref, *, tq):
    qi = pl.program_id(2)

    def start(q_ref):
        max0_ref[...] = _scores(0, k_ref, _mla_q_cat(q_ref, tq), s_even, None)

    pl.when(qi == 0)(lambda: start(qt_ref))
    out_t = _flash_heads(_mla_q_cat(qt_ref, tq), k_ref, vt_ref, (s_even, s_odd), max0_ref,
                         lambda: start(qt_next_ref), qi, tq, 2)
    o_ref[0] = _heads_to_rows(out_t, 2, tq).astype(BF16)


def _moba_attn_kernel(qt_ref, qft_ref, qt_next_ref, qft_next_ref, k_ref, vt_ref, km_ref, o_ref,
                      s_even, s_odd, max0_ref, bias_ref, bias_next_ref, *, tq, n_blocks):
    qi = pl.program_id(2)
    feat = lax.broadcasted_iota(jnp.int32, (LANES, tq), 0)
    mine = [(feat & (MOBA_HEAD_DIM // 2)) == h * (MOBA_HEAD_DIM // 2) for h in range(2)]

    def q_cat(q_ref):
        return jnp.concatenate([jnp.where(m, q_ref[0], jnp.zeros((), BF16)) for m in mine], axis=1)

    def gate_bias(qf_ref, tile):
        blk = lax.broadcasted_iota(jnp.int32, (n_blocks, 2 * tq), 0)
        col = lax.broadcasted_iota(jnp.int32, (1, 2 * tq), 1)
        cur = (tile * tq + (col & (tq - 1))) // MOBA_BLOCK
        qf_cat = jnp.concatenate([jnp.where(m, qf_ref[0], 0.0) for m in mine], axis=1)
        gate = jnp.dot(km_ref[0], qf_cat, preferred_element_type=F32,
                       precision=lax.Precision.HIGHEST)
        ahead_count = jnp.zeros((n_blocks, 2 * tq), F32)
        for jp in range(n_blocks):
            g = gate[jp:jp + 1, :]
            ahead = (g > gate) | ((g == gate) & (jp < blk))
            ahead_count = ahead_count + jnp.where(ahead & (jp < cur), 1.0, 0.0)
        keep = ((blk < cur) & (ahead_count < MOBA_TOPK)) | (blk == cur)
        return jnp.where(keep, 0.0, NEG_INF)

    def start(q_ref, qf_ref, tile, b_ref):
        b_ref[...] = gate_bias(qf_ref, tile)
        max0_ref[...] = _scores(0, k_ref, q_cat(q_ref), s_even, b_ref)

    pl.when(qi == 0)(lambda: start(qt_ref, qft_ref, qi, bias_ref))

    @pl.when(qi > 0)
    def _():
        bias_ref[...] = bias_next_ref[...]

    out_t = _flash_heads(q_cat(qt_ref), k_ref, vt_ref, (s_even, s_odd), max0_ref,
                         lambda: start(qt_next_ref, qft_next_ref, qi + 1, bias_next_ref),
                         qi, tq, 2, bias_ref)
    o_ref[0] = _heads_to_rows(out_t, 2, tq).astype(BF16)


def _mlp_kernel(x_ref, a_ref, b_ref, wo_ref, g1_ref, g2_ref, wu_ref, wd_ref, g3_ref, o_ref,
                *, ff_chunk):
    n_a = a_ref.shape[1]
    y = (jnp.dot(a_ref[...], wo_ref[0:n_a, :], preferred_element_type=F32)
         + jnp.dot(b_ref[...], wo_ref[n_a:, :], preferred_element_type=F32))
    h = x_ref[...] + _rms(y, g1_ref[...])
    u = _rms(h, g2_ref[...]).astype(BF16)
    acc = jnp.zeros(h.shape, F32)
    for c in range(wu_ref.shape[1] // ff_chunk):
        sl = slice(c * ff_chunk, (c + 1) * ff_chunk)
        a = jnp.maximum(jnp.dot(u, wu_ref[:, sl], preferred_element_type=F32), 0.0)
        acc = acc + jnp.dot((a * a).astype(BF16), wd_ref[sl, :], preferred_element_type=F32)
    o_ref[...] = h + _rms(acc, g3_ref[...])


def _const_spec(shape):
    return pl.BlockSpec(shape, lambda *_: (0,) * len(shape))


def _layer(x, attn_pre_g, w_in, mla_q_norm_g, w_mla_q_up, mla_kv_norm_g, w_mla_kv_up,
           w_out, attn_post_g, mlp_pre_g, w_up, w_down, mlp_post_g):
    B, S, D = x.shape
    T = B * S
    assert S % KEY_CHUNK == 0 and KEY_CHUNK % ATTN_TILE == 0 and ATTN_TILE % MOBA_BLOCK == 0
    n_blocks = S // MOBA_BLOCK
    row = lambda g: g.reshape(1, -1).astype(F32)

    cols_rm, cols_fm = _w_in_columns()
    win = _gather_columns(w_in, cols_rm)
    wint = _gather_columns(w_in, cols_fm).T
    wqt = _gather_columns(w_mla_q_up, _mla_q_columns()).T
    wk = _gather_columns(w_mla_kv_up, _mla_k_columns())
    wvt = _gather_columns(w_mla_kv_up, _mla_v_columns()).T
    tables = _rope_tables(S)
    tables_t = tuple(t.T for t in tables)

    tm = PROJ_TILE
    n_st = S // tm
    tok = lambda w: pl.BlockSpec((tm, w), lambda i: (i, 0))
    tok_t = lambda w: pl.BlockSpec((1, w, tm), lambda i: (i // n_st, 0, i % n_st))
    pos = pl.BlockSpec((tm, LANES), lambda i: (i % n_st, 0))
    pos_t = pl.BlockSpec((LANES, tm), lambda i: (0, i % n_st))
    x2 = x.reshape(T, D)
    log2e = math.log2(math.e)
    mqt, mk, mvt, oqt, oqft, ok, ovt, kmean = pl.pallas_call(
        functools.partial(_proj_kernel,
                          mla_scale=float((MLA_NOPE_DIM + MLA_ROPE_DIM) ** -0.5 * log2e),
                          moba_scale=float(MOBA_HEAD_DIM ** -0.5 * log2e)),
        grid=(T // tm,),
        in_specs=[tok(D), _const_spec((1, D)), _const_spec(win.shape), _const_spec(wint.shape),
                  _const_spec((1, MLA_Q_RANK)), _const_spec(wqt.shape),
                  _const_spec((1, MLA_KV_RANK)), _const_spec(wk.shape), _const_spec(wvt.shape),
                  pos, pos, pos, pos, pos_t, pos_t, pos_t, pos_t],
        out_specs=[tok_t(MLA_PAD_WIDTH), tok(MLA_PAD_WIDTH), tok_t(MLA_WIDTH),
                   tok_t(MOBA_WIDTH), tok_t(MOBA_WIDTH), tok(MOBA_WIDTH), tok_t(MOBA_WIDTH),
                   pl.BlockSpec((1, 1, MOBA_WIDTH), lambda i: (i, 0, 0))],
        out_shape=[jax.ShapeDtypeStruct((B, MLA_PAD_WIDTH, S), BF16),
                   jax.ShapeDtypeStruct((T, MLA_PAD_WIDTH), BF16),
                   jax.ShapeDtypeStruct((B, MLA_WIDTH, S), BF16),
                   jax.ShapeDtypeStruct((B, MOBA_WIDTH, S), BF16),
                   jax.ShapeDtypeStruct((B, MOBA_WIDTH, S), F32),
                   jax.ShapeDtypeStruct((T, MOBA_WIDTH), BF16),
                   jax.ShapeDtypeStruct((B, MOBA_WIDTH, S), BF16),
                   jax.ShapeDtypeStruct((T // tm, 1, MOBA_WIDTH), F32)],
        compiler_params=pltpu.CompilerParams(vmem_limit_bytes=VMEM_LIMIT),
        name="token_projection",
    )(x2, row(attn_pre_g), win, wint, row(mla_q_norm_g), wqt, row(mla_kv_norm_g), wk, wvt,
      *tables, *tables_t)

    tq = ATTN_TILE
    n_q = S // tq
    q_tile_t = lambda w: pl.BlockSpec((1, w, tq), lambda b, p, i: (b, p, i))
    q_next_t = lambda w: pl.BlockSpec((1, w, tq), lambda b, p, i: (b, p, jnp.minimum(i + 1, n_q - 1)))
    keys = lambda w: pl.BlockSpec((1, S, w), lambda b, p, i: (b, 0, p))
    values_t = pl.BlockSpec((1, LANES, S), lambda b, p, i: (b, p, 0))
    out_tile = pl.BlockSpec((1, tq, LANES), lambda b, p, i: (b, i, p))
    attn_params = pltpu.CompilerParams(vmem_limit_bytes=VMEM_LIMIT,
                                       dimension_semantics=("parallel", "parallel", "arbitrary"))
    score_bufs = [pltpu.VMEM((KEY_CHUNK, 2 * tq), F32)] * 2 + [pltpu.VMEM((1, 2 * tq), F32)]

    mla_o = pl.pallas_call(
        functools.partial(_mla_attn_kernel, tq=tq),
        grid=(B, MLA_HEADS // 2, n_q),
        in_specs=[q_tile_t(2 * LANES), q_next_t(2 * LANES), keys(2 * LANES), values_t],
        out_specs=out_tile,
        out_shape=jax.ShapeDtypeStruct((B, S, MLA_WIDTH), BF16),
        scratch_shapes=score_bufs,
        compiler_params=attn_params,
        name="mla_attention",
    )(mqt, mqt, mk.reshape(B, S, MLA_PAD_WIDTH), mvt)

    moba_o = pl.pallas_call(
        functools.partial(_moba_attn_kernel, tq=tq, n_blocks=n_blocks),
        grid=(B, MOBA_HEADS // 2, n_q),
        in_specs=[q_tile_t(LANES), q_tile_t(LANES), q_next_t(LANES), q_next_t(LANES),
                  keys(LANES), values_t,
                  pl.BlockSpec((1, n_blocks, LANES), lambda b, p, i: (b, 0, p))],
        out_specs=out_tile,
        out_shape=jax.ShapeDtypeStruct((B, S, MOBA_WIDTH), BF16),
        scratch_shapes=score_bufs + [pltpu.VMEM((n_blocks, 2 * tq), F32)] * 2,
        compiler_params=attn_params,
        name="moba_attention",
    )(oqt, oqft, oqt, oqft, ok.reshape(B, S, MOBA_WIDTH), ovt,
      kmean.reshape(B, n_blocks, MOBA_WIDTH))

    tm = MLP_TILE
    d_ff = w_up.shape[1]
    tok = lambda w: pl.BlockSpec((tm, w), lambda i: (i, 0))
    out = pl.pallas_call(
        functools.partial(_mlp_kernel, ff_chunk=FF_CHUNK),
        grid=(T // tm,),
        in_specs=[tok(D), tok(MLA_WIDTH), tok(MOBA_WIDTH),
                  _const_spec((MLA_WIDTH + MOBA_WIDTH, D)), _const_spec((1, D)), _const_spec((1, D)),
                  _const_spec((D, d_ff)), _const_spec((d_ff, D)), _const_spec((1, D))],
        out_specs=tok(D),
        out_shape=jax.ShapeDtypeStruct((T, D), F32),
        compiler_params=pltpu.CompilerParams(vmem_limit_bytes=VMEM_LIMIT),
        name="out_proj_mlp",
    )(x2, mla_o.reshape(T, MLA_WIDTH), moba_o.reshape(T, MOBA_WIDTH),
      w_out.astype(BF16), row(attn_post_g), row(mlp_pre_g),
      w_up.astype(BF16), w_down.astype(BF16), row(mlp_post_g))
    return out.reshape(B, S, D)


def kernel(x, attn_pre_g, w_in, mla_q_norm_g, w_mla_q_up, mla_kv_norm_g, w_mla_kv_up, w_out,
           attn_post_g, mlp_pre_g, w_up, w_down, mlp_post_g):
    h = x
    for l in range(w_in.shape[0]):
        h = _layer(h, attn_pre_g[l], w_in[l], mla_q_norm_g[l], w_mla_q_up[l], mla_kv_norm_g[l],
                   w_mla_kv_up[l], w_out[l], attn_post_g[l], mlp_pre_g[l], w_up[l], w_down[l],
                   mlp_post_g[l])
    return h
```

```python
import functools
import math

import numpy as np
import jax
import jax.numpy as jnp
from jax import lax
from jax.experimental import pallas as pl
from jax.experimental.pallas import tpu as pltpu

MLA_HEADS = 8
MLA_NOPE_DIM = 64
MLA_ROPE_DIM = 32
MLA_V_DIM = 64
MLA_Q_RANK = 256
MLA_KV_RANK = 128
MOBA_HEADS = 8
MOBA_HEAD_DIM = 64
MOBA_BLOCK = 256
MOBA_TOPK = 3
ROPE_THETA = 10000.0
NORM_EPS = 1e-6
NEG_INF = -1e30

LANES = 128
SUBLANES = 8
SUBLANES_BF16 = 16
HALF = LANES // 2
MOBA_WIDTH = MOBA_HEADS * MOBA_HEAD_DIM
MLA_WIDTH = MLA_HEADS * MLA_V_DIM
MLA_PAD_WIDTH = MLA_HEADS * LANES
PROJ_TILE = 2 * MOBA_BLOCK
ATTN_TILE = 512
MLP_TILE = 512
MLP_SUB_TILE = 256
FF_CHUNK = 1024
VMEM_LIMIT = 56 * 1024 * 1024

F32 = jnp.float32
BF16 = jnp.bfloat16
_NT = (((1,), (1,)), ((), ()))


def _moba_pair_columns(base):
    cols = []
    for p in range(MOBA_HEADS // 2):
        for off in (0, MOBA_HEAD_DIM // 2):
            for h in (2 * p, 2 * p + 1):
                cols += [base + h * MOBA_HEAD_DIM + off + i for i in range(MOBA_HEAD_DIM // 2)]
    return cols


def _w_in_columns():
    s2 = MLA_Q_RANK + MLA_KV_RANK
    s3 = s2 + MLA_ROPE_DIM
    half = MLA_ROPE_DIM // 2
    shared = [-1] * LANES
    for i in range(half):
        shared[32 + i] = s2 + i
        shared[96 + i] = s2 + half + i
    row_major = list(range(0, s2)) + shared + _moba_pair_columns(s3 + MOBA_WIDTH)
    feature_major = _moba_pair_columns(s3) + list(range(s3 + 2 * MOBA_WIDTH, s3 + 3 * MOBA_WIDTH))
    return np.asarray(row_major, np.int32), np.asarray(feature_major, np.int32)


def _mla_q_columns():
    cols = []
    half = MLA_ROPE_DIM // 2
    for h in range(MLA_HEADS):
        b = h * (MLA_NOPE_DIM + MLA_ROPE_DIM)
        cols += [b + i for i in range(32)] + [b + MLA_NOPE_DIM + i for i in range(half)] + [-1] * 16
        cols += [b + 32 + i for i in range(32)] + [b + MLA_NOPE_DIM + half + i for i in range(half)] + [-1] * 16
    return np.asarray(cols, np.int32)


def _mla_k_columns():
    cols = []
    for h in range(MLA_HEADS):
        b = h * (MLA_NOPE_DIM + MLA_V_DIM)
        cols += [b + i for i in range(32)] + [-1] * 32 + [b + 32 + i for i in range(32)] + [-1] * 32
    return np.asarray(cols, np.int32)


def _mla_v_columns():
    cols = []
    for h in range(MLA_HEADS):
        b = h * (MLA_NOPE_DIM + MLA_V_DIM) + MLA_NOPE_DIM
        cols += [b + i for i in range(MLA_V_DIM)]
    return np.asarray(cols, np.int32)


def _gather_columns(w, cols):
    picked = jnp.take(w, jnp.asarray(np.maximum(cols, 0)), axis=1)
    return jnp.where(jnp.asarray(cols >= 0)[None, :], picked, 0.0).astype(BF16)


def _rope_tables(seq):
    pos = jnp.arange(seq, dtype=F32)

    def cs(half):
        inv_freq = 1.0 / (ROPE_THETA ** (jnp.arange(half, dtype=F32) / half))
        ang = pos[:, None] * inv_freq[None, :]
        return jnp.cos(ang), jnp.sin(ang)

    c, s = cs(MLA_ROPE_DIM // 2)
    one = lambda n: jnp.ones((seq, n), F32)
    zero = lambda n: jnp.zeros((seq, n), F32)
    cos_mla = jnp.concatenate([one(32), c, one(16), one(32), c, one(16)], axis=1)
    sin_mla = jnp.concatenate([zero(32), -s, zero(16), zero(32), s, zero(16)], axis=1)
    c, s = cs(MOBA_HEAD_DIM // 2)
    cos_moba = jnp.concatenate([c, c, c, c], axis=1)
    sin_moba = jnp.concatenate([-s, -s, s, s], axis=1)
    return cos_mla, sin_mla, cos_moba, sin_moba


def _rms(x, g):
    ms = jnp.mean(x * x, axis=-1, keepdims=True)
    return x * lax.rsqrt(ms + NORM_EPS) * g


def _rope_rows(x, cos, sin):
    return x * cos + pltpu.roll(x, HALF, 1) * sin


def _rope_cols(x, cos, sin):
    return x * cos + jnp.concatenate([x[HALF:], x[:HALF]], axis=0) * sin


def _proj_kernel(x_ref, g_ref, win_ref, wint_ref, gq_ref, wqt_ref, gkv_ref, wk_ref, wvt_ref,
                 cm_ref, sm_ref, co_ref, so_ref, cmt_ref, smt_ref, cot_ref, sot_ref,
                 mqt_ref, mk_ref, mvt_ref, oqt_ref, oqft_ref, ok_ref, ovt_ref, kmean_ref,
                 *, mla_scale, moba_scale):
    for r in range(x_ref.shape[0] // MOBA_BLOCK):
        _proj_rows(r, slice(r * MOBA_BLOCK, (r + 1) * MOBA_BLOCK),
                   x_ref, g_ref, win_ref, wint_ref, gq_ref, wqt_ref, gkv_ref, wk_ref, wvt_ref,
                   cm_ref, sm_ref, co_ref, so_ref, cmt_ref, smt_ref, cot_ref, sot_ref,
                   mqt_ref, mk_ref, mvt_ref, oqt_ref, oqft_ref, ok_ref, ovt_ref, kmean_ref,
                   mla_scale, moba_scale)


def _proj_rows(r, rows, x_ref, g_ref, win_ref, wint_ref, gq_ref, wqt_ref, gkv_ref, wk_ref, wvt_ref,
               cm_ref, sm_ref, co_ref, so_ref, cmt_ref, smt_ref, cot_ref, sot_ref,
               mqt_ref, mk_ref, mvt_ref, oqt_ref, oqft_ref, ok_ref, ovt_ref, kmean_ref,
               mla_scale, moba_scale):
    u = _rms(x_ref[rows, :], g_ref[...]).astype(BF16)

    def proj(lo, hi):
        return jnp.dot(u, win_ref[:, lo:hi], preferred_element_type=F32)

    def proj_t(lo, hi):
        return lax.dot_general(wint_ref[lo:hi, :], u, _NT, preferred_element_type=F32)

    c0 = MLA_Q_RANK
    c1 = c0 + MLA_KV_RANK
    c2 = c1 + LANES
    c3 = c2 + MOBA_WIDTH

    cq = _rms(proj(0, c0), gq_ref[...]).astype(BF16)
    qt = lax.dot_general(wqt_ref[...], cq, _NT, preferred_element_type=F32)
    cmt, smt = cmt_ref[:, rows], smt_ref[:, rows]
    for h in range(MLA_HEADS):
        sl = slice(h * LANES, (h + 1) * LANES)
        mqt_ref[0, sl, rows] = (_rope_cols(qt[sl], cmt, smt) * mla_scale).astype(BF16)

    ckv = _rms(proj(c0, c1), gkv_ref[...]).astype(BF16)
    k_shared = _rope_rows(proj(c1, c2), cm_ref[rows, :], sm_ref[rows, :])
    k_nope = jnp.dot(ckv, wk_ref[...], preferred_element_type=F32)
    for h in range(MLA_HEADS):
        sl = slice(h * LANES, (h + 1) * LANES)
        mk_ref[rows, sl] = (k_nope[:, sl] + k_shared).astype(BF16)
    mvt_ref[0, :, rows] = lax.dot_general(wvt_ref[...], ckv, _NT,
                                          preferred_element_type=F32).astype(BF16)

    oqt = proj_t(0, MOBA_WIDTH)
    ok = proj(c2, c3)
    co, so = co_ref[rows, :], so_ref[rows, :]
    cot, sot = cot_ref[:, rows], sot_ref[:, rows]
    for p in range(MOBA_WIDTH // LANES):
        sl = slice(p * LANES, (p + 1) * LANES)
        q_rot = _rope_cols(oqt[sl], cot, sot) * moba_scale
        oqt_ref[0, sl, rows] = q_rot.astype(BF16)
        oqft_ref[0, sl, rows] = q_rot
        k_rot = _rope_rows(ok[:, sl], co, so)
        ok_ref[rows, sl] = k_rot.astype(BF16)
        kmean_ref[r, :, sl] = jnp.mean(k_rot, axis=0, keepdims=True)
    ovt_ref[0, :, rows] = proj_t(MOBA_WIDTH, 2 * MOBA_WIDTH).astype(BF16)


def _schedule(n_q):
    tiles = [q for q in range(n_q) for _ in range(q)]
    chunks = [c for q in range(n_q) for c in range(q)]
    return np.asarray(tiles + tiles[-1:], np.int32), np.asarray(chunks + chunks[-1:], np.int32)


def _heads_to_rows(out_t, n_heads, tq):
    return jnp.concatenate([out_t[:, h * tq:(h + 1) * tq] for h in range(n_heads)], axis=0).T


def _attend_row(tile_tab, chunk_tab, q_cat_of, k_ref, vt_ref, o_ref, s_bufs, st_ref, tri_ref,
                n_heads, bias_of=None):
    tq = tk = ATTN_TILE
    n_q = st_ref.shape[0]
    n_steps = n_q * (n_q - 1) // 2
    unroll = 4 if n_steps % 4 == 0 else 2
    width = n_heads * tq
    d_v = vt_ref.shape[1] // n_heads
    acc_rows = d_v + SUBLANES_BF16
    ones = jnp.ones((SUBLANES_BF16, tk), BF16)

    def scores(tile, chunk, s_ref, diagonal):
        start = pl.multiple_of(chunk * tk, tk)
        s = jnp.dot(k_ref[0, pl.ds(start, tk), :], q_cat_of(tile), preferred_element_type=F32)
        if bias_of is not None:
            per_chunk = tk // MOBA_BLOCK
            s = jnp.concatenate(
                [s[j * MOBA_BLOCK:(j + 1) * MOBA_BLOCK] + bias_of(tile, per_chunk * chunk + j)
                 for j in range(per_chunk)], axis=0)
        if diagonal:
            tri = tri_ref[...]
            s = jnp.concatenate([s[:, h * tq:(h + 1) * tq] + tri for h in range(n_heads)], axis=1)
        s_ref[...] = s
        return jnp.max(s, axis=0, keepdims=True)

    def accumulate(chunk, state, s, s_max):
        m, acc = state
        m_new = jnp.maximum(m, s_max)
        alpha = jnp.exp2(m - m_new)
        p = jnp.exp2(s - m_new).astype(BF16)
        start = pl.multiple_of(chunk * tk, tk)
        pv = [jnp.dot(jnp.concatenate([vt_ref[0, h * d_v:(h + 1) * d_v, pl.ds(start, tk)], ones], axis=0),
                      p[:, h * tq:(h + 1) * tq], preferred_element_type=F32)
              for h in range(n_heads)]
        return m_new, alpha * acc + jnp.concatenate(pv, axis=1)

    def save(tile, state):
        m, acc = state
        st_ref[tile, :acc_rows, :] = acc
        st_ref[tile, acc_rows:, :] = jnp.broadcast_to(m, (SUBLANES, width))

    def full_steps(i, carry):
        state, s_max = carry
        for j in range(unroll):
            t = i * unroll + j
            next_max = scores(tile_tab[t + 1], chunk_tab[t + 1], s_bufs[(j + 1) % 2], False)
            tile, chunk = tile_tab[t], chunk_tab[t]
            m, acc = state
            fresh = chunk == 0
            state = accumulate(chunk, (jnp.where(fresh, NEG_INF, m), jnp.where(fresh, 0.0, acc)),
                               s_bufs[j % 2][...], s_max)
            save(tile, state)
            s_max = next_max
        return state, s_max

    def diagonal_steps(i, s_max):
        for j in range(2):
            tile = 2 * i + j
            following = jnp.minimum(tile + 1, n_q - 1)
            next_max = scores(following, following, s_bufs[(j + 1) % 2], True)
            state = (st_ref[tile, acc_rows:acc_rows + 1, :], st_ref[tile, :acc_rows, :])
            _, acc = accumulate(tile, state, s_bufs[j % 2][...], s_max)
            out_t = acc[:d_v] * (1.0 / acc[d_v:d_v + 1])
            o_ref[0, pl.ds(pl.multiple_of(tile * tq, tq), tq), :] = (
                _heads_to_rows(out_t, n_heads, tq).astype(BF16))
            s_max = next_max
        return s_max

    empty = (jnp.full((1, width), NEG_INF, F32), jnp.zeros((acc_rows, width), F32))
    save(0, empty)
    lax.fori_loop(0, n_steps // unroll, full_steps,
                  (empty, scores(tile_tab[0], chunk_tab[0], s_bufs[0], False)))
    lax.fori_loop(0, n_q // 2, diagonal_steps, scores(0, 0, s_bufs[0], True))


def _init_causal_bias(tri_ref):
    @pl.when((pl.program_id(0) == 0) & (pl.program_id(1) == 0))
    def _():
        key = lax.broadcasted_iota(jnp.int32, tri_ref.shape, 0)
        qry = lax.broadcasted_iota(jnp.int32, tri_ref.shape, 1)
        tri_ref[...] = jnp.where(key <= qry, 0.0, NEG_INF)


def _mla_attn_kernel(tile_tab, chunk_tab, qt_ref, k_ref, vt_ref, o_ref, s_a, s_b, st_ref, tri_ref):
    tq = ATTN_TILE
    _init_causal_bias(tri_ref)

    def q_cat_of(tile):
        q = qt_ref[0, :, pl.ds(pl.multiple_of(tile * tq, tq), tq)]
        zero = jnp.zeros((LANES, tq), BF16)
        return jnp.concatenate([jnp.concatenate([q[:LANES], zero], axis=1),
                                jnp.concatenate([zero, q[LANES:]], axis=1)], axis=0)

    _attend_row(tile_tab, chunk_tab, q_cat_of, k_ref, vt_ref, o_ref, (s_a, s_b), st_ref, tri_ref, 2)


def _moba_gate_kernel(qft_ref, km_ref, bias_ref, *, n_blocks):
    tq = ATTN_TILE
    tile = pl.program_id(2)
    feat = lax.broadcasted_iota(jnp.int32, (LANES, tq), 0)
    blk = lax.broadcasted_iota(jnp.int32, (n_blocks, 2 * tq), 0)
    col = lax.broadcasted_iota(jnp.int32, (1, 2 * tq), 1)
    cur = (tile * tq + (col & (tq - 1))) // MOBA_BLOCK
    qf_cat = jnp.concatenate(
        [jnp.where((feat & (MOBA_HEAD_DIM // 2)) == h * (MOBA_HEAD_DIM // 2), qft_ref[0], 0.0)
         for h in range(2)], axis=1)
    gate = jnp.dot(km_ref[0], qf_cat, preferred_element_type=F32,
                   precision=lax.Precision.HIGHEST)
    ahead_count = jnp.zeros((n_blocks, 2 * tq), F32)
    for jp in range(n_blocks):
        g = gate[jp:jp + 1, :]
        ahead = (g > gate) | ((g == gate) & (jp < blk))
        ahead_count = ahead_count + jnp.where(ahead & (jp < cur), 1.0, 0.0)
    keep = ((blk < cur) & (ahead_count < MOBA_TOPK)) | (blk == cur)
    bias_ref[0, 0] = jnp.where(keep, 0.0, NEG_INF)


def _moba_attn_kernel(tile_tab, chunk_tab, qt_ref, k_ref, vt_ref, bias_ref, o_ref,
                      s_a, s_b, st_ref, tri_ref, *, n_blocks):
    tq = ATTN_TILE
    _init_causal_bias(tri_ref)
    feat = lax.broadcasted_iota(jnp.int32, (LANES, tq), 0)

    def q_cat_of(tile):
        q = qt_ref[0, :, pl.ds(pl.multiple_of(tile * tq, tq), tq)]
        return jnp.concatenate(
            [jnp.where((feat & (MOBA_HEAD_DIM // 2)) == h * (MOBA_HEAD_DIM // 2), q,
                       jnp.zeros((), BF16)) for h in range(2)], axis=1)

    def bias_of(tile, block):
        return bias_ref[0, 0, pl.ds(tile * n_blocks + block, 1), :]

    _attend_row(tile_tab, chunk_tab, q_cat_of, k_ref, vt_ref, o_ref, (s_a, s_b), st_ref, tri_ref, 2,
                bias_of)


def _mlp_kernel(x_ref, a_ref, b_ref, wo_ref, g1_ref, g2_ref, wu_ref, wd_ref, g3_ref, o_ref,
                *, ff_chunk, sub_tile):
    n_a = a_ref.shape[1]
    for r in range(x_ref.shape[0] // sub_tile):
        rows = slice(r * sub_tile, (r + 1) * sub_tile)
        y = (jnp.dot(a_ref[rows, :], wo_ref[0:n_a, :], preferred_element_type=F32)
             + jnp.dot(b_ref[rows, :], wo_ref[n_a:, :], preferred_element_type=F32))
        h = x_ref[rows, :] + _rms(y, g1_ref[...])
        u = _rms(h, g2_ref[...]).astype(BF16)
        acc = jnp.zeros(h.shape, F32)
        for c in range(wu_ref.shape[1] // ff_chunk):
            sl = slice(c * ff_chunk, (c + 1) * ff_chunk)
            a = jnp.maximum(jnp.dot(u, wu_ref[:, sl], preferred_element_type=F32), 0.0)
            acc = acc + jnp.dot((a * a).astype(BF16), wd_ref[sl, :], preferred_element_type=F32)
        o_ref[rows, :] = h + _rms(acc, g3_ref[...])


def _const_spec(shape):
    return pl.BlockSpec(shape, lambda *_: (0,) * len(shape))


def _layer(x, attn_pre_g, w_in, mla_q_norm_g, w_mla_q_up, mla_kv_norm_g, w_mla_kv_up,
           w_out, attn_post_g, mlp_pre_g, w_up, w_down, mlp_post_g):
    B, S, D = x.shape
    T = B * S
    n_blocks = S // MOBA_BLOCK
    n_q = S // ATTN_TILE
    assert S % ATTN_TILE == 0 and ATTN_TILE % MOBA_BLOCK == 0
    assert n_q % 2 == 0 and (n_q * (n_q - 1) // 2) % 2 == 0
    row = lambda g: g.reshape(1, -1).astype(F32)

    cols_rm, cols_fm = _w_in_columns()
    win = _gather_columns(w_in, cols_rm)
    wint = _gather_columns(w_in, cols_fm).T
    wqt = _gather_columns(w_mla_q_up, _mla_q_columns()).T
    wk = _gather_columns(w_mla_kv_up, _mla_k_columns())
    wvt = _gather_columns(w_mla_kv_up, _mla_v_columns()).T
    tables = _rope_tables(S)
    tables_t = tuple(t.T for t in tables)

    tm = PROJ_TILE
    n_st = S // tm
    tok = lambda w: pl.BlockSpec((tm, w), lambda i: (i, 0))
    tok_t = lambda w: pl.BlockSpec((1, w, tm), lambda i: (i // n_st, 0, i % n_st))
    pos = pl.BlockSpec((tm, LANES), lambda i: (i % n_st, 0))
    pos_t = pl.BlockSpec((LANES, tm), lambda i: (0, i % n_st))
    x2 = x.reshape(T, D)
    log2e = math.log2(math.e)
    mqt, mk, mvt, oqt, oqft, ok, ovt, kmean = pl.pallas_call(
        functools.partial(_proj_kernel,
                          mla_scale=float((MLA_NOPE_DIM + MLA_ROPE_DIM) ** -0.5 * log2e),
                          moba_scale=float(MOBA_HEAD_DIM ** -0.5 * log2e)),
        grid=(T // tm,),
        in_specs=[tok(D), _const_spec((1, D)), _const_spec(win.shape), _const_spec(wint.shape),
                  _const_spec((1, MLA_Q_RANK)), _const_spec(wqt.shape),
                  _const_spec((1, MLA_KV_RANK)), _const_spec(wk.shape), _const_spec(wvt.shape),
                  pos, pos, pos, pos, pos_t, pos_t, pos_t, pos_t],
        out_specs=[tok_t(MLA_PAD_WIDTH), tok(MLA_PAD_WIDTH), tok_t(MLA_WIDTH),
                   tok_t(MOBA_WIDTH), tok_t(MOBA_WIDTH), tok(MOBA_WIDTH), tok_t(MOBA_WIDTH),
                   pl.BlockSpec((tm // MOBA_BLOCK, 1, MOBA_WIDTH), lambda i: (i, 0, 0))],
        out_shape=[jax.ShapeDtypeStruct((B, MLA_PAD_WIDTH, S), BF16),
                   jax.ShapeDtypeStruct((T, MLA_PAD_WIDTH), BF16),
                   jax.ShapeDtypeStruct((B, MLA_WIDTH, S), BF16),
                   jax.ShapeDtypeStruct((B, MOBA_WIDTH, S), BF16),
                   jax.ShapeDtypeStruct((B, MOBA_WIDTH, S), F32),
                   jax.ShapeDtypeStruct((T, MOBA_WIDTH), BF16),
                   jax.ShapeDtypeStruct((B, MOBA_WIDTH, S), BF16),
                   jax.ShapeDtypeStruct((T // MOBA_BLOCK, 1, MOBA_WIDTH), F32)],
        compiler_params=pltpu.CompilerParams(vmem_limit_bytes=VMEM_LIMIT),
        name="token_projection",
    )(x2, row(attn_pre_g), win, wint, row(mla_q_norm_g), wqt, row(mla_kv_norm_g), wk, wvt,
      *tables, *tables_t)

    tq = ATTN_TILE
    width = 2 * tq
    tile_tab, chunk_tab = (jnp.asarray(t) for t in _schedule(n_q))
    seq_t = lambda w: pl.BlockSpec((1, w, S), lambda b, p, *_: (b, p, 0))
    seq = lambda w: pl.BlockSpec((1, S, w), lambda b, p, *_: (b, 0, p))
    attn_params = pltpu.CompilerParams(vmem_limit_bytes=VMEM_LIMIT)
    attn_scratch = [pltpu.VMEM((tq, width), F32), pltpu.VMEM((tq, width), F32),
                    pltpu.VMEM((n_q, MLA_V_DIM + SUBLANES_BF16 + SUBLANES, width), F32),
                    pltpu.VMEM((tq, tq), F32)]

    mla_o = pl.pallas_call(
        _mla_attn_kernel,
        grid_spec=pltpu.PrefetchScalarGridSpec(
            num_scalar_prefetch=2, grid=(B, MLA_HEADS // 2),
            in_specs=[seq_t(2 * LANES), seq(2 * LANES), seq_t(LANES)],
            out_specs=seq(LANES), scratch_shapes=attn_scratch),
        out_shape=jax.ShapeDtypeStruct((B, S, MLA_WIDTH), BF16),
        compiler_params=attn_params,
        name="mla_attention",
    )(tile_tab, chunk_tab, mqt, mk.reshape(B, S, MLA_PAD_WIDTH), mvt)

    moba_bias = pl.pallas_call(
        functools.partial(_moba_gate_kernel, n_blocks=n_blocks),
        grid=(B, MOBA_HEADS // 2, n_q),
        in_specs=[pl.BlockSpec((1, LANES, tq), lambda b, p, i: (b, p, i)),
                  pl.BlockSpec((1, n_blocks, LANES), lambda b, p, i: (b, 0, p))],
        out_specs=pl.BlockSpec((1, 1, n_blocks, width), lambda b, p, i: (b, p, i, 0)),
        out_shape=jax.ShapeDtypeStruct((B, MOBA_HEADS // 2, n_q * n_blocks, width), F32),
        name="moba_gate",
    )(oqft, kmean.reshape(B, n_blocks, MOBA_WIDTH))

    moba_o = pl.pallas_call(
        functools.partial(_moba_attn_kernel, n_blocks=n_blocks),
        grid_spec=pltpu.PrefetchScalarGridSpec(
            num_scalar_prefetch=2, grid=(B, MOBA_HEADS // 2),
            in_specs=[seq_t(LANES), seq(LANES), seq_t(LANES),
                      pl.BlockSpec((1, 1, n_q * n_blocks, width), lambda b, p, *_: (b, p, 0, 0))],
            out_specs=seq(LANES), scratch_shapes=attn_scratch),
        out_shape=jax.ShapeDtypeStruct((B, S, MOBA_WIDTH), BF16),
        compiler_params=attn_params,
        name="moba_attention",
    )(tile_tab, chunk_tab, oqt, ok.reshape(B, S, MOBA_WIDTH), ovt, moba_bias)

    tm = MLP_TILE
    d_ff = w_up.shape[1]
    tok = lambda w: pl.BlockSpec((tm, w), lambda i: (i, 0))
    out = pl.pallas_call(
        functools.partial(_mlp_kernel, ff_chunk=FF_CHUNK, sub_tile=MLP_SUB_TILE),
        grid=(T // tm,),
        in_specs=[tok(D), tok(MLA_WIDTH), tok(MOBA_WIDTH),
                  _const_spec((MLA_WIDTH + MOBA_WIDTH, D)), _const_spec((1, D)), _const_spec((1, D)),
                  _const_spec((D, d_ff)), _const_spec((d_ff, D)), _const_spec((1, D))],
        out_specs=tok(D),
        out_shape=jax.ShapeDtypeStruct((T, D), F32),
        compiler_params=pltpu.CompilerParams(vmem_limit_bytes=VMEM_LIMIT),
        name="out_proj_mlp",
    )(x2, mla_o.reshape(T, MLA_WIDTH), moba_o.reshape(T, MOBA_WIDTH),
      w_out.astype(BF16), row(attn_post_g), row(mlp_pre_g),
      w_up.astype(BF16), w_down.astype(BF16), row(mlp_post_g))
    return out.reshape(B, S, D)


def kernel(x, attn_pre_g, w_in, mla_q_norm_g, w_mla_q_up, mla_kv_norm_g, w_mla_kv_up, w_out,
           attn_post_g, mlp_pre_g, w_up, w_down, mlp_post_g):
    h = x
    for l in range(w_in.shape[0]):
        h = _layer(h, attn_pre_g[l], w_in[l], mla_q_norm_g[l], w_mla_q_up[l], mla_kv_norm_g[l],
                   w_mla_kv_up[l], w_out[l], attn_post_g[l], mlp_pre_g[l], w_up[l], w_down[l],
                   mlp_post_g[l])
    return h
```

```python
import functools
import math

import numpy as np
import jax
import jax.numpy as jnp
from jax import lax
from jax.experimental import pallas as pl
from jax.experimental.pallas import tpu as pltpu

MLA_HEADS = 8
MLA_NOPE_DIM = 64
MLA_ROPE_DIM = 32
MLA_V_DIM = 64
MLA_Q_RANK = 256
MLA_KV_RANK = 128
MOBA_HEADS = 8
MOBA_HEAD_DIM = 64
MOBA_BLOCK = 256
MOBA_TOPK = 3
ROPE_THETA = 10000.0
NORM_EPS = 1e-6
NEG_INF = -1e30

LANES = 128
SUBLANES = 8
SUBLANES_BF16 = 16
HALF = LANES // 2
MOBA_WIDTH = MOBA_HEADS * MOBA_HEAD_DIM
MLA_WIDTH = MLA_HEADS * MLA_V_DIM
MLA_PAD_WIDTH = MLA_HEADS * LANES
PROJ_TILE = 2 * MOBA_BLOCK
ATTN_TILE = 512
MLP_TILE = 512
MLP_SUB_TILE = 256
FF_CHUNK = 1024
VMEM_LIMIT = 56 * 1024 * 1024

F32 = jnp.float32
BF16 = jnp.bfloat16
_NT = (((1,), (1,)), ((), ()))


def _moba_pair_columns(base):
    cols = []
    for p in range(MOBA_HEADS // 2):
        for off in (0, MOBA_HEAD_DIM // 2):
            for h in (2 * p, 2 * p + 1):
                cols += [base + h * MOBA_HEAD_DIM + off + i for i in range(MOBA_HEAD_DIM // 2)]
    return cols


def _w_in_columns():
    s2 = MLA_Q_RANK + MLA_KV_RANK
    s3 = s2 + MLA_ROPE_DIM
    half = MLA_ROPE_DIM // 2
    shared = [-1] * LANES
    for i in range(half):
        shared[32 + i] = s2 + i
        shared[96 + i] = s2 + half + i
    row_major = list(range(0, s2)) + shared + _moba_pair_columns(s3 + MOBA_WIDTH)
    feature_major = _moba_pair_columns(s3) + list(range(s3 + 2 * MOBA_WIDTH, s3 + 3 * MOBA_WIDTH))
    return np.asarray(row_major, np.int32), np.asarray(feature_major, np.int32)


def _mla_q_columns():
    cols = []
    half = MLA_ROPE_DIM // 2
    for h in range(MLA_HEADS):
        b = h * (MLA_NOPE_DIM + MLA_ROPE_DIM)
        cols += [b + i for i in range(32)] + [b + MLA_NOPE_DIM + i for i in range(half)] + [-1] * 16
        cols += [b + 32 + i for i in range(32)] + [b + MLA_NOPE_DIM + half + i for i in range(half)] + [-1] * 16
    return np.asarray(cols, np.int32)


def _mla_k_columns():
    cols = []
    for h in range(MLA_HEADS):
        b = h * (MLA_NOPE_DIM + MLA_V_DIM)
        cols += [b + i for i in range(32)] + [-1] * 32 + [b + 32 + i for i in range(32)] + [-1] * 32
    return np.asarray(cols, np.int32)


def _mla_v_columns():
    cols = []
    for h in range(MLA_HEADS):
        b = h * (MLA_NOPE_DIM + MLA_V_DIM) + MLA_NOPE_DIM
        cols += [b + i for i in range(MLA_V_DIM)]
    return np.asarray(cols, np.int32)


def _gather_columns(w, cols):
    picked = jnp.take(w, jnp.asarray(np.maximum(cols, 0)), axis=1)
    return jnp.where(jnp.asarray(cols >= 0)[None, :], picked, 0.0).astype(BF16)


def _rope_tables(seq):
    pos = jnp.arange(seq, dtype=F32)

    def cs(half):
        inv_freq = 1.0 / (ROPE_THETA ** (jnp.arange(half, dtype=F32) / half))
        ang = pos[:, None] * inv_freq[None, :]
        return jnp.cos(ang), jnp.sin(ang)

    c, s = cs(MLA_ROPE_DIM // 2)
    one = lambda n: jnp.ones((seq, n), F32)
    zero = lambda n: jnp.zeros((seq, n), F32)
    cos_mla = jnp.concatenate([one(32), c, one(16), one(32), c, one(16)], axis=1)
    sin_mla = jnp.concatenate([zero(32), -s, zero(16), zero(32), s, zero(16)], axis=1)
    c, s = cs(MOBA_HEAD_DIM // 2)
    cos_moba = jnp.concatenate([c, c, c, c], axis=1)
    sin_moba = jnp.concatenate([-s, -s, s, s], axis=1)
    return cos_mla, sin_mla, cos_moba, sin_moba


def _rms(x, g):
    ms = jnp.mean(x * x, axis=-1, keepdims=True)
    return x * lax.rsqrt(ms + NORM_EPS) * g


def _rope_rows(x, cos, sin):
    return x * cos + pltpu.roll(x, HALF, 1) * sin


def _rope_cols(x, cos, sin):
    return x * cos + jnp.concatenate([x[HALF:], x[:HALF]], axis=0) * sin


def _proj_kernel(x_ref, g_ref, win_ref, wint_ref, gq_ref, wqt_ref, gkv_ref, wk_ref, wvt_ref,
                 cm_ref, sm_ref, co_ref, so_ref, cmt_ref, smt_ref, cot_ref, sot_ref,
                 mqt_ref, mk_ref, mvt_ref, oqt_ref, oqft_ref, ok_ref, ovt_ref, kmean_ref,
                 *, mla_scale, moba_scale):
    for r in range(x_ref.shape[0] // MOBA_BLOCK):
        _proj_rows(r, slice(r * MOBA_BLOCK, (r + 1) * MOBA_BLOCK),
                   x_ref, g_ref, win_ref, wint_ref, gq_ref, wqt_ref, gkv_ref, wk_ref, wvt_ref,
                   cm_ref, sm_ref, co_ref, so_ref, cmt_ref, smt_ref, cot_ref, sot_ref,
                   mqt_ref, mk_ref, mvt_ref, oqt_ref, oqft_ref, ok_ref, ovt_ref, kmean_ref,
                   mla_scale, moba_scale)


def _proj_rows(r, rows, x_ref, g_ref, win_ref, wint_ref, gq_ref, wqt_ref, gkv_ref, wk_ref, wvt_ref,
               cm_ref, sm_ref, co_ref, so_ref, cmt_ref, smt_ref, cot_ref, sot_ref,
               mqt_ref, mk_ref, mvt_ref, oqt_ref, oqft_ref, ok_ref, ovt_ref, kmean_ref,
               mla_scale, moba_scale):
    u = _rms(x_ref[rows, :], g_ref[...]).astype(BF16)

    def proj(lo, hi):
        return jnp.dot(u, win_ref[:, lo:hi], preferred_element_type=F32)

    def proj_t(lo, hi):
        return lax.dot_general(wint_ref[lo:hi, :], u, _NT, preferred_element_type=F32)

    c0 = MLA_Q_RANK
    c1 = c0 + MLA_KV_RANK
    c2 = c1 + LANES
    c3 = c2 + MOBA_WIDTH

    cq = _rms(proj(0, c0), gq_ref[...]).astype(BF16)
    qt = lax.dot_general(wqt_ref[...], cq, _NT, preferred_element_type=F32)
    cmt, smt = cmt_ref[:, rows], smt_ref[:, rows]
    for h in range(MLA_HEADS):
        sl = slice(h * LANES, (h + 1) * LANES)
        mqt_ref[0, sl, rows] = (_rope_cols(qt[sl], cmt, smt) * mla_scale).astype(BF16)

    ckv = _rms(proj(c0, c1), gkv_ref[...]).astype(BF16)
    k_shared = _rope_rows(proj(c1, c2), cm_ref[rows, :], sm_ref[rows, :])
    k_nope = jnp.dot(ckv, wk_ref[...], preferred_element_type=F32)
    for h in range(MLA_HEADS):
        sl = slice(h * LANES, (h + 1) * LANES)
        mk_ref[rows, sl] = (k_nope[:, sl] + k_shared).astype(BF16)
    mvt_ref[0, :, rows] = lax.dot_general(wvt_ref[...], ckv, _NT,
                                          preferred_element_type=F32).astype(BF16)

    oqt = proj_t(0, MOBA_WIDTH)
    ok = proj(c2, c3)
    co, so = co_ref[rows, :], so_ref[rows, :]
    cot, sot = cot_ref[:, rows], sot_ref[:, rows]
    for p in range(MOBA_WIDTH // LANES):
        sl = slice(p * LANES, (p + 1) * LANES)
        q_rot = _rope_cols(oqt[sl], cot, sot) * moba_scale
        oqt_ref[0, sl, rows] = q_rot.astype(BF16)
        oqft_ref[0, sl, rows] = q_rot
        k_rot = _rope_rows(ok[:, sl], co, so)
        ok_ref[rows, sl] = k_rot.astype(BF16)
        kmean_ref[r, :, sl] = jnp.mean(k_rot, axis=0, keepdims=True)
    ovt_ref[0, :, rows] = proj_t(MOBA_WIDTH, 2 * MOBA_WIDTH).astype(BF16)


def _schedule(n_q):
    tiles = [q for q in range(n_q) for _ in range(q)]
    chunks = [c for q in range(n_q) for c in range(q)]
    return np.asarray(tiles + tiles[-1:], np.int32), np.asarray(chunks + chunks[-1:], np.int32)


def _heads_to_rows(out_t, n_heads, tq):
    return jnp.concatenate([out_t[:, h * tq:(h + 1) * tq] for h in range(n_heads)], axis=0).T


def _attend_row(tile_tab, chunk_tab, q_cat_of, k_ref, vt_ref, o_ref, s_bufs, st_ref, tri_ref,
                n_heads, bias_of=None):
    tq = tk = ATTN_TILE
    n_q = st_ref.shape[0]
    n_steps = n_q * (n_q - 1) // 2
    unroll = next(u for u in (14, 4, 2) if n_steps % u == 0)
    diag_unroll = 4 if n_q % 4 == 0 else 2
    width = n_heads * tq
    d_v = vt_ref.shape[1] // n_heads
    acc_rows = d_v + SUBLANES_BF16
    ones = jnp.ones((SUBLANES_BF16, tk), BF16)

    def scores(tile, chunk, s_ref, diagonal):
        start = pl.multiple_of(chunk * tk, tk)
        s = jnp.dot(k_ref[0, pl.ds(start, tk), :], q_cat_of(tile), preferred_element_type=F32)
        if bias_of is not None:
            per_chunk = tk // MOBA_BLOCK
            s = jnp.concatenate(
                [s[j * MOBA_BLOCK:(j + 1) * MOBA_BLOCK] + bias_of(tile, per_chunk * chunk + j)
                 for j in range(per_chunk)], axis=0)
        if diagonal:
            tri = tri_ref[...]
            s = jnp.concatenate([s[:, h * tq:(h + 1) * tq] + tri for h in range(n_heads)], axis=1)
        s_ref[...] = s
        return jnp.max(s, axis=0, keepdims=True)

    def accumulate(chunk, state, s, s_max):
        m, acc = state
        m_new = jnp.maximum(m, s_max)
        alpha = jnp.exp2(m - m_new)
        p = jnp.exp2(s - m_new).astype(BF16)
        start = pl.multiple_of(chunk * tk, tk)
        pv = [jnp.dot(jnp.concatenate([vt_ref[0, h * d_v:(h + 1) * d_v, pl.ds(start, tk)], ones], axis=0),
                      p[:, h * tq:(h + 1) * tq], preferred_element_type=F32)
              for h in range(n_heads)]
        return m_new, alpha * acc + jnp.concatenate(pv, axis=1)

    def save(tile, state):
        m, acc = state
        st_ref[tile, :acc_rows, :] = acc
        st_ref[tile, acc_rows:, :] = jnp.broadcast_to(m, (SUBLANES, width))

    def full_steps(i, carry):
        state, s_max = carry
        for j in range(unroll):
            t = i * unroll + j
            next_max = scores(tile_tab[t + 1], chunk_tab[t + 1], s_bufs[(j + 1) % 2], False)
            tile, chunk = tile_tab[t], chunk_tab[t]
            m, acc = state
            fresh = chunk == 0
            state = accumulate(chunk, (jnp.where(fresh, NEG_INF, m), jnp.where(fresh, 0.0, acc)),
                               s_bufs[j % 2][...], s_max)
            save(tile, state)
            s_max = next_max
        return state, s_max

    def diagonal_steps(i, s_max):
        for j in range(diag_unroll):
            tile = diag_unroll * i + j
            following = jnp.minimum(tile + 1, n_q - 1)
            next_max = scores(following, following, s_bufs[(j + 1) % 2], True)
            state = (st_ref[tile, acc_rows:acc_rows + 1, :], st_ref[tile, :acc_rows, :])
            _, acc = accumulate(tile, state, s_bufs[j % 2][...], s_max)
            out_t = acc[:d_v] * (1.0 / acc[d_v:d_v + 1])
            o_ref[0, pl.ds(pl.multiple_of(tile * tq, tq), tq), :] = (
                _heads_to_rows(out_t, n_heads, tq).astype(BF16))
            s_max = next_max
        return s_max

    empty = (jnp.full((1, width), NEG_INF, F32), jnp.zeros((acc_rows, width), F32))
    save(0, empty)
    lax.fori_loop(0, n_steps // unroll, full_steps,
                  (empty, scores(tile_tab[0], chunk_tab[0], s_bufs[0], False)))
    lax.fori_loop(0, n_q // diag_unroll, diagonal_steps, scores(0, 0, s_bufs[0], True))


def _init_causal_bias(tri_ref):
    @pl.when((pl.program_id(0) == 0) & (pl.program_id(1) == 0))
    def _():
        key = lax.broadcasted_iota(jnp.int32, tri_ref.shape, 0)
        qry = lax.broadcasted_iota(jnp.int32, tri_ref.shape, 1)
        tri_ref[...] = jnp.where(key <= qry, 0.0, NEG_INF)


def _mla_attn_kernel(tile_tab, chunk_tab, qt_ref, k_ref, vt_ref, o_ref, s_a, s_b, st_ref, tri_ref):
    tq = ATTN_TILE
    _init_causal_bias(tri_ref)

    def q_cat_of(tile):
        q = qt_ref[0, :, pl.ds(pl.multiple_of(tile * tq, tq), tq)]
        zero = jnp.zeros((LANES, tq), BF16)
        return jnp.concatenate([jnp.concatenate([q[:LANES], zero], axis=1),
                                jnp.concatenate([zero, q[LANES:]], axis=1)], axis=0)

    _attend_row(tile_tab, chunk_tab, q_cat_of, k_ref, vt_ref, o_ref, (s_a, s_b), st_ref, tri_ref, 2)


def _moba_gate_kernel(qft_ref, km_ref, bias_ref, *, n_blocks):
    tq = ATTN_TILE
    feat = lax.broadcasted_iota(jnp.int32, (LANES, tq), 0)
    blk = lax.broadcasted_iota(jnp.int32, (n_blocks, 2 * tq), 0)
    col = lax.broadcasted_iota(jnp.int32, (1, 2 * tq), 1)
    for tile in range(qft_ref.shape[2] // tq):
        cur = (tile * tq + (col & (tq - 1))) // MOBA_BLOCK
        qf = qft_ref[0, :, tile * tq:(tile + 1) * tq]
        qf_cat = jnp.concatenate(
            [jnp.where((feat & (MOBA_HEAD_DIM // 2)) == h * (MOBA_HEAD_DIM // 2), qf, 0.0)
             for h in range(2)], axis=1)
        gate = jnp.dot(km_ref[0], qf_cat, preferred_element_type=F32,
                       precision=lax.Precision.HIGHEST)
        gate = jnp.where(blk < cur, gate, -jnp.inf)
        ahead_count = jnp.zeros((n_blocks, 2 * tq), F32)
        for jp in range(n_blocks):
            g = gate[jp:jp + 1, :]
            ahead = jnp.where(jp < blk, jnp.where(g >= gate, 1.0, 0.0), jnp.where(g > gate, 1.0, 0.0))
            ahead_count = ahead_count + ahead
        keep = ((blk < cur) & (ahead_count < MOBA_TOPK)) | (blk == cur)
        bias_ref[0, 0, tile * n_blocks:(tile + 1) * n_blocks, :] = jnp.where(keep, 0.0, NEG_INF)


def _moba_attn_kernel(tile_tab, chunk_tab, qt_ref, k_ref, vt_ref, bias_ref, o_ref,
                      s_a, s_b, st_ref, tri_ref, *, n_blocks):
    tq = ATTN_TILE
    _init_causal_bias(tri_ref)
    feat = lax.broadcasted_iota(jnp.int32, (LANES, tq), 0)

    def q_cat_of(tile):
        q = qt_ref[0, :, pl.ds(pl.multiple_of(tile * tq, tq), tq)]
        return jnp.concatenate(
            [jnp.where((feat & (MOBA_HEAD_DIM // 2)) == h * (MOBA_HEAD_DIM // 2), q,
                       jnp.zeros((), BF16)) for h in range(2)], axis=1)

    def bias_of(tile, block):
        return bias_ref[0, 0, pl.ds(tile * n_blocks + block, 1), :]

    _attend_row(tile_tab, chunk_tab, q_cat_of, k_ref, vt_ref, o_ref, (s_a, s_b), st_ref, tri_ref, 2,
                bias_of)


def _mlp_kernel(x_ref, a_ref, b_ref, wo_ref, g1_ref, g2_ref, wu_ref, wd_ref, g3_ref, o_ref,
                *, ff_chunk, sub_tile):
    n_a = a_ref.shape[1]
    for r in range(x_ref.shape[0] // sub_tile):
        rows = slice(r * sub_tile, (r + 1) * sub_tile)
        y = (jnp.dot(a_ref[rows, :], wo_ref[0:n_a, :], preferred_element_type=F32)
             + jnp.dot(b_ref[rows, :], wo_ref[n_a:, :], preferred_element_type=F32))
        h = x_ref[rows, :] + _rms(y, g1_ref[...])
        u = _rms(h, g2_ref[...]).astype(BF16)
        acc = jnp.zeros(h.shape, F32)
        for c in range(wu_ref.shape[1] // ff_chunk):
            sl = slice(c * ff_chunk, (c + 1) * ff_chunk)
            a = jnp.maximum(jnp.dot(u, wu_ref[:, sl], preferred_element_type=F32), 0.0)
            acc = acc + jnp.dot((a * a).astype(BF16), wd_ref[sl, :], preferred_element_type=F32)
        o_ref[rows, :] = h + _rms(acc, g3_ref[...])


def _const_spec(shape):
    return pl.BlockSpec(shape, lambda *_: (0,) * len(shape))


def _layer(x, attn_pre_g, w_in, mla_q_norm_g, w_mla_q_up, mla_kv_norm_g, w_mla_kv_up,
           w_out, attn_post_g, mlp_pre_g, w_up, w_down, mlp_post_g):
    B, S, D = x.shape
    T = B * S
    n_blocks = S // MOBA_BLOCK
    n_q = S // ATTN_TILE
    assert S % ATTN_TILE == 0 and ATTN_TILE % MOBA_BLOCK == 0
    assert n_q % 2 == 0 and (n_q * (n_q - 1) // 2) % 2 == 0
    row = lambda g: g.reshape(1, -1).astype(F32)

    cols_rm, cols_fm = _w_in_columns()
    win = _gather_columns(w_in, cols_rm)
    wint = _gather_columns(w_in, cols_fm).T
    wqt = _gather_columns(w_mla_q_up, _mla_q_columns()).T
    wk = _gather_columns(w_mla_kv_up, _mla_k_columns())
    wvt = _gather_columns(w_mla_kv_up, _mla_v_columns()).T
    tables = _rope_tables(S)
    tables_t = tuple(t.T for t in tables)

    tm = PROJ_TILE
    n_st = S // tm
    tok = lambda w: pl.BlockSpec((tm, w), lambda i: (i, 0))
    tok_t = lambda w: pl.BlockSpec((1, w, tm), lambda i: (i // n_st, 0, i % n_st))
    pos = pl.BlockSpec((tm, LANES), lambda i: (i % n_st, 0))
    pos_t = pl.BlockSpec((LANES, tm), lambda i: (0, i % n_st))
    x2 = x.reshape(T, D)
    log2e = math.log2(math.e)
    mqt, mk, mvt, oqt, oqft, ok, ovt, kmean = pl.pallas_call(
        functools.partial(_proj_kernel,
                          mla_scale=float((MLA_NOPE_DIM + MLA_ROPE_DIM) ** -0.5 * log2e),
                          moba_scale=float(MOBA_HEAD_DIM ** -0.5 * log2e)),
        grid=(T // tm,),
        in_specs=[tok(D), _const_spec((1, D)), _const_spec(win.shape), _const_spec(wint.shape),
                  _const_spec((1, MLA_Q_RANK)), _const_spec(wqt.shape),
                  _const_spec((1, MLA_KV_RANK)), _const_spec(wk.shape), _const_spec(wvt.shape),
                  pos, pos, pos, pos, pos_t, pos_t, pos_t, pos_t],
        out_specs=[tok_t(MLA_PAD_WIDTH), tok(MLA_PAD_WIDTH), tok_t(MLA_WIDTH),
                   tok_t(MOBA_WIDTH), tok_t(MOBA_WIDTH), tok(MOBA_WIDTH), tok_t(MOBA_WIDTH),
                   pl.BlockSpec((tm // MOBA_BLOCK, 1, MOBA_WIDTH), lambda i: (i, 0, 0))],
        out_shape=[jax.ShapeDtypeStruct((B, MLA_PAD_WIDTH, S), BF16),
                   jax.ShapeDtypeStruct((T, MLA_PAD_WIDTH), BF16),
                   jax.ShapeDtypeStruct((B, MLA_WIDTH, S), BF16),
                   jax.ShapeDtypeStruct((B, MOBA_WIDTH, S), BF16),
                   jax.ShapeDtypeStruct((B, MOBA_WIDTH, S), F32),
                   jax.ShapeDtypeStruct((T, MOBA_WIDTH), BF16),
                   jax.ShapeDtypeStruct((B, MOBA_WIDTH, S), BF16),
                   jax.ShapeDtypeStruct((T // MOBA_BLOCK, 1, MOBA_WIDTH), F32)],
        compiler_params=pltpu.CompilerParams(vmem_limit_bytes=VMEM_LIMIT),
        name="token_projection",
    )(x2, row(attn_pre_g), win, wint, row(mla_q_norm_g), wqt, row(mla_kv_norm_g), wk, wvt,
      *tables, *tables_t)

    tq = ATTN_TILE
    width = 2 * tq
    tile_tab, chunk_tab = (jnp.asarray(t) for t in _schedule(n_q))
    seq_t = lambda w: pl.BlockSpec((1, w, S), lambda b, p, *_: (b, p, 0))
    seq = lambda w: pl.BlockSpec((1, S, w), lambda b, p, *_: (b, 0, p))
    attn_params = pltpu.CompilerParams(vmem_limit_bytes=VMEM_LIMIT)
    attn_scratch = [pltpu.VMEM((tq, width), F32), pltpu.VMEM((tq, width), F32),
                    pltpu.VMEM((n_q, MLA_V_DIM + SUBLANES_BF16 + SUBLANES, width), F32),
                    pltpu.VMEM((tq, tq), F32)]

    mla_o = pl.pallas_call(
        _mla_attn_kernel,
        grid_spec=pltpu.PrefetchScalarGridSpec(
            num_scalar_prefetch=2, grid=(B, MLA_HEADS // 2),
            in_specs=[seq_t(2 * LANES), seq(2 * LANES), seq_t(LANES)],
            out_specs=seq(LANES), scratch_shapes=attn_scratch),
        out_shape=jax.ShapeDtypeStruct((B, S, MLA_WIDTH), BF16),
        compiler_params=attn_params,
        name="mla_attention",
    )(tile_tab, chunk_tab, mqt, mk.reshape(B, S, MLA_PAD_WIDTH), mvt)

    moba_bias = pl.pallas_call(
        functools.partial(_moba_gate_kernel, n_blocks=n_blocks),
        grid=(B, MOBA_HEADS // 2),
        in_specs=[pl.BlockSpec((1, LANES, S), lambda b, p: (b, p, 0)),
                  pl.BlockSpec((1, n_blocks, LANES), lambda b, p: (b, 0, p))],
        out_specs=pl.BlockSpec((1, 1, n_q * n_blocks, width), lambda b, p: (b, p, 0, 0)),
        out_shape=jax.ShapeDtypeStruct((B, MOBA_HEADS // 2, n_q * n_blocks, width), F32),
        name="moba_gate",
    )(oqft, kmean.reshape(B, n_blocks, MOBA_WIDTH))

    moba_o = pl.pallas_call(
        functools.partial(_moba_attn_kernel, n_blocks=n_blocks),
        grid_spec=pltpu.PrefetchScalarGridSpec(
            num_scalar_prefetch=2, grid=(B, MOBA_HEADS // 2),
            in_specs=[seq_t(LANES), seq(LANES), seq_t(LANES),
                      pl.BlockSpec((1, 1, n_q * n_blocks, width), lambda b, p, *_: (b, p, 0, 0))],
            out_specs=seq(LANES), scratch_shapes=attn_scratch),
        out_shape=jax.ShapeDtypeStruct((B, S, MOBA_WIDTH), BF16),
        compiler_params=attn_params,
        name="moba_attention",
    )(tile_tab, chunk_tab, oqt, ok.reshape(B, S, MOBA_WIDTH), ovt, moba_bias)

    tm = MLP_TILE
    d_ff = w_up.shape[1]
    tok = lambda w: pl.BlockSpec((tm, w), lambda i: (i, 0))
    out = pl.pallas_call(
        functools.partial(_mlp_kernel, ff_chunk=FF_CHUNK, sub_tile=MLP_SUB_TILE),
        grid=(T // tm,),
        in_specs=[tok(D), tok(MLA_WIDTH), tok(MOBA_WIDTH),
                  _const_spec((MLA_WIDTH + MOBA_WIDTH, D)), _const_spec((1, D)), _const_spec((1, D)),
                  _const_spec((D, d_ff)), _const_spec((d_ff, D)), _const_spec((1, D))],
        out_specs=tok(D),
        out_shape=jax.ShapeDtypeStruct((T, D), F32),
        compiler_params=pltpu.CompilerParams(vmem_limit_bytes=VMEM_LIMIT),
        name="out_proj_mlp",
    )(x2, mla_o.reshape(T, MLA_WIDTH), moba_o.reshape(T, MOBA_WIDTH),
      w_out.astype(BF16), row(attn_post_g), row(mlp_pre_g),
      w_up.astype(BF16), w_down.astype(BF16), row(mlp_post_g))
    return out.reshape(B, S, D)


def kernel(x, attn_pre_g, w_in, mla_q_norm_g, w_mla_q_up, mla_kv_norm_g, w_mla_kv_up, w_out,
           attn_post_g, mlp_pre_g, w_up, w_down, mlp_post_g):
    h = x
    for l in range(w_in.shape[0]):
        h = _layer(h, attn_pre_g[l], w_in[l], mla_q_norm_g[l], w_mla_q_up[l], mla_kv_norm_g[l],
                   w_mla_kv_up[l], w_out[l], attn_post_g[l], mlp_pre_g[l], w_up[l], w_down[l],
                   mlp_post_g[l])
    return h
```

```python
import functools
import math

import numpy as np
import jax
import jax.numpy as jnp
from jax import lax
from jax.experimental import pallas as pl
from jax.experimental.pallas import tpu as pltpu

MLA_HEADS = 8
MLA_NOPE_DIM = 64
MLA_ROPE_DIM = 32
MLA_V_DIM = 64
MLA_Q_RANK = 256
MLA_KV_RANK = 128
MOBA_HEADS = 8
MOBA_HEAD_DIM = 64
MOBA_BLOCK = 256
MOBA_TOPK = 3
ROPE_THETA = 10000.0
NORM_EPS = 1e-6
NEG_INF = -1e30

LANES = 128
SUBLANES = 8
SUBLANES_BF16 = 16
HALF = LANES // 2
MOBA_WIDTH = MOBA_HEADS * MOBA_HEAD_DIM
MLA_WIDTH = MLA_HEADS * MLA_V_DIM
MLA_PAD_WIDTH = MLA_HEADS * LANES
PROJ_TILE = 512
PROJ_SUB_TILE = 512
ATTN_TILE = 512
MLP_TILE = 512
MLP_SUB_TILE = 256
FF_CHUNK = 1024
VMEM_LIMIT = 56 * 1024 * 1024

F32 = jnp.float32
BF16 = jnp.bfloat16
_NT = (((1,), (1,)), ((), ()))


def _moba_pair_columns(base):
    cols = []
    for p in range(MOBA_HEADS // 2):
        for off in (0, MOBA_HEAD_DIM // 2):
            for h in (2 * p, 2 * p + 1):
                cols += [base + h * MOBA_HEAD_DIM + off + i for i in range(MOBA_HEAD_DIM // 2)]
    return cols


def _w_in_columns():
    s2 = MLA_Q_RANK + MLA_KV_RANK
    s3 = s2 + MLA_ROPE_DIM
    half = MLA_ROPE_DIM // 2
    shared = [-1] * LANES
    for i in range(half):
        shared[32 + i] = s2 + i
        shared[96 + i] = s2 + half + i
    row_major = list(range(0, s2)) + shared + _moba_pair_columns(s3 + MOBA_WIDTH)
    feature_major = _moba_pair_columns(s3) + list(range(s3 + 2 * MOBA_WIDTH, s3 + 3 * MOBA_WIDTH))
    return np.asarray(row_major, np.int32), np.asarray(feature_major, np.int32)


def _mla_q_columns():
    cols = []
    half = MLA_ROPE_DIM // 2
    for h in range(MLA_HEADS):
        b = h * (MLA_NOPE_DIM + MLA_ROPE_DIM)
        cols += [b + i for i in range(32)] + [b + MLA_NOPE_DIM + i for i in range(half)] + [-1] * 16
        cols += [b + 32 + i for i in range(32)] + [b + MLA_NOPE_DIM + half + i for i in range(half)] + [-1] * 16
    return np.asarray(cols, np.int32)


def _mla_k_columns():
    cols = []
    for h in range(MLA_HEADS):
        b = h * (MLA_NOPE_DIM + MLA_V_DIM)
        cols += [b + i for i in range(32)] + [-1] * 32 + [b + 32 + i for i in range(32)] + [-1] * 32
    return np.asarray(cols, np.int32)


def _mla_v_columns():
    cols = []
    for h in range(MLA_HEADS):
        b = h * (MLA_NOPE_DIM + MLA_V_DIM) + MLA_NOPE_DIM
        cols += [b + i for i in range(MLA_V_DIM)]
    return np.asarray(cols, np.int32)


def _gather_columns(w, cols):
    picked = jnp.take(w, jnp.asarray(np.maximum(cols, 0)), axis=1)
    return jnp.where(jnp.asarray(cols >= 0)[None, :], picked, 0.0).astype(BF16)


def _rope_tables(seq):
    pos = jnp.arange(seq, dtype=F32)

    def cs(half):
        inv_freq = 1.0 / (ROPE_THETA ** (jnp.arange(half, dtype=F32) / half))
        ang = pos[:, None] * inv_freq[None, :]
        return jnp.cos(ang), jnp.sin(ang)

    c, s = cs(MLA_ROPE_DIM // 2)
    one = lambda n: jnp.ones((seq, n), F32)
    zero = lambda n: jnp.zeros((seq, n), F32)
    cos_mla = jnp.concatenate([one(32), c, one(16), one(32), c, one(16)], axis=1)
    sin_mla = jnp.concatenate([zero(32), -s, zero(16), zero(32), s, zero(16)], axis=1)
    c, s = cs(MOBA_HEAD_DIM // 2)
    cos_moba = jnp.concatenate([c, c, c, c], axis=1)
    sin_moba = jnp.concatenate([-s, -s, s, s], axis=1)
    return cos_mla, sin_mla, cos_moba, sin_moba


def _rms(x, g):
    ms = jnp.mean(x * x, axis=-1, keepdims=True)
    return x * lax.rsqrt(ms + NORM_EPS) * g


def _rope_rows(x, cos, sin):
    return x * cos + pltpu.roll(x, HALF, 1) * sin


def _rope_cols(x, cos, sin):
    return x * cos + jnp.concatenate([x[HALF:], x[:HALF]], axis=0) * sin


def _proj_kernel(x_ref, g_ref, win_ref, wint_ref, gq_ref, wqt_ref, gkv_ref, wk_ref, wvt_ref,
                 cm_ref, sm_ref, co_ref, so_ref, cmt_ref, smt_ref, cot_ref, sot_ref,
                 mqt_ref, mk_ref, mvt_ref, oqt_ref, oqft_ref, ok_ref, ovt_ref, kmean_ref,
                 *, mla_scale, moba_scale):
    for r in range(x_ref.shape[0] // PROJ_SUB_TILE):
        _proj_rows(r, slice(r * PROJ_SUB_TILE, (r + 1) * PROJ_SUB_TILE),
                   x_ref, g_ref, win_ref, wint_ref, gq_ref, wqt_ref, gkv_ref, wk_ref, wvt_ref,
                   cm_ref, sm_ref, co_ref, so_ref, cmt_ref, smt_ref, cot_ref, sot_ref,
                   mqt_ref, mk_ref, mvt_ref, oqt_ref, oqft_ref, ok_ref, ovt_ref, kmean_ref,
                   mla_scale, moba_scale)


def _proj_rows(r, rows, x_ref, g_ref, win_ref, wint_ref, gq_ref, wqt_ref, gkv_ref, wk_ref, wvt_ref,
               cm_ref, sm_ref, co_ref, so_ref, cmt_ref, smt_ref, cot_ref, sot_ref,
               mqt_ref, mk_ref, mvt_ref, oqt_ref, oqft_ref, ok_ref, ovt_ref, kmean_ref,
               mla_scale, moba_scale):
    u = _rms(x_ref[rows, :], g_ref[...]).astype(BF16)

    def proj(lo, hi):
        return jnp.dot(u, win_ref[:, lo:hi], preferred_element_type=F32)

    def proj_t(lo, hi):
        return lax.dot_general(wint_ref[lo:hi, :], u, _NT, preferred_element_type=F32)

    c0 = MLA_Q_RANK
    c1 = c0 + MLA_KV_RANK
    c2 = c1 + LANES
    c3 = c2 + MOBA_WIDTH

    latent = proj(0, c2)
    cq = _rms(latent[:, :c0], gq_ref[...]).astype(BF16)
    qt = lax.dot_general(wqt_ref[...], cq, _NT, preferred_element_type=F32)
    cmt, smt = cmt_ref[:, rows], smt_ref[:, rows]
    for h in range(MLA_HEADS):
        sl = slice(h * LANES, (h + 1) * LANES)
        mqt_ref[0, sl, rows] = (_rope_cols(qt[sl], cmt, smt) * mla_scale).astype(BF16)

    ckv = _rms(latent[:, c0:c1], gkv_ref[...]).astype(BF16)
    k_shared = _rope_rows(latent[:, c1:c2], cm_ref[rows, :], sm_ref[rows, :])
    k_nope = jnp.dot(ckv, wk_ref[...], preferred_element_type=F32)
    for h in range(MLA_HEADS):
        sl = slice(h * LANES, (h + 1) * LANES)
        mk_ref[rows, sl] = (k_nope[:, sl] + k_shared).astype(BF16)
    mvt_ref[0, :, rows] = lax.dot_general(wvt_ref[...], ckv, _NT,
                                          preferred_element_type=F32).astype(BF16)

    oqt = proj_t(0, MOBA_WIDTH)
    ok = proj(c2, c3)
    co, so = co_ref[rows, :], so_ref[rows, :]
    cot, sot = cot_ref[:, rows], sot_ref[:, rows]
    for p in range(MOBA_WIDTH // LANES):
        sl = slice(p * LANES, (p + 1) * LANES)
        q_rot = _rope_cols(oqt[sl], cot, sot) * moba_scale
        oqt_ref[0, sl, rows] = q_rot.astype(BF16)
        oqft_ref[0, sl, rows] = q_rot
        k_rot = _rope_rows(ok[:, sl], co, so)
        ok_ref[rows, sl] = k_rot.astype(BF16)
        for b in range(PROJ_SUB_TILE // MOBA_BLOCK):
            kmean_ref[r * (PROJ_SUB_TILE // MOBA_BLOCK) + b, :, sl] = jnp.mean(
                k_rot[b * MOBA_BLOCK:(b + 1) * MOBA_BLOCK], axis=0, keepdims=True)
    ovt_ref[0, :, rows] = proj_t(MOBA_WIDTH, 2 * MOBA_WIDTH).astype(BF16)


def _schedule(n_q):
    tiles = [q for q in range(n_q) for _ in range(q)]
    chunks = [c for q in range(n_q) for c in range(q)]
    return np.asarray(tiles + tiles[-1:], np.int32), np.asarray(chunks + chunks[-1:], np.int32)


def _heads_to_rows(out_t, n_heads, tq):
    return jnp.concatenate([out_t[:, h * tq:(h + 1) * tq] for h in range(n_heads)], axis=0).T


def _attend_row(tile_tab, chunk_tab, q_cat_of, k_ref, vt_ref, o_ref, s_bufs, st_ref, tri_ref,
                n_heads, bias_of=None):
    tq = tk = ATTN_TILE
    n_q = st_ref.shape[0]
    n_steps = n_q * (n_q - 1) // 2
    unroll = next(u for u in (14, 4, 2) if n_steps % u == 0)
    diag_unroll = 4 if n_q % 4 == 0 else 2
    width = n_heads * tq
    d_v = vt_ref.shape[1] // n_heads
    acc_rows = d_v + SUBLANES_BF16
    ones = jnp.ones((SUBLANES_BF16, tk), BF16)

    def scores(tile, chunk, s_ref, diagonal):
        start = pl.multiple_of(chunk * tk, tk)
        s = jnp.dot(k_ref[0, pl.ds(start, tk), :], q_cat_of(tile), preferred_element_type=F32)
        if bias_of is not None:
            per_chunk = tk // MOBA_BLOCK
            s = jnp.concatenate(
                [s[j * MOBA_BLOCK:(j + 1) * MOBA_BLOCK] + bias_of(tile, per_chunk * chunk + j)
                 for j in range(per_chunk)], axis=0)
        if diagonal:
            tri = tri_ref[...]
            s = jnp.concatenate([s[:, h * tq:(h + 1) * tq] + tri for h in range(n_heads)], axis=1)
        s_ref[...] = s
        return jnp.max(s, axis=0, keepdims=True)

    def accumulate(chunk, state, s, s_max):
        m, acc = state
        m_new = jnp.maximum(m, s_max)
        alpha = jnp.exp2(m - m_new)
        p = jnp.exp2(s - m_new).astype(BF16)
        start = pl.multiple_of(chunk * tk, tk)
        pv = [jnp.dot(jnp.concatenate([vt_ref[0, h * d_v:(h + 1) * d_v, pl.ds(start, tk)], ones], axis=0),
                      p[:, h * tq:(h + 1) * tq], preferred_element_type=F32)
              for h in range(n_heads)]
        return m_new, alpha * acc + jnp.concatenate(pv, axis=1)

    def save(tile, state):
        m, acc = state
        st_ref[tile, :acc_rows, :] = acc
        st_ref[tile, acc_rows:, :] = jnp.broadcast_to(m, (SUBLANES, width))

    def full_steps(i, carry):
        state, s_max = carry
        for j in range(unroll):
            t = i * unroll + j
            next_max = scores(tile_tab[t + 1], chunk_tab[t + 1], s_bufs[(j + 1) % 2], False)
            tile, chunk = tile_tab[t], chunk_tab[t]
            m, acc = state
            fresh = chunk == 0
            state = accumulate(chunk, (jnp.where(fresh, NEG_INF, m), jnp.where(fresh, 0.0, acc)),
                               s_bufs[j % 2][...], s_max)
            save(tile, state)
            s_max = next_max
        return state, s_max

    def diagonal_steps(i, s_max):
        for j in range(diag_unroll):
            tile = diag_unroll * i + j
            following = jnp.minimum(tile + 1, n_q - 1)
            next_max = scores(following, following, s_bufs[(j + 1) % 2], True)
            state = (st_ref[tile, acc_rows:acc_rows + 1, :], st_ref[tile, :acc_rows, :])
            _, acc = accumulate(tile, state, s_bufs[j % 2][...], s_max)
            out_t = acc[:d_v] * (1.0 / acc[d_v:d_v + 1])
            o_ref[0, pl.ds(pl.multiple_of(tile * tq, tq), tq), :] = (
                _heads_to_rows(out_t, n_heads, tq).astype(BF16))
            s_max = next_max
        return s_max

    empty = (jnp.full((1, width), NEG_INF, F32), jnp.zeros((acc_rows, width), F32))
    save(0, empty)
    lax.fori_loop(0, n_steps // unroll, full_steps,
                  (empty, scores(tile_tab[0], chunk_tab[0], s_bufs[0], False)))
    lax.fori_loop(0, n_q // diag_unroll, diagonal_steps, scores(0, 0, s_bufs[0], True))


def _init_causal_bias(tri_ref):
    @pl.when((pl.program_id(0) == 0) & (pl.program_id(1) == 0))
    def _():
        key = lax.broadcasted_iota(jnp.int32, tri_ref.shape, 0)
        qry = lax.broadcasted_iota(jnp.int32, tri_ref.shape, 1)
        tri_ref[...] = jnp.where(key <= qry, 0.0, NEG_INF)


def _mla_attn_kernel(tile_tab, chunk_tab, qt_ref, k_ref, vt_ref, o_ref, s_a, s_b, st_ref, tri_ref):
    tq = ATTN_TILE
    _init_causal_bias(tri_ref)

    def q_cat_of(tile):
        q = qt_ref[0, :, pl.ds(pl.multiple_of(tile * tq, tq), tq)]
        zero = jnp.zeros((LANES, tq), BF16)
        return jnp.concatenate([jnp.concatenate([q[:LANES], zero], axis=1),
                                jnp.concatenate([zero, q[LANES:]], axis=1)], axis=0)

    _attend_row(tile_tab, chunk_tab, q_cat_of, k_ref, vt_ref, o_ref, (s_a, s_b), st_ref, tri_ref, 2)


def _moba_gate_kernel(qft_ref, km_ref, bias_ref, *, n_blocks):
    tq = ATTN_TILE
    feat = lax.broadcasted_iota(jnp.int32, (LANES, tq), 0)
    blk = lax.broadcasted_iota(jnp.int32, (n_blocks, 2 * tq), 0)
    col = lax.broadcasted_iota(jnp.int32, (1, 2 * tq), 1)
    for tile in range(qft_ref.shape[2] // tq):
        cur = (tile * tq + (col & (tq - 1))) // MOBA_BLOCK
        qf = qft_ref[0, :, tile * tq:(tile + 1) * tq]
        qf_cat = jnp.concatenate(
            [jnp.where((feat & (MOBA_HEAD_DIM // 2)) == h * (MOBA_HEAD_DIM // 2), qf, 0.0)
             for h in range(2)], axis=1)
        gate = jnp.dot(km_ref[0], qf_cat, preferred_element_type=F32,
                       precision=lax.Precision.HIGHEST)
        gate = jnp.where(blk < cur, gate, -jnp.inf)
        ahead_count = jnp.zeros((n_blocks, 2 * tq), F32)
        for jp in range(n_blocks):
            g = gate[jp:jp + 1, :]
            ahead = jnp.where(jp < blk, jnp.where(g >= gate, 1.0, 0.0), jnp.where(g > gate, 1.0, 0.0))
            ahead_count = ahead_count + ahead
        keep = ((blk < cur) & (ahead_count < MOBA_TOPK)) | (blk == cur)
        bias_ref[0, 0, tile * n_blocks:(tile + 1) * n_blocks, :] = jnp.where(keep, 0.0, NEG_INF)


def _moba_attn_kernel(tile_tab, chunk_tab, qt_ref, k_ref, vt_ref, bias_ref, o_ref,
                      s_a, s_b, st_ref, tri_ref, *, n_blocks):
    tq = ATTN_TILE
    _init_causal_bias(tri_ref)
    feat = lax.broadcasted_iota(jnp.int32, (LANES, tq), 0)

    def q_cat_of(tile):
        q = qt_ref[0, :, pl.ds(pl.multiple_of(tile * tq, tq), tq)]
        return jnp.concatenate(
            [jnp.where((feat & (MOBA_HEAD_DIM // 2)) == h * (MOBA_HEAD_DIM // 2), q,
                       jnp.zeros((), BF16)) for h in range(2)], axis=1)

    def bias_of(tile, block):
        return bias_ref[0, 0, pl.ds(tile * n_blocks + block, 1), :]

    _attend_row(tile_tab, chunk_tab, q_cat_of, k_ref, vt_ref, o_ref, (s_a, s_b), st_ref, tri_ref, 2,
                bias_of)


def _mlp_kernel(x_ref, a_ref, b_ref, wo_ref, g1_ref, g2_ref, wu_ref, wd_ref, g3_ref, o_ref,
                *, ff_chunk, sub_tile):
    n_a = a_ref.shape[1]
    for r in range(x_ref.shape[0] // sub_tile):
        rows = slice(r * sub_tile, (r + 1) * sub_tile)
        y = (jnp.dot(a_ref[rows, :], wo_ref[0:n_a, :], preferred_element_type=F32)
             + jnp.dot(b_ref[rows, :], wo_ref[n_a:, :], preferred_element_type=F32))
        h = x_ref[rows, :] + _rms(y, g1_ref[...])
        u = _rms(h, g2_ref[...]).astype(BF16)
        acc = jnp.zeros(h.shape, F32)
        for c in range(wu_ref.shape[1] // ff_chunk):
            sl = slice(c * ff_chunk, (c + 1) * ff_chunk)
            a = jnp.maximum(jnp.dot(u, wu_ref[:, sl], preferred_element_type=F32), 0.0)
            acc = acc + jnp.dot((a * a).astype(BF16), wd_ref[sl, :], preferred_element_type=F32)
        o_ref[rows, :] = h + _rms(acc, g3_ref[...])


def _const_spec(shape):
    return pl.BlockSpec(shape, lambda *_: (0,) * len(shape))


def _layer(x, attn_pre_g, w_in, mla_q_norm_g, w_mla_q_up, mla_kv_norm_g, w_mla_kv_up,
           w_out, attn_post_g, mlp_pre_g, w_up, w_down, mlp_post_g):
    B, S, D = x.shape
    T = B * S
    n_blocks = S // MOBA_BLOCK
    n_q = S // ATTN_TILE
    assert S % ATTN_TILE == 0 and ATTN_TILE % MOBA_BLOCK == 0
    assert n_q % 2 == 0 and (n_q * (n_q - 1) // 2) % 2 == 0
    row = lambda g: g.reshape(1, -1).astype(F32)

    cols_rm, cols_fm = _w_in_columns()
    win = _gather_columns(w_in, cols_rm)
    wint = _gather_columns(w_in, cols_fm).T
    wqt = _gather_columns(w_mla_q_up, _mla_q_columns()).T
    wk = _gather_columns(w_mla_kv_up, _mla_k_columns())
    wvt = _gather_columns(w_mla_kv_up, _mla_v_columns()).T
    tables = _rope_tables(S)
    tables_t = tuple(t.T for t in tables)

    tm = PROJ_TILE
    n_st = S // tm
    tok = lambda w: pl.BlockSpec((tm, w), lambda i: (i, 0))
    tok_t = lambda w: pl.BlockSpec((1, w, tm), lambda i: (i // n_st, 0, i % n_st))
    pos = pl.BlockSpec((tm, LANES), lambda i: (i % n_st, 0))
    pos_t = pl.BlockSpec((LANES, tm), lambda i: (0, i % n_st))
    x2 = x.reshape(T, D)
    log2e = math.log2(math.e)
    mqt, mk, mvt, oqt, oqft, ok, ovt, kmean = pl.pallas_call(
        functools.partial(_proj_kernel,
                          mla_scale=float((MLA_NOPE_DIM + MLA_ROPE_DIM) ** -0.5 * log2e),
                          moba_scale=float(MOBA_HEAD_DIM ** -0.5 * log2e)),
        grid=(T // tm,),
        in_specs=[tok(D), _const_spec((1, D)), _const_spec(win.shape), _const_spec(wint.shape),
                  _const_spec((1, MLA_Q_RANK)), _const_spec(wqt.shape),
                  _const_spec((1, MLA_KV_RANK)), _const_spec(wk.shape), _const_spec(wvt.shape),
                  pos, pos, pos, pos, pos_t, pos_t, pos_t, pos_t],
        out_specs=[tok_t(MLA_PAD_WIDTH), tok(MLA_PAD_WIDTH), tok_t(MLA_WIDTH),
                   tok_t(MOBA_WIDTH), tok_t(MOBA_WIDTH), tok(MOBA_WIDTH), tok_t(MOBA_WIDTH),
                   pl.BlockSpec((tm // MOBA_BLOCK, 1, MOBA_WIDTH), lambda i: (i, 0, 0))],
        out_shape=[jax.ShapeDtypeStruct((B, MLA_PAD_WIDTH, S), BF16),
                   jax.ShapeDtypeStruct((T, MLA_PAD_WIDTH), BF16),
                   jax.ShapeDtypeStruct((B, MLA_WIDTH, S), BF16),
                   jax.ShapeDtypeStruct((B, MOBA_WIDTH, S), BF16),
                   jax.ShapeDtypeStruct((B, MOBA_WIDTH, S), F32),
                   jax.ShapeDtypeStruct((T, MOBA_WIDTH), BF16),
                   jax.ShapeDtypeStruct((B, MOBA_WIDTH, S), BF16),
                   jax.ShapeDtypeStruct((T // MOBA_BLOCK, 1, MOBA_WIDTH), F32)],
        compiler_params=pltpu.CompilerParams(vmem_limit_bytes=VMEM_LIMIT),
        name="token_projection",
    )(x2, row(attn_pre_g), win, wint, row(mla_q_norm_g), wqt, row(mla_kv_norm_g), wk, wvt,
      *tables, *tables_t)

    tq = ATTN_TILE
    width = 2 * tq
    tile_tab, chunk_tab = (jnp.asarray(t) for t in _schedule(n_q))
    seq_t = lambda w: pl.BlockSpec((1, w, S), lambda b, p, *_: (b, p, 0))
    seq = lambda w: pl.BlockSpec((1, S, w), lambda b, p, *_: (b, 0, p))
    attn_params = pltpu.CompilerParams(vmem_limit_bytes=VMEM_LIMIT)
    attn_scratch = [pltpu.VMEM((tq, width), F32), pltpu.VMEM((tq, width), F32),
                    pltpu.VMEM((n_q, MLA_V_DIM + SUBLANES_BF16 + SUBLANES, width), F32),
                    pltpu.VMEM((tq, tq), F32)]

    mla_o = pl.pallas_call(
        _mla_attn_kernel,
        grid_spec=pltpu.PrefetchScalarGridSpec(
            num_scalar_prefetch=2, grid=(B, MLA_HEADS // 2),
            in_specs=[seq_t(2 * LANES), seq(2 * LANES), seq_t(LANES)],
            out_specs=seq(LANES), scratch_shapes=attn_scratch),
        out_shape=jax.ShapeDtypeStruct((B, S, MLA_WIDTH), BF16),
        compiler_params=attn_params,
        name="mla_attention",
    )(tile_tab, chunk_tab, mqt, mk.reshape(B, S, MLA_PAD_WIDTH), mvt)

    moba_bias = pl.pallas_call(
        functools.partial(_moba_gate_kernel, n_blocks=n_blocks),
        grid=(B, MOBA_HEADS // 2),
        in_specs=[pl.BlockSpec((1, LANES, S), lambda b, p: (b, p, 0)),
                  pl.BlockSpec((1, n_blocks, LANES), lambda b, p: (b, 0, p))],
        out_specs=pl.BlockSpec((1, 1, n_q * n_blocks, width), lambda b, p: (b, p, 0, 0)),
        out_shape=jax.ShapeDtypeStruct((B, MOBA_HEADS // 2, n_q * n_blocks, width), F32),
        name="moba_gate",
    )(oqft, kmean.reshape(B, n_blocks, MOBA_WIDTH))

    moba_o = pl.pallas_call(
        functools.partial(_moba_attn_kernel, n_blocks=n_blocks),
        grid_spec=pltpu.PrefetchScalarGridSpec(
            num_scalar_prefetch=2, grid=(B, MOBA_HEADS // 2),
            in_specs=[seq_t(LANES), seq(LANES), seq_t(LANES),
                      pl.BlockSpec((1, 1, n_q * n_blocks, width), lambda b, p, *_: (b, p, 0, 0))],
            out_specs=seq(LANES), scratch_shapes=attn_scratch),
        out_shape=jax.ShapeDtypeStruct((B, S, MOBA_WIDTH), BF16),
        compiler_params=attn_params,
        name="moba_attention",
    )(tile_tab, chunk_tab, oqt, ok.reshape(B, S, MOBA_WIDTH), ovt, moba_bias)

    tm = MLP_TILE
    d_ff = w_up.shape[1]
    tok = lambda w: pl.BlockSpec((tm, w), lambda i: (i, 0))
    out = pl.pallas_call(
        functools.partial(_mlp_kernel, ff_chunk=FF_CHUNK, sub_tile=MLP_SUB_TILE),
        grid=(T // tm,),
        in_specs=[tok(D), tok(MLA_WIDTH), tok(MOBA_WIDTH),
                  _const_spec((MLA_WIDTH + MOBA_WIDTH, D)), _const_spec((1, D)), _const_spec((1, D)),
                  _const_spec((D, d_ff)), _const_spec((d_ff, D)), _const_spec((1, D))],
        out_specs=tok(D),
        out_shape=jax.ShapeDtypeStruct((T, D), F32),
        compiler_params=pltpu.CompilerParams(vmem_limit_bytes=VMEM_LIMIT),
        name="out_proj_mlp",
    )(x2, mla_o.reshape(T, MLA_WIDTH), moba_o.reshape(T, MOBA_WIDTH),
      w_out.astype(BF16), row(attn_post_g), row(mlp_pre_g),
      w_up.astype(BF16), w_down.astype(BF16), row(mlp_post_g))
    return out.reshape(B, S, D)


def kernel(x, attn_pre_g, w_in, mla_q_norm_g, w_mla_q_up, mla_kv_norm_g, w_mla_kv_up, w_out,
           attn_post_g, mlp_pre_g, w_up, w_down, mlp_post_g):
    h = x
    for l in range(w_in.shape[0]):
        h = _layer(h, attn_pre_g[l], w_in[l], mla_q_norm_g[l], w_mla_q_up[l], mla_kv_norm_g[l],
                   w_mla_kv_up[l], w_out[l], attn_post_g[l], mlp_pre_g[l], w_up[l], w_down[l],
                   mlp_post_g[l])
    return h
```

```python
import functools
import math

import numpy as np
import jax
import jax.numpy as jnp
from jax import lax
from jax.experimental import pallas as pl
from jax.experimental.pallas import tpu as pltpu

MLA_HEADS = 8
MLA_NOPE_DIM = 64
MLA_ROPE_DIM = 32
MLA_V_DIM = 64
MLA_Q_RANK = 256
MLA_KV_RANK = 128
MOBA_HEADS = 8
MOBA_HEAD_DIM = 64
MOBA_BLOCK = 256
MOBA_TOPK = 3
ROPE_THETA = 10000.0
NORM_EPS = 1e-6
NEG_INF = -1e30

LANES = 128
SUBLANES = 8
SUBLANES_BF16 = 16
HALF = LANES // 2
MOBA_WIDTH = MOBA_HEADS * MOBA_HEAD_DIM
MLA_WIDTH = MLA_HEADS * MLA_V_DIM
MLA_PAD_WIDTH = MLA_HEADS * LANES
PROJ_TILE = 512
PROJ_SUB_TILE = 512
ATTN_TILE = 512
MLP_TILE = 1024
MLP_SUB_TILE = 256
FF_CHUNK = 1024
VMEM_LIMIT = 56 * 1024 * 1024

F32 = jnp.float32
BF16 = jnp.bfloat16
_NT = (((1,), (1,)), ((), ()))


def _moba_pair_columns(base):
    cols = []
    for p in range(MOBA_HEADS // 2):
        for off in (0, MOBA_HEAD_DIM // 2):
            for h in (2 * p, 2 * p + 1):
                cols += [base + h * MOBA_HEAD_DIM + off + i for i in range(MOBA_HEAD_DIM // 2)]
    return cols


def _w_in_columns():
    s2 = MLA_Q_RANK + MLA_KV_RANK
    s3 = s2 + MLA_ROPE_DIM
    half = MLA_ROPE_DIM // 2
    shared = [-1] * LANES
    for i in range(half):
        shared[32 + i] = s2 + i
        shared[96 + i] = s2 + half + i
    row_major = list(range(0, s2)) + shared + _moba_pair_columns(s3 + MOBA_WIDTH)
    feature_major = _moba_pair_columns(s3) + list(range(s3 + 2 * MOBA_WIDTH, s3 + 3 * MOBA_WIDTH))
    return np.asarray(row_major, np.int32), np.asarray(feature_major, np.int32)


def _mla_q_columns():
    cols = []
    half = MLA_ROPE_DIM // 2
    for h in range(MLA_HEADS):
        b = h * (MLA_NOPE_DIM + MLA_ROPE_DIM)
        cols += [b + i for i in range(32)] + [b + MLA_NOPE_DIM + i for i in range(half)] + [-1] * 16
        cols += [b + 32 + i for i in range(32)] + [b + MLA_NOPE_DIM + half + i for i in range(half)] + [-1] * 16
    return np.asarray(cols, np.int32)


def _mla_k_columns():
    cols = []
    for h in range(MLA_HEADS):
        b = h * (MLA_NOPE_DIM + MLA_V_DIM)
        cols += [b + i for i in range(32)] + [-1] * 32 + [b + 32 + i for i in range(32)] + [-1] * 32
    return np.asarray(cols, np.int32)


def _mla_v_columns():
    cols = []
    for h in range(MLA_HEADS):
        b = h * (MLA_NOPE_DIM + MLA_V_DIM) + MLA_NOPE_DIM
        cols += [b + i for i in range(MLA_V_DIM)]
    return np.asarray(cols, np.int32)


def _gather_columns(w, cols):
    picked = jnp.take(w, jnp.asarray(np.maximum(cols, 0)), axis=1)
    return jnp.where(jnp.asarray(cols >= 0)[None, :], picked, 0.0).astype(BF16)


def _rope_tables(seq):
    pos = jnp.arange(seq, dtype=F32)

    def cs(half):
        inv_freq = 1.0 / (ROPE_THETA ** (jnp.arange(half, dtype=F32) / half))
        ang = pos[:, None] * inv_freq[None, :]
        return jnp.cos(ang), jnp.sin(ang)

    c, s = cs(MLA_ROPE_DIM // 2)
    one = lambda n: jnp.ones((seq, n), F32)
    zero = lambda n: jnp.zeros((seq, n), F32)
    cos_mla = jnp.concatenate([one(32), c, one(16), one(32), c, one(16)], axis=1)
    sin_mla = jnp.concatenate([zero(32), -s, zero(16), zero(32), s, zero(16)], axis=1)
    c, s = cs(MOBA_HEAD_DIM // 2)
    cos_moba = jnp.concatenate([c, c, c, c], axis=1)
    sin_moba = jnp.concatenate([-s, -s, s, s], axis=1)
    return cos_mla, sin_mla, cos_moba, sin_moba


def _rms(x, g):
    ms = jnp.mean(x * x, axis=-1, keepdims=True)
    return x * lax.rsqrt(ms + NORM_EPS) * g


def _rope_rows(x, cos, sin):
    return x * cos + pltpu.roll(x, HALF, 1) * sin


def _rope_cols(x, cos, sin):
    return x * cos + jnp.concatenate([x[HALF:], x[:HALF]], axis=0) * sin


def _proj_kernel(x_ref, g_ref, win_ref, wint_ref, gq_ref, wqt_ref, gkv_ref, wk_ref, wvt_ref,
                 cm_ref, sm_ref, co_ref, so_ref, cmt_ref, smt_ref, cot_ref, sot_ref,
                 mqt_ref, mk_ref, mvt_ref, oqt_ref, oqft_ref, ok_ref, ovt_ref, kmean_ref,
                 *, mla_scale, moba_scale):
    for r in range(x_ref.shape[0] // PROJ_SUB_TILE):
        _proj_rows(r, slice(r * PROJ_SUB_TILE, (r + 1) * PROJ_SUB_TILE),
                   x_ref, g_ref, win_ref, wint_ref, gq_ref, wqt_ref, gkv_ref, wk_ref, wvt_ref,
                   cm_ref, sm_ref, co_ref, so_ref, cmt_ref, smt_ref, cot_ref, sot_ref,
                   mqt_ref, mk_ref, mvt_ref, oqt_ref, oqft_ref, ok_ref, ovt_ref, kmean_ref,
                   mla_scale, moba_scale)


def _proj_rows(r, rows, x_ref, g_ref, win_ref, wint_ref, gq_ref, wqt_ref, gkv_ref, wk_ref, wvt_ref,
               cm_ref, sm_ref, co_ref, so_ref, cmt_ref, smt_ref, cot_ref, sot_ref,
               mqt_ref, mk_ref, mvt_ref, oqt_ref, oqft_ref, ok_ref, ovt_ref, kmean_ref,
               mla_scale, moba_scale):
    u = _rms(x_ref[rows, :], g_ref[...]).astype(BF16)

    def proj(lo, hi):
        return jnp.dot(u, win_ref[:, lo:hi], preferred_element_type=F32)

    def proj_t(lo, hi):
        return lax.dot_general(wint_ref[lo:hi, :], u, _NT, preferred_element_type=F32)

    c0 = MLA_Q_RANK
    c1 = c0 + MLA_KV_RANK
    c2 = c1 + LANES
    c3 = c2 + MOBA_WIDTH

    latent = proj(0, c2)
    cq = _rms(latent[:, :c0], gq_ref[...]).astype(BF16)
    qt = lax.dot_general(wqt_ref[...], cq, _NT, preferred_element_type=F32)
    cmt, smt = cmt_ref[:, rows], smt_ref[:, rows]
    for h in range(MLA_HEADS):
        sl = slice(h * LANES, (h + 1) * LANES)
        mqt_ref[0, sl, rows] = (_rope_cols(qt[sl], cmt, smt) * mla_scale).astype(BF16)

    ckv = _rms(latent[:, c0:c1], gkv_ref[...]).astype(BF16)
    k_shared = _rope_rows(latent[:, c1:c2], cm_ref[rows, :], sm_ref[rows, :])
    k_nope = jnp.dot(ckv, wk_ref[...], preferred_element_type=F32)
    for h in range(MLA_HEADS):
        sl = slice(h * LANES, (h + 1) * LANES)
        mk_ref[rows, sl] = (k_nope[:, sl] + k_shared).astype(BF16)
    mvt_ref[0, :, rows] = lax.dot_general(wvt_ref[...], ckv, _NT,
                                          preferred_element_type=F32).astype(BF16)

    oqt = proj_t(0, MOBA_WIDTH)
    ok = proj(c2, c3)
    co, so = co_ref[rows, :], so_ref[rows, :]
    cot, sot = cot_ref[:, rows], sot_ref[:, rows]
    for p in range(MOBA_WIDTH // LANES):
        sl = slice(p * LANES, (p + 1) * LANES)
        q_rot = _rope_cols(oqt[sl], cot, sot) * moba_scale
        oqt_ref[0, sl, rows] = q_rot.astype(BF16)
        oqft_ref[0, sl, rows] = q_rot
        k_rot = _rope_rows(ok[:, sl], co, so)
        ok_ref[rows, sl] = k_rot.astype(BF16)
        for b in range(PROJ_SUB_TILE // MOBA_BLOCK):
            kmean_ref[r * (PROJ_SUB_TILE // MOBA_BLOCK) + b, :, sl] = jnp.mean(
                k_rot[b * MOBA_BLOCK:(b + 1) * MOBA_BLOCK], axis=0, keepdims=True)
    ovt_ref[0, :, rows] = proj_t(MOBA_WIDTH, 2 * MOBA_WIDTH).astype(BF16)


def _schedule(n_q):
    tiles = [q for q in range(n_q) for _ in range(q)]
    chunks = [c for q in range(n_q) for c in range(q)]
    return np.asarray(tiles + tiles[-1:], np.int32), np.asarray(chunks + chunks[-1:], np.int32)


def _heads_to_rows(out_t, n_heads, tq):
    return jnp.concatenate([out_t[:, h * tq:(h + 1) * tq] for h in range(n_heads)], axis=0).T


def _attend_row(tile_tab, chunk_tab, q_cat_of, k_ref, vt_ref, o_ref, s_bufs, st_ref, tri_ref,
                n_heads, bias_of=None):
    tq = tk = ATTN_TILE
    n_q = st_ref.shape[0]
    n_steps = n_q * (n_q - 1) // 2
    unroll = next(u for u in (14, 4, 2) if n_steps % u == 0)
    diag_unroll = 4 if n_q % 4 == 0 else 2
    width = n_heads * tq
    d_v = vt_ref.shape[1] // n_heads
    acc_rows = d_v + SUBLANES_BF16
    ones = jnp.ones((SUBLANES_BF16, tk), BF16)

    def scores(tile, chunk, s_ref, diagonal):
        start = pl.multiple_of(chunk * tk, tk)
        s = jnp.dot(k_ref[0, pl.ds(start, tk), :], q_cat_of(tile), preferred_element_type=F32)
        if bias_of is not None:
            per_chunk = tk // MOBA_BLOCK
            s = jnp.concatenate(
                [s[j * MOBA_BLOCK:(j + 1) * MOBA_BLOCK] + bias_of(tile, per_chunk * chunk + j)
                 for j in range(per_chunk)], axis=0)
        if diagonal:
            tri = tri_ref[...]
            s = jnp.concatenate([s[:, h * tq:(h + 1) * tq] + tri for h in range(n_heads)], axis=1)
        s_ref[...] = s
        return jnp.max(s, axis=0, keepdims=True)

    def accumulate(chunk, state, s, s_max):
        m, acc = state
        m_new = jnp.maximum(m, s_max)
        alpha = jnp.exp2(m - m_new)
        p = jnp.exp2(s - m_new).astype(BF16)
        start = pl.multiple_of(chunk * tk, tk)
        pv = [jnp.dot(jnp.concatenate([vt_ref[0, h * d_v:(h + 1) * d_v, pl.ds(start, tk)], ones], axis=0),
                      p[:, h * tq:(h + 1) * tq], preferred_element_type=F32)
              for h in range(n_heads)]
        return m_new, alpha * acc + jnp.concatenate(pv, axis=1)

    def save(tile, state):
        m, acc = state
        st_ref[tile, :acc_rows, :] = acc
        st_ref[tile, acc_rows:, :] = jnp.broadcast_to(m, (SUBLANES, width))

    def full_steps(i, carry):
        state, s_max = carry
        for j in range(unroll):
            t = i * unroll + j
            next_max = scores(tile_tab[t + 1], chunk_tab[t + 1], s_bufs[(j + 1) % 2], False)
            tile, chunk = tile_tab[t], chunk_tab[t]
            m, acc = state
            fresh = chunk == 0
            state = accumulate(chunk, (jnp.where(fresh, NEG_INF, m), jnp.where(fresh, 0.0, acc)),
                               s_bufs[j % 2][...], s_max)
            save(tile, state)
            s_max = next_max
        return state, s_max

    def diagonal_steps(i, s_max):
        for j in range(diag_unroll):
            tile = diag_unroll * i + j
            following = jnp.minimum(tile + 1, n_q - 1)
            next_max = scores(following, following, s_bufs[(j + 1) % 2], True)
            state = (st_ref[tile, acc_rows:acc_rows + 1, :], st_ref[tile, :acc_rows, :])
            _, acc = accumulate(tile, state, s_bufs[j % 2][...], s_max)
            out_t = acc[:d_v] * (1.0 / acc[d_v:d_v + 1])
            o_ref[0, pl.ds(pl.multiple_of(tile * tq, tq), tq), :] = (
                _heads_to_rows(out_t, n_heads, tq).astype(BF16))
            s_max = next_max
        return s_max

    empty = (jnp.full((1, width), NEG_INF, F32), jnp.zeros((acc_rows, width), F32))
    save(0, empty)
    lax.fori_loop(0, n_steps // unroll, full_steps,
                  (empty, scores(tile_tab[0], chunk_tab[0], s_bufs[0], False)))
    lax.fori_loop(0, n_q // diag_unroll, diagonal_steps, scores(0, 0, s_bufs[0], True))


def _init_causal_bias(tri_ref):
    @pl.when((pl.program_id(0) == 0) & (pl.program_id(1) == 0))
    def _():
        key = lax.broadcasted_iota(jnp.int32, tri_ref.shape, 0)
        qry = lax.broadcasted_iota(jnp.int32, tri_ref.shape, 1)
        tri_ref[...] = jnp.where(key <= qry, 0.0, NEG_INF)


def _mla_attn_kernel(tile_tab, chunk_tab, qt_ref, k_ref, vt_ref, o_ref, s_a, s_b, st_ref, tri_ref):
    tq = ATTN_TILE
    _init_causal_bias(tri_ref)

    def q_cat_of(tile):
        q = qt_ref[0, :, pl.ds(pl.multiple_of(tile * tq, tq), tq)]
        zero = jnp.zeros((LANES, tq), BF16)
        return jnp.concatenate([jnp.concatenate([q[:LANES], zero], axis=1),
                                jnp.concatenate([zero, q[LANES:]], axis=1)], axis=0)

    _attend_row(tile_tab, chunk_tab, q_cat_of, k_ref, vt_ref, o_ref, (s_a, s_b), st_ref, tri_ref, 2)


def _moba_gate_kernel(qft_ref, km_ref, bias_ref, *, n_blocks):
    tq = ATTN_TILE
    feat = lax.broadcasted_iota(jnp.int32, (LANES, tq), 0)
    blk = lax.broadcasted_iota(jnp.int32, (n_blocks, 2 * tq), 0)
    col = lax.broadcasted_iota(jnp.int32, (1, 2 * tq), 1)
    for tile in range(qft_ref.shape[2] // tq):
        cur = (tile * tq + (col & (tq - 1))) // MOBA_BLOCK
        qf = qft_ref[0, :, tile * tq:(tile + 1) * tq]
        qf_cat = jnp.concatenate(
            [jnp.where((feat & (MOBA_HEAD_DIM // 2)) == h * (MOBA_HEAD_DIM // 2), qf, 0.0)
             for h in range(2)], axis=1)
        gate = jnp.dot(km_ref[0], qf_cat, preferred_element_type=F32,
                       precision=lax.Precision.HIGHEST)
        gate = jnp.where(blk < cur, gate, -jnp.inf)
        ahead_count = jnp.zeros((n_blocks, 2 * tq), F32)
        for jp in range(n_blocks):
            g = gate[jp:jp + 1, :]
            ahead = jnp.where(jp < blk, jnp.where(g >= gate, 1.0, 0.0), jnp.where(g > gate, 1.0, 0.0))
            ahead_count = ahead_count + ahead
        keep = ((blk < cur) & (ahead_count < MOBA_TOPK)) | (blk == cur)
        bias_ref[0, 0, tile * n_blocks:(tile + 1) * n_blocks, :] = jnp.where(keep, 0.0, NEG_INF)


def _moba_attn_kernel(tile_tab, chunk_tab, qt_ref, k_ref, vt_ref, bias_ref, o_ref,
                      s_a, s_b, st_ref, tri_ref, *, n_blocks):
    tq = ATTN_TILE
    _init_causal_bias(tri_ref)
    feat = lax.broadcasted_iota(jnp.int32, (LANES, tq), 0)

    def q_cat_of(tile):
        q = qt_ref[0, :, pl.ds(pl.multiple_of(tile * tq, tq), tq)]
        return jnp.concatenate(
            [jnp.where((feat & (MOBA_HEAD_DIM // 2)) == h * (MOBA_HEAD_DIM // 2), q,
                       jnp.zeros((), BF16)) for h in range(2)], axis=1)

    def bias_of(tile, block):
        return bias_ref[0, 0, pl.ds(tile * n_blocks + block, 1), :]

    _attend_row(tile_tab, chunk_tab, q_cat_of, k_ref, vt_ref, o_ref, (s_a, s_b), st_ref, tri_ref, 2,
                bias_of)


def _mlp_kernel(x_ref, a_ref, b_ref, wo_ref, g1_ref, g2_ref, wu_ref, wd_ref, g3_ref, o_ref,
                *, ff_chunk, sub_tile):
    n_a = a_ref.shape[1]
    subs = [slice(r * sub_tile, (r + 1) * sub_tile) for r in range(x_ref.shape[0] // sub_tile)]

    def attn_out(rows):
        return (jnp.dot(a_ref[rows, :], wo_ref[0:n_a, :], preferred_element_type=F32)
                + jnp.dot(b_ref[rows, :], wo_ref[n_a:, :], preferred_element_type=F32))

    def before(rows, y):
        h = x_ref[rows, :] + _rms(y, g1_ref[...])
        return h, _rms(h, g2_ref[...]).astype(BF16)

    def after(rows, h, acc):
        o_ref[rows, :] = h + _rms(acc, g3_ref[...])

    ys = [attn_out(rows) for rows in subs]
    ready = before(subs[0], ys[0])
    done = None
    for r, rows in enumerate(subs):
        h, u = ready
        acc = jnp.zeros(h.shape, F32)
        for c in range(wu_ref.shape[1] // ff_chunk):
            sl = slice(c * ff_chunk, (c + 1) * ff_chunk)
            a = jnp.maximum(jnp.dot(u, wu_ref[:, sl], preferred_element_type=F32), 0.0)
            acc = acc + jnp.dot((a * a).astype(BF16), wd_ref[sl, :], preferred_element_type=F32)
            if c == 0:
                if r + 1 < len(subs):
                    ready = before(subs[r + 1], ys[r + 1])
                if done is not None:
                    after(*done)
        done = (rows, h, acc)
    after(*done)


def _const_spec(shape):
    return pl.BlockSpec(shape, lambda *_: (0,) * len(shape), pipeline_mode=pl.Buffered(1))


def _layer(x, attn_pre_g, w_in, mla_q_norm_g, w_mla_q_up, mla_kv_norm_g, w_mla_kv_up,
           w_out, attn_post_g, mlp_pre_g, w_up, w_down, mlp_post_g):
    B, S, D = x.shape
    T = B * S
    n_blocks = S // MOBA_BLOCK
    n_q = S // ATTN_TILE
    assert S % ATTN_TILE == 0 and ATTN_TILE % MOBA_BLOCK == 0
    assert n_q % 2 == 0 and (n_q * (n_q - 1) // 2) % 2 == 0
    row = lambda g: g.reshape(1, -1).astype(F32)

    cols_rm, cols_fm = _w_in_columns()
    win = _gather_columns(w_in, cols_rm)
    wint = _gather_columns(w_in, cols_fm).T
    wqt = _gather_columns(w_mla_q_up, _mla_q_columns()).T
    wk = _gather_columns(w_mla_kv_up, _mla_k_columns())
    wvt = _gather_columns(w_mla_kv_up, _mla_v_columns()).T
    tables = _rope_tables(S)
    tables_t = tuple(t.T for t in tables)

    tm = PROJ_TILE
    n_st = S // tm
    tok = lambda w: pl.BlockSpec((tm, w), lambda i: (i, 0))
    tok_t = lambda w: pl.BlockSpec((1, w, tm), lambda i: (i // n_st, 0, i % n_st))
    pos = pl.BlockSpec((tm, LANES), lambda i: (i % n_st, 0))
    pos_t = pl.BlockSpec((LANES, tm), lambda i: (0, i % n_st))
    x2 = x.reshape(T, D)
    log2e = math.log2(math.e)
    mqt, mk, mvt, oqt, oqft, ok, ovt, kmean = pl.pallas_call(
        functools.partial(_proj_kernel,
                          mla_scale=float((MLA_NOPE_DIM + MLA_ROPE_DIM) ** -0.5 * log2e),
                          moba_scale=float(MOBA_HEAD_DIM ** -0.5 * log2e)),
        grid=(T // tm,),
        in_specs=[tok(D), _const_spec((1, D)), _const_spec(win.shape), _const_spec(wint.shape),
                  _const_spec((1, MLA_Q_RANK)), _const_spec(wqt.shape),
                  _const_spec((1, MLA_KV_RANK)), _const_spec(wk.shape), _const_spec(wvt.shape),
                  pos, pos, pos, pos, pos_t, pos_t, pos_t, pos_t],
        out_specs=[tok_t(MLA_PAD_WIDTH), tok(MLA_PAD_WIDTH), tok_t(MLA_WIDTH),
                   tok_t(MOBA_WIDTH), tok_t(MOBA_WIDTH), tok(MOBA_WIDTH), tok_t(MOBA_WIDTH),
                   pl.BlockSpec((tm // MOBA_BLOCK, 1, MOBA_WIDTH), lambda i: (i, 0, 0))],
        out_shape=[jax.ShapeDtypeStruct((B, MLA_PAD_WIDTH, S), BF16),
                   jax.ShapeDtypeStruct((T, MLA_PAD_WIDTH), BF16),
                   jax.ShapeDtypeStruct((B, MLA_WIDTH, S), BF16),
                   jax.ShapeDtypeStruct((B, MOBA_WIDTH, S), BF16),
                   jax.ShapeDtypeStruct((B, MOBA_WIDTH, S), F32),
                   jax.ShapeDtypeStruct((T, MOBA_WIDTH), BF16),
                   jax.ShapeDtypeStruct((B, MOBA_WIDTH, S), BF16),
                   jax.ShapeDtypeStruct((T // MOBA_BLOCK, 1, MOBA_WIDTH), F32)],
        compiler_params=pltpu.CompilerParams(vmem_limit_bytes=VMEM_LIMIT),
        name="token_projection",
    )(x2, row(attn_pre_g), win, wint, row(mla_q_norm_g), wqt, row(mla_kv_norm_g), wk, wvt,
      *tables, *tables_t)

    tq = ATTN_TILE
    width = 2 * tq
    tile_tab, chunk_tab = (jnp.asarray(t) for t in _schedule(n_q))
    seq_t = lambda w: pl.BlockSpec((1, w, S), lambda b, p, *_: (b, p, 0))
    seq = lambda w: pl.BlockSpec((1, S, w), lambda b, p, *_: (b, 0, p))
    attn_params = pltpu.CompilerParams(vmem_limit_bytes=VMEM_LIMIT)
    attn_scratch = [pltpu.VMEM((tq, width), F32), pltpu.VMEM((tq, width), F32),
                    pltpu.VMEM((n_q, MLA_V_DIM + SUBLANES_BF16 + SUBLANES, width), F32),
                    pltpu.VMEM((tq, tq), F32)]

    mla_o = pl.pallas_call(
        _mla_attn_kernel,
        grid_spec=pltpu.PrefetchScalarGridSpec(
            num_scalar_prefetch=2, grid=(B, MLA_HEADS // 2),
            in_specs=[seq_t(2 * LANES), seq(2 * LANES), seq_t(LANES)],
            out_specs=seq(LANES), scratch_shapes=attn_scratch),
        out_shape=jax.ShapeDtypeStruct((B, S, MLA_WIDTH), BF16),
        compiler_params=attn_params,
        name="mla_attention",
    )(tile_tab, chunk_tab, mqt, mk.reshape(B, S, MLA_PAD_WIDTH), mvt)

    moba_bias = pl.pallas_call(
        functools.partial(_moba_gate_kernel, n_blocks=n_blocks),
        grid=(B, MOBA_HEADS // 2),
        in_specs=[pl.BlockSpec((1, LANES, S), lambda b, p: (b, p, 0)),
                  pl.BlockSpec((1, n_blocks, LANES), lambda b, p: (b, 0, p))],
        out_specs=pl.BlockSpec((1, 1, n_q * n_blocks, width), lambda b, p: (b, p, 0, 0)),
        out_shape=jax.ShapeDtypeStruct((B, MOBA_HEADS // 2, n_q * n_blocks, width), F32),
        name="moba_gate",
    )(oqft, kmean.reshape(B, n_blocks, MOBA_WIDTH))

    moba_o = pl.pallas_call(
        functools.partial(_moba_attn_kernel, n_blocks=n_blocks),
        grid_spec=pltpu.PrefetchScalarGridSpec(
            num_scalar_prefetch=2, grid=(B, MOBA_HEADS // 2),
            in_specs=[seq_t(LANES), seq(LANES), seq_t(LANES),
                      pl.BlockSpec((1, 1, n_q * n_blocks, width), lambda b, p, *_: (b, p, 0, 0))],
            out_specs=seq(LANES), scratch_shapes=attn_scratch),
        out_shape=jax.ShapeDtypeStruct((B, S, MOBA_WIDTH), BF16),
        compiler_params=attn_params,
        name="moba_attention",
    )(tile_tab, chunk_tab, oqt, ok.reshape(B, S, MOBA_WIDTH), ovt, moba_bias)

    tm = MLP_TILE
    d_ff = w_up.shape[1]
    tok = lambda w: pl.BlockSpec((tm, w), lambda i: (i, 0))
    out = pl.pallas_call(
        functools.partial(_mlp_kernel, ff_chunk=FF_CHUNK, sub_tile=MLP_SUB_TILE),
        grid=(T // tm,),
        in_specs=[tok(D), tok(MLA_WIDTH), tok(MOBA_WIDTH),
                  _const_spec((MLA_WIDTH + MOBA_WIDTH, D)), _const_spec((1, D)), _const_spec((1, D)),
                  _const_spec((D, d_ff)), _const_spec((d_ff, D)), _const_spec((1, D))],
        out_specs=tok(D),
        out_shape=jax.ShapeDtypeStruct((T, D), F32),
        compiler_params=pltpu.CompilerParams(vmem_limit_bytes=VMEM_LIMIT),
        name="out_proj_mlp",
    )(x2, mla_o.reshape(T, MLA_WIDTH), moba_o.reshape(T, MOBA_WIDTH),
      w_out.astype(BF16), row(attn_post_g), row(mlp_pre_g),
      w_up.astype(BF16), w_down.astype(BF16), row(mlp_post_g))
    return out.reshape(B, S, D)


def kernel(x, attn_pre_g, w_in, mla_q_norm_g, w_mla_q_up, mla_kv_norm_g, w_mla_kv_up, w_out,
           attn_post_g, mlp_pre_g, w_up, w_down, mlp_post_g):
    h = x
    for l in range(w_in.shape[0]):
        h = _layer(h, attn_pre_g[l], w_in[l], mla_q_norm_g[l], w_mla_q_up[l], mla_kv_norm_g[l],
                   w_mla_kv_up[l], w_out[l], attn_post_g[l], mlp_pre_g[l], w_up[l], w_down[l],
                   mlp_post_g[l])
    return h
```

```python
import functools
import math

import numpy as np
import jax
import jax.numpy as jnp
from jax import lax
from jax.experimental import pallas as pl
from jax.experimental.pallas import tpu as pltpu

MLA_HEADS = 8
MLA_NOPE_DIM = 64
MLA_ROPE_DIM = 32
MLA_V_DIM = 64
MLA_Q_RANK = 256
MLA_KV_RANK = 128
MOBA_HEADS = 8
MOBA_HEAD_DIM = 64
MOBA_BLOCK = 256
MOBA_TOPK = 3
ROPE_THETA = 10000.0
NORM_EPS = 1e-6
NEG_INF = -1e30

LANES = 128
SUBLANES = 8
SUBLANES_BF16 = 16
HALF = LANES // 2
MOBA_WIDTH = MOBA_HEADS * MOBA_HEAD_DIM
MLA_WIDTH = MLA_HEADS * MLA_V_DIM
MLA_PAD_WIDTH = MLA_HEADS * LANES
PROJ_TILE = 1024
PROJ_SUB_TILE = 512
ATTN_TILE = 512
MLP_TILE = 1024
MLP_SUB_TILE = 256
FF_CHUNK = 1024
VMEM_LIMIT = 56 * 1024 * 1024

F32 = jnp.float32
BF16 = jnp.bfloat16
_NT = (((1,), (1,)), ((), ()))


def _moba_pair_columns(base):
    cols = []
    for p in range(MOBA_HEADS // 2):
        for off in (0, MOBA_HEAD_DIM // 2):
            for h in (2 * p, 2 * p + 1):
                cols += [base + h * MOBA_HEAD_DIM + off + i for i in range(MOBA_HEAD_DIM // 2)]
    return cols


def _w_in_columns():
    s2 = MLA_Q_RANK + MLA_KV_RANK
    s3 = s2 + MLA_ROPE_DIM
    half = MLA_ROPE_DIM // 2
    shared = [-1] * LANES
    for i in range(half):
        shared[32 + i] = s2 + i
        shared[96 + i] = s2 + half + i
    row_major = list(range(0, s2)) + shared + _moba_pair_columns(s3 + MOBA_WIDTH)
    feature_major = _moba_pair_columns(s3) + list(range(s3 + 2 * MOBA_WIDTH, s3 + 3 * MOBA_WIDTH))
    return np.asarray(row_major, np.int32), np.asarray(feature_major, np.int32)


def _mla_q_columns():
    cols = []
    half = MLA_ROPE_DIM // 2
    for h in range(MLA_HEADS):
        b = h * (MLA_NOPE_DIM + MLA_ROPE_DIM)
        cols += [b + i for i in range(32)] + [b + MLA_NOPE_DIM + i for i in range(half)] + [-1] * 16
        cols += [b + 32 + i for i in range(32)] + [b + MLA_NOPE_DIM + half + i for i in range(half)] + [-1] * 16
    return np.asarray(cols, np.int32)


def _mla_k_columns():
    cols = []
    for h in range(MLA_HEADS):
        b = h * (MLA_NOPE_DIM + MLA_V_DIM)
        cols += [b + i for i in range(32)] + [-1] * 32 + [b + 32 + i for i in range(32)] + [-1] * 32
    return np.asarray(cols, np.int32)


def _mla_v_columns():
    cols = []
    for h in range(MLA_HEADS):
        b = h * (MLA_NOPE_DIM + MLA_V_DIM) + MLA_NOPE_DIM
        cols += [b + i for i in range(MLA_V_DIM)]
    return np.asarray(cols, np.int32)


def _gather_columns(w, cols):
    picked = jnp.take(w, jnp.asarray(np.maximum(cols, 0)), axis=1)
    return jnp.where(jnp.asarray(cols >= 0)[None, :], picked, 0.0).astype(BF16)


def _rope_tables(seq):
    pos = jnp.arange(seq, dtype=F32)

    def cs(half):
        inv_freq = 1.0 / (ROPE_THETA ** (jnp.arange(half, dtype=F32) / half))
        ang = pos[:, None] * inv_freq[None, :]
        return jnp.cos(ang), jnp.sin(ang)

    c, s = cs(MLA_ROPE_DIM // 2)
    one = lambda n: jnp.ones((seq, n), F32)
    zero = lambda n: jnp.zeros((seq, n), F32)
    cos_mla = jnp.concatenate([one(32), c, one(16), one(32), c, one(16)], axis=1)
    sin_mla = jnp.concatenate([zero(32), -s, zero(16), zero(32), s, zero(16)], axis=1)
    c, s = cs(MOBA_HEAD_DIM // 2)
    cos_moba = jnp.concatenate([c, c, c, c], axis=1)
    sin_moba = jnp.concatenate([-s, -s, s, s], axis=1)
    return cos_mla, sin_mla, cos_moba, sin_moba


def _rms(x, g):
    ms = jnp.mean(x * x, axis=-1, keepdims=True)
    return x * lax.rsqrt(ms + NORM_EPS) * g


def _rope_rows(x, cos, sin):
    return x * cos + pltpu.roll(x, HALF, 1) * sin


def _rope_cols(x, cos, sin):
    return x * cos + jnp.concatenate([x[HALF:], x[:HALF]], axis=0) * sin


def _proj_kernel(x_ref, g_ref, win_ref, wint_ref, gq_ref, wqt_ref, gkv_ref, wk_ref, wvt_ref,
                 cm_ref, sm_ref, co_ref, so_ref, cmt_ref, smt_ref, cot_ref, sot_ref,
                 mqt_ref, mk_ref, mvt_ref, oqt_ref, oqft_ref, ok_ref, ovt_ref, kmean_ref,
                 *, mla_scale, moba_scale):
    subs = [slice(r * PROJ_SUB_TILE, (r + 1) * PROJ_SUB_TILE)
            for r in range(x_ref.shape[0] // PROJ_SUB_TILE)]

    def normed(rows):
        return _rms(x_ref[rows, :], g_ref[...]).astype(BF16)

    u = normed(subs[0])
    for r, rows in enumerate(subs):
        u = _proj_rows(r, rows, u, (lambda nxt=subs[r + 1]: normed(nxt)) if r + 1 < len(subs) else None,
                       win_ref, wint_ref, gq_ref, wqt_ref, gkv_ref, wk_ref, wvt_ref,
                       cm_ref, sm_ref, co_ref, so_ref, cmt_ref, smt_ref, cot_ref, sot_ref,
                       mqt_ref, mk_ref, mvt_ref, oqt_ref, oqft_ref, ok_ref, ovt_ref, kmean_ref,
                       mla_scale, moba_scale)


def _proj_rows(r, rows, u, next_u, win_ref, wint_ref, gq_ref, wqt_ref, gkv_ref, wk_ref, wvt_ref,
               cm_ref, sm_ref, co_ref, so_ref, cmt_ref, smt_ref, cot_ref, sot_ref,
               mqt_ref, mk_ref, mvt_ref, oqt_ref, oqft_ref, ok_ref, ovt_ref, kmean_ref,
               mla_scale, moba_scale):
    def proj(lo, hi):
        return jnp.dot(u, win_ref[:, lo:hi], preferred_element_type=F32)

    def proj_t(lo, hi):
        return lax.dot_general(wint_ref[lo:hi, :], u, _NT, preferred_element_type=F32)

    c0 = MLA_Q_RANK
    c1 = c0 + MLA_KV_RANK
    c2 = c1 + LANES
    c3 = c2 + MOBA_WIDTH

    latent = proj(0, c2)
    u_following = next_u() if next_u is not None else None
    cq = _rms(latent[:, :c0], gq_ref[...]).astype(BF16)
    qt = lax.dot_general(wqt_ref[...], cq, _NT, preferred_element_type=F32)
    cmt, smt = cmt_ref[:, rows], smt_ref[:, rows]
    for h in range(MLA_HEADS):
        sl = slice(h * LANES, (h + 1) * LANES)
        mqt_ref[0, sl, rows] = (_rope_cols(qt[sl], cmt, smt) * mla_scale).astype(BF16)

    ckv = _rms(latent[:, c0:c1], gkv_ref[...]).astype(BF16)
    k_shared = _rope_rows(latent[:, c1:c2], cm_ref[rows, :], sm_ref[rows, :])
    k_nope = jnp.dot(ckv, wk_ref[...], preferred_element_type=F32)
    for h in range(MLA_HEADS):
        sl = slice(h * LANES, (h + 1) * LANES)
        mk_ref[rows, sl] = (k_nope[:, sl] + k_shared).astype(BF16)
    mvt_ref[0, :, rows] = lax.dot_general(wvt_ref[...], ckv, _NT,
                                          preferred_element_type=F32).astype(BF16)

    oqt = proj_t(0, MOBA_WIDTH)
    ok = proj(c2, c3)
    co, so = co_ref[rows, :], so_ref[rows, :]
    cot, sot = cot_ref[:, rows], sot_ref[:, rows]
    for p in range(MOBA_WIDTH // LANES):
        sl = slice(p * LANES, (p + 1) * LANES)
        q_rot = _rope_cols(oqt[sl], cot, sot) * moba_scale
        q_hi = q_rot.astype(BF16)
        oqt_ref[0, sl, rows] = q_hi
        oqft_ref[0, sl, rows] = (q_rot - q_hi.astype(F32)).astype(BF16)
        k_rot = _rope_rows(ok[:, sl], co, so)
        ok_ref[rows, sl] = k_rot.astype(BF16)
        for b in range(PROJ_SUB_TILE // MOBA_BLOCK):
            kmean_ref[r * (PROJ_SUB_TILE // MOBA_BLOCK) + b, :, sl] = jnp.mean(
                k_rot[b * MOBA_BLOCK:(b + 1) * MOBA_BLOCK], axis=0, keepdims=True)
    ovt_ref[0, :, rows] = proj_t(MOBA_WIDTH, 2 * MOBA_WIDTH).astype(BF16)
    return u_following


def _schedule(n_q):
    tiles = [q for q in range(n_q) for _ in range(q)]
    chunks = [c for q in range(n_q) for c in range(q)]
    return np.asarray(tiles + tiles[-1:], np.int32), np.asarray(chunks + chunks[-1:], np.int32)


def _heads_to_rows(out_t, n_heads, tq):
    return jnp.concatenate([out_t[:, h * tq:(h + 1) * tq] for h in range(n_heads)], axis=0).T


def _attend_row(tile_tab, chunk_tab, q_cat_of, k_ref, vt_ref, o_ref, s_bufs, st_ref, tri_ref,
                n_heads, bias_of=None):
    tq = tk = ATTN_TILE
    n_q = st_ref.shape[0]
    n_steps = n_q * (n_q - 1) // 2
    unroll = next(u for u in (14, 4, 2) if n_steps % u == 0)
    diag_unroll = 4 if n_q % 4 == 0 else 2
    width = n_heads * tq
    d_v = vt_ref.shape[1] // n_heads
    acc_rows = d_v + SUBLANES_BF16
    ones = jnp.ones((SUBLANES_BF16, tk), BF16)

    def scores(tile, chunk, s_ref, diagonal):
        start = pl.multiple_of(chunk * tk, tk)
        s = jnp.dot(k_ref[0, pl.ds(start, tk), :], q_cat_of(tile), preferred_element_type=F32)
        if bias_of is not None:
            per_chunk = tk // MOBA_BLOCK
            s = jnp.concatenate(
                [s[j * MOBA_BLOCK:(j + 1) * MOBA_BLOCK] + bias_of(tile, per_chunk * chunk + j)
                 for j in range(per_chunk)], axis=0)
        if diagonal:
            tri = tri_ref[...]
            s = jnp.concatenate([s[:, h * tq:(h + 1) * tq] + tri for h in range(n_heads)], axis=1)
        s_ref[...] = s
        return jnp.max(s, axis=0, keepdims=True)

    def accumulate(chunk, state, s, s_max):
        m, acc = state
        m_new = jnp.maximum(m, s_max)
        alpha = jnp.exp2(m - m_new)
        p = jnp.exp2(s - m_new).astype(BF16)
        start = pl.multiple_of(chunk * tk, tk)
        pv = [jnp.dot(jnp.concatenate([vt_ref[0, h * d_v:(h + 1) * d_v, pl.ds(start, tk)], ones], axis=0),
                      p[:, h * tq:(h + 1) * tq], preferred_element_type=F32)
              for h in range(n_heads)]
        return m_new, alpha * acc + jnp.concatenate(pv, axis=1)

    def save(tile, state):
        m, acc = state
        st_ref[tile, :acc_rows, :] = acc
        st_ref[tile, acc_rows:, :] = jnp.broadcast_to(m, (SUBLANES, width))

    def full_steps(i, carry):
        state, s_max = carry
        for j in range(unroll):
            t = i * unroll + j
            next_max = scores(tile_tab[t + 1], chunk_tab[t + 1], s_bufs[(j + 1) % 2], False)
            tile, chunk = tile_tab[t], chunk_tab[t]
            m, acc = state
            fresh = chunk == 0
            state = accumulate(chunk, (jnp.where(fresh, NEG_INF, m), jnp.where(fresh, 0.0, acc)),
                               s_bufs[j % 2][...], s_max)
            save(tile, state)
            s_max = next_max
        return state, s_max

    def diagonal_steps(i, s_max):
        for j in range(diag_unroll):
            tile = diag_unroll * i + j
            following = jnp.minimum(tile + 1, n_q - 1)
            next_max = scores(following, following, s_bufs[(j + 1) % 2], True)
            state = (st_ref[tile, acc_rows:acc_rows + 1, :], st_ref[tile, :acc_rows, :])
            _, acc = accumulate(tile, state, s_bufs[j % 2][...], s_max)
            out_t = acc[:d_v] * (1.0 / acc[d_v:d_v + 1])
            o_ref[0, pl.ds(pl.multiple_of(tile * tq, tq), tq), :] = (
                _heads_to_rows(out_t, n_heads, tq).astype(BF16))
            s_max = next_max
        return s_max

    empty = (jnp.full((1, width), NEG_INF, F32), jnp.zeros((acc_rows, width), F32))
    save(0, empty)
    lax.fori_loop(0, n_steps // unroll, full_steps,
                  (empty, scores(tile_tab[0], chunk_tab[0], s_bufs[0], False)))
    lax.fori_loop(0, n_q // diag_unroll, diagonal_steps, scores(0, 0, s_bufs[0], True))


def _init_causal_bias(tri_ref):
    @pl.when((pl.program_id(0) == 0) & (pl.program_id(1) == 0))
    def _():
        key = lax.broadcasted_iota(jnp.int32, tri_ref.shape, 0)
        qry = lax.broadcasted_iota(jnp.int32, tri_ref.shape, 1)
        tri_ref[...] = jnp.where(key <= qry, 0.0, NEG_INF)


def _mla_attn_kernel(tile_tab, chunk_tab, qt_ref, k_ref, vt_ref, o_ref, s_a, s_b, st_ref, tri_ref):
    tq = ATTN_TILE
    _init_causal_bias(tri_ref)

    def q_cat_of(tile):
        q = qt_ref[0, :, pl.ds(pl.multiple_of(tile * tq, tq), tq)]
        zero = jnp.zeros((LANES, tq), BF16)
        return jnp.concatenate([jnp.concatenate([q[:LANES], zero], axis=1),
                                jnp.concatenate([zero, q[LANES:]], axis=1)], axis=0)

    _attend_row(tile_tab, chunk_tab, q_cat_of, k_ref, vt_ref, o_ref, (s_a, s_b), st_ref, tri_ref, 2)


def _moba_gate_kernel(q_hi_ref, q_lo_ref, km_ref, bias_ref, *, n_blocks):
    tq = ATTN_TILE
    feat = lax.broadcasted_iota(jnp.int32, (LANES, tq), 0)
    col = lax.broadcasted_iota(jnp.int32, (1, 2 * tq), 1)
    km = km_ref[0]
    km_hi = km.astype(BF16)
    km_lo = (km - km_hi.astype(F32)).astype(BF16)

    def heads_side_by_side(q):
        return jnp.concatenate(
            [jnp.where((feat & (MOBA_HEAD_DIM // 2)) == h * (MOBA_HEAD_DIM // 2), q,
                       jnp.zeros((), BF16)) for h in range(2)], axis=1)

    for tile in range(q_hi_ref.shape[2] // tq):
        cur = (tile * tq + (col & (tq - 1))) // MOBA_BLOCK
        n_seen = (tile + 1) * tq // MOBA_BLOCK
        rows = min(n_blocks, -(-n_seen // SUBLANES) * SUBLANES)
        blk = lax.broadcasted_iota(jnp.int32, (rows, 2 * tq), 0)
        q_hi = heads_side_by_side(q_hi_ref[0, :, tile * tq:(tile + 1) * tq])
        q_lo = heads_side_by_side(q_lo_ref[0, :, tile * tq:(tile + 1) * tq])
        dot = lambda a, b: jnp.dot(a[:rows], b, preferred_element_type=F32)
        gate = (dot(km_lo, q_lo) + dot(km_lo, q_hi)) + dot(km_hi, q_lo) + dot(km_hi, q_hi)
        gate = jnp.where(blk < cur, gate, -jnp.inf)
        ahead_count = jnp.zeros((rows, 2 * tq), F32)
        for jp in range(n_seen - 1):
            g = gate[jp:jp + 1, :]
            ahead = jnp.where(jp < blk, jnp.where(g >= gate, 1.0, 0.0), jnp.where(g > gate, 1.0, 0.0))
            ahead_count = ahead_count + ahead
        keep = ((blk < cur) & (ahead_count < MOBA_TOPK)) | (blk == cur)
        base = tile * n_blocks
        bias_ref[0, 0, base:base + rows, :] = jnp.where(keep, 0.0, NEG_INF)
        if rows < n_blocks:
            bias_ref[0, 0, base + rows:base + n_blocks, :] = jnp.full((n_blocks - rows, 2 * tq),
                                                                      NEG_INF, F32)


def _moba_attn_kernel(tile_tab, chunk_tab, qt_ref, k_ref, vt_ref, bias_ref, o_ref,
                      s_a, s_b, st_ref, tri_ref, *, n_blocks):
    tq = ATTN_TILE
    _init_causal_bias(tri_ref)
    feat = lax.broadcasted_iota(jnp.int32, (LANES, tq), 0)

    def q_cat_of(tile):
        q = qt_ref[0, :, pl.ds(pl.multiple_of(tile * tq, tq), tq)]
        return jnp.concatenate(
            [jnp.where((feat & (MOBA_HEAD_DIM // 2)) == h * (MOBA_HEAD_DIM // 2), q,
                       jnp.zeros((), BF16)) for h in range(2)], axis=1)

    def bias_of(tile, block):
        return bias_ref[0, 0, pl.ds(tile * n_blocks + block, 1), :]

    _attend_row(tile_tab, chunk_tab, q_cat_of, k_ref, vt_ref, o_ref, (s_a, s_b), st_ref, tri_ref, 2,
                bias_of)


def _mlp_kernel(x_ref, a_ref, b_ref, wo_ref, g1_ref, g2_ref, wu_ref, wd_ref, g3_ref, o_ref,
                *, ff_chunk, sub_tile):
    n_a = a_ref.shape[1]
    subs = [slice(r * sub_tile, (r + 1) * sub_tile) for r in range(x_ref.shape[0] // sub_tile)]

    def attn_out(rows):
        return (jnp.dot(a_ref[rows, :], wo_ref[0:n_a, :], preferred_element_type=F32)
                + jnp.dot(b_ref[rows, :], wo_ref[n_a:, :], preferred_element_type=F32))

    def before(rows, y):
        h = x_ref[rows, :] + _rms(y, g1_ref[...])
        return h, _rms(h, g2_ref[...]).astype(BF16)

    def after(rows, h, acc):
        o_ref[rows, :] = h + _rms(acc, g3_ref[...])

    ys = [attn_out(rows) for rows in subs]
    ready = before(subs[0], ys[0])
    done = None
    for r, rows in enumerate(subs):
        h, u = ready
        acc = jnp.zeros(h.shape, F32)
        for c in range(wu_ref.shape[1] // ff_chunk):
            sl = slice(c * ff_chunk, (c + 1) * ff_chunk)
            a = jnp.maximum(jnp.dot(u, wu_ref[:, sl], preferred_element_type=F32), 0.0)
            acc = acc + jnp.dot((a * a).astype(BF16), wd_ref[sl, :], preferred_element_type=F32)
            if c == 0:
                if r + 1 < len(subs):
                    ready = before(subs[r + 1], ys[r + 1])
                if done is not None:
                    after(*done)
        done = (rows, h, acc)
    after(*done)


def _const_spec(shape):
    return pl.BlockSpec(shape, lambda *_: (0,) * len(shape), pipeline_mode=pl.Buffered(1))


def _layer(x, attn_pre_g, w_in, mla_q_norm_g, w_mla_q_up, mla_kv_norm_g, w_mla_kv_up,
           w_out, attn_post_g, mlp_pre_g, w_up, w_down, mlp_post_g):
    B, S, D = x.shape
    T = B * S
    n_blocks = S // MOBA_BLOCK
    n_q = S // ATTN_TILE
    assert S % ATTN_TILE == 0 and ATTN_TILE % MOBA_BLOCK == 0
    assert n_q % 2 == 0 and (n_q * (n_q - 1) // 2) % 2 == 0
    row = lambda g: g.reshape(1, -1).astype(F32)

    cols_rm, cols_fm = _w_in_columns()
    win = _gather_columns(w_in, cols_rm)
    wint = _gather_columns(w_in, cols_fm).T
    wqt = _gather_columns(w_mla_q_up, _mla_q_columns()).T
    wk = _gather_columns(w_mla_kv_up, _mla_k_columns())
    wvt = _gather_columns(w_mla_kv_up, _mla_v_columns()).T
    tables = _rope_tables(S)
    tables_t = tuple(t.T for t in tables)

    tm = PROJ_TILE
    n_st = S // tm
    tok = lambda w: pl.BlockSpec((tm, w), lambda i: (i, 0))
    tok_t = lambda w: pl.BlockSpec((1, w, tm), lambda i: (i // n_st, 0, i % n_st))
    pos = pl.BlockSpec((tm, LANES), lambda i: (i % n_st, 0))
    pos_t = pl.BlockSpec((LANES, tm), lambda i: (0, i % n_st))
    x2 = x.reshape(T, D)
    log2e = math.log2(math.e)
    mqt, mk, mvt, oqt, oqft, ok, ovt, kmean = pl.pallas_call(
        functools.partial(_proj_kernel,
                          mla_scale=float((MLA_NOPE_DIM + MLA_ROPE_DIM) ** -0.5 * log2e),
                          moba_scale=float(MOBA_HEAD_DIM ** -0.5 * log2e)),
        grid=(T // tm,),
        in_specs=[tok(D), _const_spec((1, D)), _const_spec(win.shape), _const_spec(wint.shape),
                  _const_spec((1, MLA_Q_RANK)), _const_spec(wqt.shape),
                  _const_spec((1, MLA_KV_RANK)), _const_spec(wk.shape), _const_spec(wvt.shape),
                  pos, pos, pos, pos, pos_t, pos_t, pos_t, pos_t],
        out_specs=[tok_t(MLA_PAD_WIDTH), tok(MLA_PAD_WIDTH), tok_t(MLA_WIDTH),
                   tok_t(MOBA_WIDTH), tok_t(MOBA_WIDTH), tok(MOBA_WIDTH), tok_t(MOBA_WIDTH),
                   pl.BlockSpec((tm // MOBA_BLOCK, 1, MOBA_WIDTH), lambda i: (i, 0, 0))],
        out_shape=[jax.ShapeDtypeStruct((B, MLA_PAD_WIDTH, S), BF16),
                   jax.ShapeDtypeStruct((T, MLA_PAD_WIDTH), BF16),
                   jax.ShapeDtypeStruct((B, MLA_WIDTH, S), BF16),
                   jax.ShapeDtypeStruct((B, MOBA_WIDTH, S), BF16),
                   jax.ShapeDtypeStruct((B, MOBA_WIDTH, S), BF16),
                   jax.ShapeDtypeStruct((T, MOBA_WIDTH), BF16),
                   jax.ShapeDtypeStruct((B, MOBA_WIDTH, S), BF16),
                   jax.ShapeDtypeStruct((T // MOBA_BLOCK, 1, MOBA_WIDTH), F32)],
        compiler_params=pltpu.CompilerParams(vmem_limit_bytes=VMEM_LIMIT),
        name="token_projection",
    )(x2, row(attn_pre_g), win, wint, row(mla_q_norm_g), wqt, row(mla_kv_norm_g), wk, wvt,
      *tables, *tables_t)

    tq = ATTN_TILE
    width = 2 * tq
    tile_tab, chunk_tab = (jnp.asarray(t) for t in _schedule(n_q))
    seq_t = lambda w: pl.BlockSpec((1, w, S), lambda b, p, *_: (b, p, 0))
    seq = lambda w: pl.BlockSpec((1, S, w), lambda b, p, *_: (b, 0, p))
    attn_params = pltpu.CompilerParams(vmem_limit_bytes=VMEM_LIMIT)
    attn_scratch = [pltpu.VMEM((tq, width), F32), pltpu.VMEM((tq, width), F32),
                    pltpu.VMEM((n_q, MLA_V_DIM + SUBLANES_BF16 + SUBLANES, width), F32),
                    pltpu.VMEM((tq, tq), F32)]

    mla_o = pl.pallas_call(
        _mla_attn_kernel,
        grid_spec=pltpu.PrefetchScalarGridSpec(
            num_scalar_prefetch=2, grid=(B, MLA_HEADS // 2),
            in_specs=[seq_t(2 * LANES), seq(2 * LANES), seq_t(LANES)],
            out_specs=seq(LANES), scratch_shapes=attn_scratch),
        out_shape=jax.ShapeDtypeStruct((B, S, MLA_WIDTH), BF16),
        compiler_params=attn_params,
        name="mla_attention",
    )(tile_tab, chunk_tab, mqt, mk.reshape(B, S, MLA_PAD_WIDTH), mvt)

    moba_bias = pl.pallas_call(
        functools.partial(_moba_gate_kernel, n_blocks=n_blocks),
        grid=(B, MOBA_HEADS // 2),
        in_specs=[pl.BlockSpec((1, LANES, S), lambda b, p: (b, p, 0)),
                  pl.BlockSpec((1, LANES, S), lambda b, p: (b, p, 0)),
                  pl.BlockSpec((1, n_blocks, LANES), lambda b, p: (b, 0, p))],
        out_specs=pl.BlockSpec((1, 1, n_q * n_blocks, width), lambda b, p: (b, p, 0, 0)),
        out_shape=jax.ShapeDtypeStruct((B, MOBA_HEADS // 2, n_q * n_blocks, width), F32),
        name="moba_gate",
    )(oqt, oqft, kmean.reshape(B, n_blocks, MOBA_WIDTH))

    moba_o = pl.pallas_call(
        functools.partial(_moba_attn_kernel, n_blocks=n_blocks),
        grid_spec=pltpu.PrefetchScalarGridSpec(
            num_scalar_prefetch=2, grid=(B, MOBA_HEADS // 2),
            in_specs=[seq_t(LANES), seq(LANES), seq_t(LANES),
                      pl.BlockSpec((1, 1, n_q * n_blocks, width), lambda b, p, *_: (b, p, 0, 0))],
            out_specs=seq(LANES), scratch_shapes=attn_scratch),
        out_shape=jax.ShapeDtypeStruct((B, S, MOBA_WIDTH), BF16),
        compiler_params=attn_params,
        name="moba_attention",
    )(tile_tab, chunk_tab, oqt, ok.reshape(B, S, MOBA_WIDTH), ovt, moba_bias)

    tm = MLP_TILE
    d_ff = w_up.shape[1]
    tok = lambda w: pl.BlockSpec((tm, w), lambda i: (i, 0))
    out = pl.pallas_call(
        functools.partial(_mlp_kernel, ff_chunk=FF_CHUNK, sub_tile=MLP_SUB_TILE),
        grid=(T // tm,),
        in_specs=[tok(D), tok(MLA_WIDTH), tok(MOBA_WIDTH),
                  _const_spec((MLA_WIDTH + MOBA_WIDTH, D)), _const_spec((1, D)), _const_spec((1, D)),
                  _const_spec((D, d_ff)), _const_spec((d_ff, D)), _const_spec((1, D))],
        out_specs=tok(D),
        out_shape=jax.ShapeDtypeStruct((T, D), F32),
        compiler_params=pltpu.CompilerParams(vmem_limit_bytes=VMEM_LIMIT),
        name="out_proj_mlp",
    )(x2, mla_o.reshape(T, MLA_WIDTH), moba_o.reshape(T, MOBA_WIDTH),
      w_out.astype(BF16), row(attn_post_g), row(mlp_pre_g),
      w_up.astype(BF16), w_down.astype(BF16), row(mlp_post_g))
    return out.reshape(B, S, D)


def kernel(x, attn_pre_g, w_in, mla_q_norm_g, w_mla_q_up, mla_kv_norm_g, w_mla_kv_up, w_out,
           attn_post_g, mlp_pre_g, w_up, w_down, mlp_post_g):
    h = x
    for l in range(w_in.shape[0]):
        h = _layer(h, attn_pre_g[l], w_in[l], mla_q_norm_g[l], w_mla_q_up[l], mla_kv_norm_g[l],
                   w_mla_kv_up[l], w_out[l], attn_post_g[l], mlp_pre_g[l], w_up[l], w_down[l],
                   mlp_post_g[l])
    return h
```

```python
import functools
import math

import numpy as np
import jax
import jax.numpy as jnp
from jax import lax
from jax.experimental import pallas as pl
from jax.experimental.pallas import tpu as pltpu

MLA_HEADS = 8
MLA_NOPE_DIM = 64
MLA_ROPE_DIM = 32
MLA_V_DIM = 64
MLA_Q_RANK = 256
MLA_KV_RANK = 128
MOBA_HEADS = 8
MOBA_HEAD_DIM = 64
MOBA_BLOCK = 256
MOBA_TOPK = 3
ROPE_THETA = 10000.0
NORM_EPS = 1e-6
NEG_INF = -1e30

LANES = 128
SUBLANES = 8
SUBLANES_BF16 = 16
HALF = LANES // 2
MOBA_WIDTH = MOBA_HEADS * MOBA_HEAD_DIM
MLA_WIDTH = MLA_HEADS * MLA_V_DIM
MLA_PAD_WIDTH = MLA_HEADS * LANES
PROJ_TILE = 1024
PROJ_SUB_TILE = 512
ATTN_TILE = 512
MLP_TILE = 1024
MLP_SUB_TILE = 256
FF_CHUNK = 1024
VMEM_LIMIT = 56 * 1024 * 1024

F32 = jnp.float32
BF16 = jnp.bfloat16
_NT = (((1,), (1,)), ((), ()))


def _moba_pair_columns(base):
    cols = []
    for p in range(MOBA_HEADS // 2):
        for off in (0, MOBA_HEAD_DIM // 2):
            for h in (2 * p, 2 * p + 1):
                cols += [base + h * MOBA_HEAD_DIM + off + i for i in range(MOBA_HEAD_DIM // 2)]
    return cols


def _w_in_columns():
    s2 = MLA_Q_RANK + MLA_KV_RANK
    s3 = s2 + MLA_ROPE_DIM
    half = MLA_ROPE_DIM // 2
    shared = [-1] * LANES
    for i in range(half):
        shared[32 + i] = s2 + i
        shared[96 + i] = s2 + half + i
    row_major = list(range(0, s2)) + shared + _moba_pair_columns(s3 + MOBA_WIDTH)
    feature_major = _moba_pair_columns(s3) + list(range(s3 + 2 * MOBA_WIDTH, s3 + 3 * MOBA_WIDTH))
    return np.asarray(row_major, np.int32), np.asarray(feature_major, np.int32)


def _mla_q_columns():
    cols = []
    half = MLA_ROPE_DIM // 2
    for h in range(MLA_HEADS):
        b = h * (MLA_NOPE_DIM + MLA_ROPE_DIM)
        cols += [b + i for i in range(32)] + [b + MLA_NOPE_DIM + i for i in range(half)] + [-1] * 16
        cols += [b + 32 + i for i in range(32)] + [b + MLA_NOPE_DIM + half + i for i in range(half)] + [-1] * 16
    return np.asarray(cols, np.int32)


def _mla_k_columns():
    cols = []
    for h in range(MLA_HEADS):
        b = h * (MLA_NOPE_DIM + MLA_V_DIM)
        cols += [b + i for i in range(32)] + [-1] * 32 + [b + 32 + i for i in range(32)] + [-1] * 32
    return np.asarray(cols, np.int32)


def _mla_v_columns():
    cols = []
    for h in range(MLA_HEADS):
        b = h * (MLA_NOPE_DIM + MLA_V_DIM) + MLA_NOPE_DIM
        cols += [b + i for i in range(MLA_V_DIM)]
    return np.asarray(cols, np.int32)


def _gather_columns(w, cols):
    picked = jnp.take(w, jnp.asarray(np.maximum(cols, 0)), axis=1)
    return jnp.where(jnp.asarray(cols >= 0)[None, :], picked, 0.0).astype(BF16)


def _rope_tables(seq):
    pos = jnp.arange(seq, dtype=F32)

    def cs(half):
        inv_freq = 1.0 / (ROPE_THETA ** (jnp.arange(half, dtype=F32) / half))
        ang = pos[:, None] * inv_freq[None, :]
        return jnp.cos(ang), jnp.sin(ang)

    c, s = cs(MLA_ROPE_DIM // 2)
    one = lambda n: jnp.ones((seq, n), F32)
    zero = lambda n: jnp.zeros((seq, n), F32)
    cos_mla = jnp.concatenate([one(32), c, one(16), one(32), c, one(16)], axis=1)
    sin_mla = jnp.concatenate([zero(32), -s, zero(16), zero(32), s, zero(16)], axis=1)
    c, s = cs(MOBA_HEAD_DIM // 2)
    cos_moba = jnp.concatenate([c, c, c, c], axis=1)
    sin_moba = jnp.concatenate([-s, -s, s, s], axis=1)
    return cos_mla, sin_mla, cos_moba, sin_moba


def _rms(x, g):
    ms = jnp.mean(x * x, axis=-1, keepdims=True)
    return x * lax.rsqrt(ms + NORM_EPS) * g


def _rope_rows(x, cos, sin):
    return x * cos + pltpu.roll(x, HALF, 1) * sin


def _rope_cols(x, cos, sin):
    return x * cos + jnp.concatenate([x[HALF:], x[:HALF]], axis=0) * sin


def _proj_kernel(x_ref, g_ref, win_ref, wint_ref, gq_ref, wqt_ref, gkv_ref, wk_ref, wvt_ref,
                 cm_ref, sm_ref, co_ref, so_ref, cmt_ref, smt_ref, cot_ref, sot_ref,
                 mqt_ref, mk_ref, mvt_ref, oqt_ref, oqft_ref, ok_ref, ovt_ref, kmean_ref,
                 *, mla_scale, moba_scale):
    subs = [slice(r * PROJ_SUB_TILE, (r + 1) * PROJ_SUB_TILE)
            for r in range(x_ref.shape[0] // PROJ_SUB_TILE)]

    def normed(rows):
        return _rms(x_ref[rows, :], g_ref[...]).astype(BF16)

    u = normed(subs[0])
    for r, rows in enumerate(subs):
        u = _proj_rows(r, rows, u, (lambda nxt=subs[r + 1]: normed(nxt)) if r + 1 < len(subs) else None,
                       win_ref, wint_ref, gq_ref, wqt_ref, gkv_ref, wk_ref, wvt_ref,
                       cm_ref, sm_ref, co_ref, so_ref, cmt_ref, smt_ref, cot_ref, sot_ref,
                       mqt_ref, mk_ref, mvt_ref, oqt_ref, oqft_ref, ok_ref, ovt_ref, kmean_ref,
                       mla_scale, moba_scale)


def _proj_rows(r, rows, u, next_u, win_ref, wint_ref, gq_ref, wqt_ref, gkv_ref, wk_ref, wvt_ref,
               cm_ref, sm_ref, co_ref, so_ref, cmt_ref, smt_ref, cot_ref, sot_ref,
               mqt_ref, mk_ref, mvt_ref, oqt_ref, oqft_ref, ok_ref, ovt_ref, kmean_ref,
               mla_scale, moba_scale):
    def proj(lo, hi):
        return jnp.dot(u, win_ref[:, lo:hi], preferred_element_type=F32)

    def proj_t(lo, hi):
        return lax.dot_general(wint_ref[lo:hi, :], u, _NT, preferred_element_type=F32)

    c0 = MLA_Q_RANK
    c1 = c0 + MLA_KV_RANK
    c2 = c1 + LANES
    c3 = c2 + MOBA_WIDTH

    latent = proj(0, c2)
    u_following = next_u() if next_u is not None else None
    cq = _rms(latent[:, :c0], gq_ref[...]).astype(BF16)
    qt = lax.dot_general(wqt_ref[...], cq, _NT, preferred_element_type=F32)
    cmt, smt = cmt_ref[:, rows], smt_ref[:, rows]
    for h in range(MLA_HEADS):
        sl = slice(h * LANES, (h + 1) * LANES)
        mqt_ref[0, sl, rows] = (_rope_cols(qt[sl], cmt, smt) * mla_scale).astype(BF16)

    ckv = _rms(latent[:, c0:c1], gkv_ref[...]).astype(BF16)
    k_shared = _rope_rows(latent[:, c1:c2], cm_ref[rows, :], sm_ref[rows, :])
    k_nope = jnp.dot(ckv, wk_ref[...], preferred_element_type=F32)
    for h in range(MLA_HEADS):
        sl = slice(h * LANES, (h + 1) * LANES)
        mk_ref[rows, sl] = (k_nope[:, sl] + k_shared).astype(BF16)
    mvt_ref[0, :, rows] = lax.dot_general(wvt_ref[...], ckv, _NT,
                                          preferred_element_type=F32).astype(BF16)

    oqt = proj_t(0, MOBA_WIDTH)
    ok = proj(c2, c3)
    co, so = co_ref[rows, :], so_ref[rows, :]
    cot, sot = cot_ref[:, rows], sot_ref[:, rows]
    for p in range(MOBA_WIDTH // LANES):
        sl = slice(p * LANES, (p + 1) * LANES)
        q_rot = _rope_cols(oqt[sl], cot, sot) * moba_scale
        q_hi = q_rot.astype(BF16)
        oqt_ref[0, sl, rows] = q_hi
        oqft_ref[0, sl, rows] = (q_rot - q_hi.astype(F32)).astype(BF16)
        k_rot = _rope_rows(ok[:, sl], co, so)
        ok_ref[rows, sl] = k_rot.astype(BF16)
        for b in range(PROJ_SUB_TILE // MOBA_BLOCK):
            kmean_ref[r * (PROJ_SUB_TILE // MOBA_BLOCK) + b, :, sl] = jnp.mean(
                k_rot[b * MOBA_BLOCK:(b + 1) * MOBA_BLOCK], axis=0, keepdims=True)
    ovt_ref[0, :, rows] = proj_t(MOBA_WIDTH, 2 * MOBA_WIDTH).astype(BF16)
    return u_following


def _schedule(n_q):
    tiles = [q for q in range(n_q) for _ in range(q)]
    chunks = [c for q in range(n_q) for c in range(q)]
    return np.asarray(tiles + tiles[-1:], np.int32), np.asarray(chunks + chunks[-1:], np.int32)


def _heads_to_rows(out_t, n_heads, tq):
    return jnp.concatenate([out_t[:, h * tq:(h + 1) * tq] for h in range(n_heads)], axis=0).T


def _attend_row(tile_tab, chunk_tab, q_cat_of, k_ref, vt_ref, o_ref, s_bufs, st_ref, tri_ref,
                n_heads, bias_of=None):
    tq = tk = ATTN_TILE
    n_q = st_ref.shape[0]
    n_steps = n_q * (n_q - 1) // 2
    unroll = next(u for u in (14, 4, 2) if n_steps % u == 0)
    diag_unroll = 4 if n_q % 4 == 0 else 2
    width = n_heads * tq
    d_v = vt_ref.shape[1] // n_heads
    acc_rows = d_v + SUBLANES_BF16
    ones = jnp.ones((SUBLANES_BF16, tk), BF16)

    def scores(tile, chunk, s_ref, diagonal):
        start = pl.multiple_of(chunk * tk, tk)
        s = jnp.concatenate(
            [jnp.dot(k_ref[0, pl.ds(start, tk), lanes], q, preferred_element_type=F32)
             for lanes, q in q_cat_of(tile)], axis=1)
        if bias_of is not None:
            per_chunk = tk // MOBA_BLOCK
            s = jnp.concatenate(
                [s[j * MOBA_BLOCK:(j + 1) * MOBA_BLOCK] + bias_of(tile, per_chunk * chunk + j)
                 for j in range(per_chunk)], axis=0)
        if diagonal:
            tri = tri_ref[...]
            s = jnp.concatenate([s[:, h * tq:(h + 1) * tq] + tri for h in range(n_heads)], axis=1)
        s_ref[...] = s
        return jnp.max(s, axis=0, keepdims=True)

    def accumulate(chunk, state, s, s_max):
        m, acc = state
        m_new = jnp.maximum(m, s_max)
        alpha = jnp.exp2(m - m_new)
        p = jnp.exp2(s - m_new).astype(BF16)
        start = pl.multiple_of(chunk * tk, tk)
        pv = [jnp.dot(jnp.concatenate([vt_ref[0, h * d_v:(h + 1) * d_v, pl.ds(start, tk)], ones], axis=0),
                      p[:, h * tq:(h + 1) * tq], preferred_element_type=F32)
              for h in range(n_heads)]
        return m_new, alpha * acc + jnp.concatenate(pv, axis=1)

    def save(tile, state):
        m, acc = state
        st_ref[tile, :acc_rows, :] = acc
        st_ref[tile, acc_rows:, :] = jnp.broadcast_to(m, (SUBLANES, width))

    def full_steps(i, carry):
        state, s_max = carry
        for j in range(unroll):
            t = i * unroll + j
            next_max = scores(tile_tab[t + 1], chunk_tab[t + 1], s_bufs[(j + 1) % 2], False)
            tile, chunk = tile_tab[t], chunk_tab[t]
            m, acc = state
            fresh = chunk == 0
            state = accumulate(chunk, (jnp.where(fresh, NEG_INF, m), jnp.where(fresh, 0.0, acc)),
                               s_bufs[j % 2][...], s_max)
            save(tile, state)
            s_max = next_max
        return state, s_max

    def diagonal_steps(i, s_max):
        for j in range(diag_unroll):
            tile = diag_unroll * i + j
            following = jnp.minimum(tile + 1, n_q - 1)
            next_max = scores(following, following, s_bufs[(j + 1) % 2], True)
            state = (st_ref[tile, acc_rows:acc_rows + 1, :], st_ref[tile, :acc_rows, :])
            _, acc = accumulate(tile, state, s_bufs[j % 2][...], s_max)
            out_t = acc[:d_v] * (1.0 / acc[d_v:d_v + 1])
            o_ref[0, pl.ds(pl.multiple_of(tile * tq, tq), tq), :] = (
                _heads_to_rows(out_t, n_heads, tq).astype(BF16))
            s_max = next_max
        return s_max

    empty = (jnp.full((1, width), NEG_INF, F32), jnp.zeros((acc_rows, width), F32))
    save(0, empty)
    lax.fori_loop(0, n_steps // unroll, full_steps,
                  (empty, scores(tile_tab[0], chunk_tab[0], s_bufs[0], False)))
    lax.fori_loop(0, n_q // diag_unroll, diagonal_steps, scores(0, 0, s_bufs[0], True))


def _init_causal_bias(tri_ref):
    @pl.when((pl.program_id(0) == 0) & (pl.program_id(1) == 0))
    def _():
        key = lax.broadcasted_iota(jnp.int32, tri_ref.shape, 0)
        qry = lax.broadcasted_iota(jnp.int32, tri_ref.shape, 1)
        tri_ref[...] = jnp.where(key <= qry, 0.0, NEG_INF)


def _mla_attn_kernel(tile_tab, chunk_tab, qt_ref, k_ref, vt_ref, o_ref, s_a, s_b, st_ref, tri_ref):
    tq = ATTN_TILE
    _init_causal_bias(tri_ref)

    def q_cat_of(tile):
        q = qt_ref[0, :, pl.ds(pl.multiple_of(tile * tq, tq), tq)]
        return [(slice(h * LANES, (h + 1) * LANES), q[h * LANES:(h + 1) * LANES]) for h in range(2)]

    _attend_row(tile_tab, chunk_tab, q_cat_of, k_ref, vt_ref, o_ref, (s_a, s_b), st_ref, tri_ref, 2)


def _moba_gate_kernel(q_hi_ref, q_lo_ref, km_ref, bias_ref, *, n_blocks):
    tq = ATTN_TILE
    feat = lax.broadcasted_iota(jnp.int32, (LANES, tq), 0)
    col = lax.broadcasted_iota(jnp.int32, (1, 2 * tq), 1)
    km = km_ref[0]
    km_hi = km.astype(BF16)
    km_lo = (km - km_hi.astype(F32)).astype(BF16)

    def heads_side_by_side(q):
        return jnp.concatenate(
            [jnp.where((feat & (MOBA_HEAD_DIM // 2)) == h * (MOBA_HEAD_DIM // 2), q,
                       jnp.zeros((), BF16)) for h in range(2)], axis=1)

    for tile in range(q_hi_ref.shape[2] // tq):
        cur = (tile * tq + (col & (tq - 1))) // MOBA_BLOCK
        n_seen = (tile + 1) * tq // MOBA_BLOCK
        rows = min(n_blocks, -(-n_seen // SUBLANES) * SUBLANES)
        blk = lax.broadcasted_iota(jnp.int32, (rows, 2 * tq), 0)
        q_hi = heads_side_by_side(q_hi_ref[0, :, tile * tq:(tile + 1) * tq])
        q_lo = heads_side_by_side(q_lo_ref[0, :, tile * tq:(tile + 1) * tq])
        dot = lambda a, b: jnp.dot(a[:rows], b, preferred_element_type=F32)
        gate = (dot(km_lo, q_lo) + dot(km_lo, q_hi)) + dot(km_hi, q_lo) + dot(km_hi, q_hi)
        gate = jnp.where(blk < cur, gate, -jnp.inf)
        ahead_count = jnp.zeros((rows, 2 * tq), F32)
        for jp in range(n_seen - 1):
            g = gate[jp:jp + 1, :]
            ahead = jnp.where(jp < blk, jnp.where(g >= gate, 1.0, 0.0), jnp.where(g > gate, 1.0, 0.0))
            ahead_count = ahead_count + ahead
        keep = ((blk < cur) & (ahead_count < MOBA_TOPK)) | (blk == cur)
        base = tile * n_blocks
        bias_ref[0, 0, base:base + rows, :] = jnp.where(keep, 0.0, NEG_INF)
        if rows < n_blocks:
            bias_ref[0, 0, base + rows:base + n_blocks, :] = jnp.full((n_blocks - rows, 2 * tq),
                                                                      NEG_INF, F32)


def _moba_attn_kernel(tile_tab, chunk_tab, qt_ref, k_ref, vt_ref, bias_ref, o_ref,
                      s_a, s_b, st_ref, tri_ref, *, n_blocks):
    tq = ATTN_TILE
    _init_causal_bias(tri_ref)
    feat = lax.broadcasted_iota(jnp.int32, (LANES, tq), 0)

    def q_cat_of(tile):
        q = qt_ref[0, :, pl.ds(pl.multiple_of(tile * tq, tq), tq)]
        return [(slice(0, LANES), jnp.concatenate(
            [jnp.where((feat & (MOBA_HEAD_DIM // 2)) == h * (MOBA_HEAD_DIM // 2), q,
                       jnp.zeros((), BF16)) for h in range(2)], axis=1))]

    def bias_of(tile, block):
        return bias_ref[0, 0, pl.ds(tile * n_blocks + block, 1), :]

    _attend_row(tile_tab, chunk_tab, q_cat_of, k_ref, vt_ref, o_ref, (s_a, s_b), st_ref, tri_ref, 2,
                bias_of)


def _mlp_kernel(x_ref, a_ref, b_ref, wo_ref, g1_ref, g2_ref, wu_ref, wd_ref, g3_ref, o_ref,
                *, ff_chunk, sub_tile):
    n_a = a_ref.shape[1]
    subs = [slice(r * sub_tile, (r + 1) * sub_tile) for r in range(x_ref.shape[0] // sub_tile)]

    def attn_out(rows):
        return (jnp.dot(a_ref[rows, :], wo_ref[0:n_a, :], preferred_element_type=F32)
                + jnp.dot(b_ref[rows, :], wo_ref[n_a:, :], preferred_element_type=F32))

    def before(rows, y):
        h = x_ref[rows, :] + _rms(y, g1_ref[...])
        return h, _rms(h, g2_ref[...]).astype(BF16)

    def after(rows, h, acc):
        o_ref[rows, :] = h + _rms(acc, g3_ref[...])

    ys = [attn_out(rows) for rows in subs]
    ready = before(subs[0], ys[0])
    done = None
    for r, rows in enumerate(subs):
        h, u = ready
        acc = jnp.zeros(h.shape, F32)
        for c in range(wu_ref.shape[1] // ff_chunk):
            sl = slice(c * ff_chunk, (c + 1) * ff_chunk)
            a = jnp.maximum(jnp.dot(u, wu_ref[:, sl], preferred_element_type=F32), 0.0)
            acc = acc + jnp.dot((a * a).astype(BF16), wd_ref[sl, :], preferred_element_type=F32)
            if c == 0:
                if r + 1 < len(subs):
                    ready = before(subs[r + 1], ys[r + 1])
                if done is not None:
                    after(*done)
        done = (rows, h, acc)
    after(*done)


def _const_spec(shape):
    return pl.BlockSpec(shape, lambda *_: (0,) * len(shape), pipeline_mode=pl.Buffered(1))


def _layer(x, attn_pre_g, w_in, mla_q_norm_g, w_mla_q_up, mla_kv_norm_g, w_mla_kv_up,
           w_out, attn_post_g, mlp_pre_g, w_up, w_down, mlp_post_g):
    B, S, D = x.shape
    T = B * S
    n_blocks = S // MOBA_BLOCK
    n_q = S // ATTN_TILE
    assert S % ATTN_TILE == 0 and ATTN_TILE % MOBA_BLOCK == 0
    assert n_q % 2 == 0 and (n_q * (n_q - 1) // 2) % 2 == 0
    row = lambda g: g.reshape(1, -1).astype(F32)

    cols_rm, cols_fm = _w_in_columns()
    win = _gather_columns(w_in, cols_rm)
    wint = _gather_columns(w_in, cols_fm).T
    wqt = _gather_columns(w_mla_q_up, _mla_q_columns()).T
    wk = _gather_columns(w_mla_kv_up, _mla_k_columns())
    wvt = _gather_columns(w_mla_kv_up, _mla_v_columns()).T
    tables = _rope_tables(S)
    tables_t = tuple(t.T for t in tables)

    tm = PROJ_TILE
    n_st = S // tm
    tok = lambda w: pl.BlockSpec((tm, w), lambda i: (i, 0))
    tok_t = lambda w: pl.BlockSpec((1, w, tm), lambda i: (i // n_st, 0, i % n_st))
    pos = pl.BlockSpec((tm, LANES), lambda i: (i % n_st, 0))
    pos_t = pl.BlockSpec((LANES, tm), lambda i: (0, i % n_st))
    x2 = x.reshape(T, D)
    log2e = math.log2(math.e)
    mqt, mk, mvt, oqt, oqft, ok, ovt, kmean = pl.pallas_call(
        functools.partial(_proj_kernel,
                          mla_scale=float((MLA_NOPE_DIM + MLA_ROPE_DIM) ** -0.5 * log2e),
                          moba_scale=float(MOBA_HEAD_DIM ** -0.5 * log2e)),
        grid=(T // tm,),
        in_specs=[tok(D), _const_spec((1, D)), _const_spec(win.shape), _const_spec(wint.shape),
                  _const_spec((1, MLA_Q_RANK)), _const_spec(wqt.shape),
                  _const_spec((1, MLA_KV_RANK)), _const_spec(wk.shape), _const_spec(wvt.shape),
                  pos, pos, pos, pos, pos_t, pos_t, pos_t, pos_t],
        out_specs=[tok_t(MLA_PAD_WIDTH), tok(MLA_PAD_WIDTH), tok_t(MLA_WIDTH),
                   tok_t(MOBA_WIDTH), tok_t(MOBA_WIDTH), tok(MOBA_WIDTH), tok_t(MOBA_WIDTH),
                   pl.BlockSpec((tm // MOBA_BLOCK, 1, MOBA_WIDTH), lambda i: (i, 0, 0))],
        out_shape=[jax.ShapeDtypeStruct((B, MLA_PAD_WIDTH, S), BF16),
                   jax.ShapeDtypeStruct((T, MLA_PAD_WIDTH), BF16),
                   jax.ShapeDtypeStruct((B, MLA_WIDTH, S), BF16),
                   jax.ShapeDtypeStruct((B, MOBA_WIDTH, S), BF16),
                   jax.ShapeDtypeStruct((B, MOBA_WIDTH, S), BF16),
                   jax.ShapeDtypeStruct((T, MOBA_WIDTH), BF16),
                   jax.ShapeDtypeStruct((B, MOBA_WIDTH, S), BF16),
                   jax.ShapeDtypeStruct((T // MOBA_BLOCK, 1, MOBA_WIDTH), F32)],
        compiler_params=pltpu.CompilerParams(vmem_limit_bytes=VMEM_LIMIT),
        name="token_projection",
    )(x2, row(attn_pre_g), win, wint, row(mla_q_norm_g), wqt, row(mla_kv_norm_g), wk, wvt,
      *tables, *tables_t)

    tq = ATTN_TILE
    width = 2 * tq
    tile_tab, chunk_tab = (jnp.asarray(t) for t in _schedule(n_q))
    seq_t = lambda w: pl.BlockSpec((1, w, S), lambda b, p, *_: (b, p, 0))
    seq = lambda w: pl.BlockSpec((1, S, w), lambda b, p, *_: (b, 0, p))
    attn_params = pltpu.CompilerParams(vmem_limit_bytes=VMEM_LIMIT)
    attn_scratch = [pltpu.VMEM((tq, width), F32), pltpu.VMEM((tq, width), F32),
                    pltpu.VMEM((n_q, MLA_V_DIM + SUBLANES_BF16 + SUBLANES, width), F32),
                    pltpu.VMEM((tq, tq), F32)]

    mla_o = pl.pallas_call(
        _mla_attn_kernel,
        grid_spec=pltpu.PrefetchScalarGridSpec(
            num_scalar_prefetch=2, grid=(B, MLA_HEADS // 2),
            in_specs=[seq_t(2 * LANES), seq(2 * LANES), seq_t(LANES)],
            out_specs=seq(LANES), scratch_shapes=attn_scratch),
        out_shape=jax.ShapeDtypeStruct((B, S, MLA_WIDTH), BF16),
        compiler_params=attn_params,
        name="mla_attention",
    )(tile_tab, chunk_tab, mqt, mk.reshape(B, S, MLA_PAD_WIDTH), mvt)

    moba_bias = pl.pallas_call(
        functools.partial(_moba_gate_kernel, n_blocks=n_blocks),
        grid=(B, MOBA_HEADS // 2),
        in_specs=[pl.BlockSpec((1, LANES, S), lambda b, p: (b, p, 0)),
                  pl.BlockSpec((1, LANES, S), lambda b, p: (b, p, 0)),
                  pl.BlockSpec((1, n_blocks, LANES), lambda b, p: (b, 0, p))],
        out_specs=pl.BlockSpec((1, 1, n_q * n_blocks, width), lambda b, p: (b, p, 0, 0)),
        out_shape=jax.ShapeDtypeStruct((B, MOBA_HEADS // 2, n_q * n_blocks, width), F32),
        name="moba_gate",
    )(oqt, oqft, kmean.reshape(B, n_blocks, MOBA_WIDTH))

    moba_o = pl.pallas_call(
        functools.partial(_moba_attn_kernel, n_blocks=n_blocks),
        grid_spec=pltpu.PrefetchScalarGridSpec(
            num_scalar_prefetch=2, grid=(B, MOBA_HEADS // 2),
            in_specs=[seq_t(LANES), seq(LANES), seq_t(LANES),
                      pl.BlockSpec((1, 1, n_q * n_blocks, width), lambda b, p, *_: (b, p, 0, 0))],
            out_specs=seq(LANES), scratch_shapes=attn_scratch),
        out_shape=jax.ShapeDtypeStruct((B, S, MOBA_WIDTH), BF16),
        compiler_params=attn_params,
        name="moba_attention",
    )(tile_tab, chunk_tab, oqt, ok.reshape(B, S, MOBA_WIDTH), ovt, moba_bias)

    tm = MLP_TILE
    d_ff = w_up.shape[1]
    tok = lambda w: pl.BlockSpec((tm, w), lambda i: (i, 0))
    out = pl.pallas_call(
        functools.partial(_mlp_kernel, ff_chunk=FF_CHUNK, sub_tile=MLP_SUB_TILE),
        grid=(T // tm,),
        in_specs=[tok(D), tok(MLA_WIDTH), tok(MOBA_WIDTH),
                  _const_spec((MLA_WIDTH + MOBA_WIDTH, D)), _const_spec((1, D)), _const_spec((1, D)),
                  _const_spec((D, d_ff)), _const_spec((d_ff, D)), _const_spec((1, D))],
        out_specs=tok(D),
        out_shape=jax.ShapeDtypeStruct((T, D), F32),
        compiler_params=pltpu.CompilerParams(vmem_limit_bytes=VMEM_LIMIT),
        name="out_proj_mlp",
    )(x2, mla_o.reshape(T, MLA_WIDTH), moba_o.reshape(T, MOBA_WIDTH),
      w_out.astype(BF16), row(attn_post_g), row(mlp_pre_g),
      w_up.astype(BF16), w_down.astype(BF16), row(mlp_post_g))
    return out.reshape(B, S, D)


def kernel(x, attn_pre_g, w_in, mla_q_norm_g, w_mla_q_up, mla_kv_norm_g, w_mla_kv_up, w_out,
           attn_post_g, mlp_pre_g, w_up, w_down, mlp_post_g):
    h = x
    for l in range(w_in.shape[0]):
        h = _layer(h, attn_pre_g[l], w_in[l], mla_q_norm_g[l], w_mla_q_up[l], mla_kv_norm_g[l],
                   w_mla_kv_up[l], w_out[l], attn_post_g[l], mlp_pre_g[l], w_up[l], w_down[l],
                   mlp_post_g[l])
    return h
```

```python
import functools
import math

import numpy as np
import jax
import jax.numpy as jnp
from jax import lax
from jax.experimental import pallas as pl
from jax.experimental.pallas import tpu as pltpu

MLA_HEADS = 8
MLA_NOPE_DIM = 64
MLA_ROPE_DIM = 32
MLA_V_DIM = 64
MLA_Q_RANK = 256
MLA_KV_RANK = 128
MOBA_HEADS = 8
MOBA_HEAD_DIM = 64
MOBA_BLOCK = 256
MOBA_TOPK = 3
ROPE_THETA = 10000.0
NORM_EPS = 1e-6
NEG_INF = -1e30

LANES = 128
SUBLANES = 8
SUBLANES_BF16 = 16
HALF = LANES // 2
MOBA_WIDTH = MOBA_HEADS * MOBA_HEAD_DIM
MLA_WIDTH = MLA_HEADS * MLA_V_DIM
MLA_PAD_WIDTH = MLA_HEADS * LANES
PROJ_TILE = 1024
PROJ_SUB_TILE = 512
ATTN_TILE = 512
MLP_TILE = 1024
MLP_SUB_TILE = 256
FF_CHUNK = 1024
VMEM_LIMIT = 56 * 1024 * 1024

F32 = jnp.float32
BF16 = jnp.bfloat16
_NT = (((1,), (1,)), ((), ()))


def _moba_pair_columns(base):
    cols = []
    for p in range(MOBA_HEADS // 2):
        for off in (0, MOBA_HEAD_DIM // 2):
            for h in (2 * p, 2 * p + 1):
                cols += [base + h * MOBA_HEAD_DIM + off + i for i in range(MOBA_HEAD_DIM // 2)]
    return cols


def _w_in_columns():
    s2 = MLA_Q_RANK + MLA_KV_RANK
    s3 = s2 + MLA_ROPE_DIM
    half = MLA_ROPE_DIM // 2
    shared = [-1] * LANES
    for i in range(half):
        shared[32 + i] = s2 + i
        shared[96 + i] = s2 + half + i
    row_major = list(range(0, s2)) + shared + _moba_pair_columns(s3 + MOBA_WIDTH)
    feature_major = _moba_pair_columns(s3) + list(range(s3 + 2 * MOBA_WIDTH, s3 + 3 * MOBA_WIDTH))
    return np.asarray(row_major, np.int32), np.asarray(feature_major, np.int32)


def _mla_q_columns():
    cols = []
    half = MLA_ROPE_DIM // 2
    for h in range(MLA_HEADS):
        b = h * (MLA_NOPE_DIM + MLA_ROPE_DIM)
        cols += [b + i for i in range(32)] + [b + MLA_NOPE_DIM + i for i in range(half)] + [-1] * 16
        cols += [b + 32 + i for i in range(32)] + [b + MLA_NOPE_DIM + half + i for i in range(half)] + [-1] * 16
    return np.asarray(cols, np.int32)


def _mla_k_columns():
    cols = []
    for h in range(MLA_HEADS):
        b = h * (MLA_NOPE_DIM + MLA_V_DIM)
        cols += [b + i for i in range(32)] + [-1] * 32 + [b + 32 + i for i in range(32)] + [-1] * 32
    return np.asarray(cols, np.int32)


def _mla_v_columns():
    cols = []
    for h in range(MLA_HEADS):
        b = h * (MLA_NOPE_DIM + MLA_V_DIM) + MLA_NOPE_DIM
        cols += [b + i for i in range(MLA_V_DIM)]
    return np.asarray(cols, np.int32)


def _gather_columns(w, cols):
    picked = jnp.take(w, jnp.asarray(np.maximum(cols, 0)), axis=1)
    return jnp.where(jnp.asarray(cols >= 0)[None, :], picked, 0.0).astype(BF16)


def _rope_tables(seq):
    pos = jnp.arange(seq, dtype=F32)

    def cs(half):
        inv_freq = 1.0 / (ROPE_THETA ** (jnp.arange(half, dtype=F32) / half))
        ang = pos[:, None] * inv_freq[None, :]
        return jnp.cos(ang), jnp.sin(ang)

    c, s = cs(MLA_ROPE_DIM // 2)
    one = lambda n: jnp.ones((seq, n), F32)
    zero = lambda n: jnp.zeros((seq, n), F32)
    cos_mla = jnp.concatenate([one(32), c, one(16), one(32), c, one(16)], axis=1)
    sin_mla = jnp.concatenate([zero(32), -s, zero(16), zero(32), s, zero(16)], axis=1)
    c, s = cs(MOBA_HEAD_DIM // 2)
    cos_moba = jnp.concatenate([c, c, c, c], axis=1)
    sin_moba = jnp.concatenate([-s, -s, s, s], axis=1)
    return cos_mla, sin_mla, cos_moba, sin_moba


def _rms(x, g):
    ms = jnp.mean(x * x, axis=-1, keepdims=True)
    return x * lax.rsqrt(ms + NORM_EPS) * g


def _rope_rows(x, cos, sin):
    return x * cos + pltpu.roll(x, HALF, 1) * sin


def _rope_cols(x, cos, sin):
    return x * cos + jnp.concatenate([x[HALF:], x[:HALF]], axis=0) * sin


def _proj_kernel(x_ref, g_ref, win_ref, wint_ref, gq_ref, wqt_ref, gkv_ref, wk_ref, wvt_ref,
                 cm_ref, sm_ref, co_ref, so_ref, cmt_ref, smt_ref, cot_ref, sot_ref,
                 mqt_ref, mk_ref, mvt_ref, oqt_ref, oqft_ref, ok_ref, ovt_ref, kmean_ref,
                 *, mla_scale, moba_scale):
    subs = [slice(r * PROJ_SUB_TILE, (r + 1) * PROJ_SUB_TILE)
            for r in range(x_ref.shape[0] // PROJ_SUB_TILE)]

    def normed(rows):
        return _rms(x_ref[rows, :], g_ref[...]).astype(BF16)

    u = normed(subs[0])
    for r, rows in enumerate(subs):
        u = _proj_rows(r, rows, u, (lambda nxt=subs[r + 1]: normed(nxt)) if r + 1 < len(subs) else None,
                       win_ref, wint_ref, gq_ref, wqt_ref, gkv_ref, wk_ref, wvt_ref,
                       cm_ref, sm_ref, co_ref, so_ref, cmt_ref, smt_ref, cot_ref, sot_ref,
                       mqt_ref, mk_ref, mvt_ref, oqt_ref, oqft_ref, ok_ref, ovt_ref, kmean_ref,
                       mla_scale, moba_scale)


def _proj_rows(r, rows, u, next_u, win_ref, wint_ref, gq_ref, wqt_ref, gkv_ref, wk_ref, wvt_ref,
               cm_ref, sm_ref, co_ref, so_ref, cmt_ref, smt_ref, cot_ref, sot_ref,
               mqt_ref, mk_ref, mvt_ref, oqt_ref, oqft_ref, ok_ref, ovt_ref, kmean_ref,
               mla_scale, moba_scale):
    def proj(lo, hi):
        return jnp.dot(u, win_ref[:, lo:hi], preferred_element_type=F32)

    def proj_t(lo, hi):
        return lax.dot_general(wint_ref[lo:hi, :], u, _NT, preferred_element_type=F32)

    c0 = MLA_Q_RANK
    c1 = c0 + MLA_KV_RANK
    c2 = c1 + LANES
    c3 = c2 + MOBA_WIDTH

    latent = proj(0, c2)
    u_following = next_u() if next_u is not None else None
    cq = _rms(latent[:, :c0], gq_ref[...]).astype(BF16)
    qt = lax.dot_general(wqt_ref[...], cq, _NT, preferred_element_type=F32)
    cmt, smt = cmt_ref[:, rows], smt_ref[:, rows]
    for h in range(MLA_HEADS):
        sl = slice(h * LANES, (h + 1) * LANES)
        mqt_ref[0, sl, rows] = (_rope_cols(qt[sl], cmt, smt) * mla_scale).astype(BF16)

    ckv = _rms(latent[:, c0:c1], gkv_ref[...]).astype(BF16)
    k_shared = _rope_rows(latent[:, c1:c2], cm_ref[rows, :], sm_ref[rows, :])
    k_nope = jnp.dot(ckv, wk_ref[...], preferred_element_type=F32)
    for h in range(MLA_HEADS):
        sl = slice(h * LANES, (h + 1) * LANES)
        mk_ref[rows, sl] = (k_nope[:, sl] + k_shared).astype(BF16)
    mvt_ref[0, :, rows] = lax.dot_general(wvt_ref[...], ckv, _NT,
                                          preferred_element_type=F32).astype(BF16)

    oqt = proj_t(0, MOBA_WIDTH)
    ok = proj(c2, c3)
    co, so = co_ref[rows, :], so_ref[rows, :]
    cot, sot = cot_ref[:, rows], sot_ref[:, rows]
    for p in range(MOBA_WIDTH // LANES):
        sl = slice(p * LANES, (p + 1) * LANES)
        q_rot = _rope_cols(oqt[sl], cot, sot) * moba_scale
        q_hi = q_rot.astype(BF16)
        oqt_ref[0, sl, rows] = q_hi
        oqft_ref[0, sl, rows] = (q_rot - q_hi.astype(F32)).astype(BF16)
        k_rot = _rope_rows(ok[:, sl], co, so)
        ok_ref[rows, sl] = k_rot.astype(BF16)
        for b in range(PROJ_SUB_TILE // MOBA_BLOCK):
            kmean_ref[r * (PROJ_SUB_TILE // MOBA_BLOCK) + b, :, sl] = jnp.mean(
                k_rot[b * MOBA_BLOCK:(b + 1) * MOBA_BLOCK], axis=0, keepdims=True)
    ovt_ref[0, :, rows] = proj_t(MOBA_WIDTH, 2 * MOBA_WIDTH).astype(BF16)
    return u_following


def _schedule(n_q):
    tiles = [q for q in range(n_q) for _ in range(q)]
    chunks = [c for q in range(n_q) for c in range(q)]
    return np.asarray(tiles + tiles[-1:], np.int32), np.asarray(chunks + chunks[-1:], np.int32)


def _heads_to_rows(out_t, n_heads, tq):
    return jnp.concatenate([out_t[:, h * tq:(h + 1) * tq] for h in range(n_heads)], axis=0).T


def _attend_row(tile_tab, chunk_tab, q_cat_of, k_ref, vt_ref, o_ref, s_bufs, st_ref, tri_ref,
                n_heads, bias_of=None):
    tq = tk = ATTN_TILE
    n_q = st_ref.shape[0]
    n_steps = n_q * (n_q - 1) // 2
    unroll = next(u for u in (14, 4, 2) if n_steps % u == 0)
    diag_unroll = 4 if n_q % 4 == 0 else 2
    width = n_heads * tq
    d_v = vt_ref.shape[1] // n_heads
    acc_rows = d_v + SUBLANES_BF16
    half = tq // 2
    ones = jnp.ones((SUBLANES_BF16, tk), BF16)

    def scores(tile, chunk, s_ref):
        start = pl.multiple_of(chunk * tk, tk)
        s = jnp.concatenate(
            [jnp.dot(k_ref[0, pl.ds(start, tk), lanes], q, preferred_element_type=F32)
             for lanes, q in q_cat_of(tile)], axis=1)
        if bias_of is not None:
            per_chunk = tk // MOBA_BLOCK
            s = jnp.concatenate(
                [s[j * MOBA_BLOCK:(j + 1) * MOBA_BLOCK] + bias_of(tile, per_chunk * chunk + j)
                 for j in range(per_chunk)], axis=0)
        s_ref[...] = s
        return jnp.max(s, axis=0, keepdims=True)

    def values(h, start, size):
        return jnp.concatenate([vt_ref[0, h * d_v:(h + 1) * d_v, pl.ds(start, size)], ones[:, :size]],
                               axis=0)

    def accumulate(chunk, state, s, s_max):
        m, acc = state
        m_new = jnp.maximum(m, s_max)
        alpha = jnp.exp2(m - m_new)
        p = jnp.exp2(s - m_new).astype(BF16)
        start = pl.multiple_of(chunk * tk, tk)
        pv = [jnp.dot(values(h, start, tk), p[:, h * tq:(h + 1) * tq], preferred_element_type=F32)
              for h in range(n_heads)]
        return m_new, alpha * acc + jnp.concatenate(pv, axis=1)

    def late(x):
        return jnp.concatenate([x[:, h * tq + half:(h + 1) * tq] for h in range(n_heads)], axis=1)

    def diagonal_scores(tile, s_ref):
        start = pl.multiple_of(tile * tk, tk)
        operands = q_cat_of(tile)
        s_early = jnp.concatenate(
            [jnp.dot(k_ref[0, pl.ds(start, half), lanes], q, preferred_element_type=F32)
             for lanes, q in operands], axis=1)
        s_late = jnp.concatenate(
            [jnp.dot(k_ref[0, pl.ds(pl.multiple_of(start + half, half), half), lanes],
                     jnp.concatenate([q[:, c + half:c + tq] for c in range(0, q.shape[1], tq)], axis=1),
                     preferred_element_type=F32)
             for lanes, q in operands], axis=1)
        if bias_of is not None:
            s_early = s_early + bias_of(tile, (tq // MOBA_BLOCK) * tile)
        tri = tri_ref[...]
        s_early = jnp.concatenate(
            [part for h in range(n_heads)
             for part in (s_early[:, h * tq:h * tq + half] + tri, s_early[:, h * tq + half:(h + 1) * tq])],
            axis=1)
        s_late = jnp.concatenate([s_late[:, h * half:(h + 1) * half] + tri for h in range(n_heads)],
                                 axis=1)
        s_ref[:half, :] = s_early
        s_ref[half:, :width // 2] = s_late
        max_early = jnp.max(s_early, axis=0, keepdims=True)
        max_late = jnp.max(s_late, axis=0, keepdims=True)
        return jnp.concatenate(
            [part for h in range(n_heads)
             for part in (max_early[:, h * tq:h * tq + half],
                          jnp.maximum(max_early[:, h * tq + half:(h + 1) * tq],
                                      max_late[:, h * half:(h + 1) * half]))], axis=1)

    def diagonal_accumulate(tile, state, s_ref, s_max):
        m, acc = state
        m_new = jnp.maximum(m, s_max)
        alpha = jnp.exp2(m - m_new)
        p_early = jnp.exp2(s_ref[:half, :] - m_new).astype(BF16)
        p_late = jnp.exp2(s_ref[half:, :width // 2] - late(m_new)).astype(BF16)
        start = pl.multiple_of(tile * tk, tk)
        pv = []
        for h in range(n_heads):
            out = jnp.dot(values(h, start, half), p_early[:, h * tq:(h + 1) * tq],
                          preferred_element_type=F32)
            out_late = jnp.dot(values(h, pl.multiple_of(start + half, half), half),
                               p_late[:, h * half:(h + 1) * half], preferred_element_type=F32)
            pv += [out[:, :half], out[:, half:] + out_late]
        return m_new, alpha * acc + jnp.concatenate(pv, axis=1)

    def save(tile, state):
        m, acc = state
        st_ref[tile, :acc_rows, :] = acc
        st_ref[tile, acc_rows:, :] = jnp.broadcast_to(m, (SUBLANES, width))

    def full_steps(i, carry):
        state, s_max = carry
        for j in range(unroll):
            t = i * unroll + j
            next_max = scores(tile_tab[t + 1], chunk_tab[t + 1], s_bufs[(j + 1) % 2])
            tile, chunk = tile_tab[t], chunk_tab[t]
            m, acc = state
            fresh = chunk == 0
            state = accumulate(chunk, (jnp.where(fresh, NEG_INF, m), jnp.where(fresh, 0.0, acc)),
                               s_bufs[j % 2][...], s_max)
            save(tile, state)
            s_max = next_max
        return state, s_max

    def diagonal_steps(i, s_max):
        for j in range(diag_unroll):
            tile = diag_unroll * i + j
            next_max = diagonal_scores(jnp.minimum(tile + 1, n_q - 1), s_bufs[(j + 1) % 2])
            state = (st_ref[tile, acc_rows:acc_rows + 1, :], st_ref[tile, :acc_rows, :])
            _, acc = diagonal_accumulate(tile, state, s_bufs[j % 2], s_max)
            out_t = acc[:d_v] * (1.0 / acc[d_v:d_v + 1])
            o_ref[0, pl.ds(pl.multiple_of(tile * tq, tq), tq), :] = (
                _heads_to_rows(out_t, n_heads, tq).astype(BF16))
            s_max = next_max
        return s_max

    empty = (jnp.full((1, width), NEG_INF, F32), jnp.zeros((acc_rows, width), F32))
    save(0, empty)
    lax.fori_loop(0, n_steps // unroll, full_steps,
                  (empty, scores(tile_tab[0], chunk_tab[0], s_bufs[0])))
    lax.fori_loop(0, n_q // diag_unroll, diagonal_steps, diagonal_scores(0, s_bufs[0]))


def _init_causal_bias(tri_ref):
    @pl.when((pl.program_id(0) == 0) & (pl.program_id(1) == 0))
    def _():
        key = lax.broadcasted_iota(jnp.int32, tri_ref.shape, 0)
        qry = lax.broadcasted_iota(jnp.int32, tri_ref.shape, 1)
        tri_ref[...] = jnp.where(key <= qry, 0.0, NEG_INF)


def _mla_attn_kernel(tile_tab, chunk_tab, qt_ref, k_ref, vt_ref, o_ref, s_a, s_b, st_ref, tri_ref):
    tq = ATTN_TILE
    _init_causal_bias(tri_ref)

    def q_cat_of(tile):
        q = qt_ref[0, :, pl.ds(pl.multiple_of(tile * tq, tq), tq)]
        return [(slice(h * LANES, (h + 1) * LANES), q[h * LANES:(h + 1) * LANES]) for h in range(2)]

    _attend_row(tile_tab, chunk_tab, q_cat_of, k_ref, vt_ref, o_ref, (s_a, s_b), st_ref, tri_ref, 2)


def _moba_gate_kernel(q_hi_ref, q_lo_ref, km_ref, bias_ref, *, n_blocks):
    tq = ATTN_TILE
    feat = lax.broadcasted_iota(jnp.int32, (LANES, tq), 0)
    col = lax.broadcasted_iota(jnp.int32, (1, 2 * tq), 1)
    km = km_ref[0]
    km_hi = km.astype(BF16)
    km_lo = (km - km_hi.astype(F32)).astype(BF16)

    def heads_side_by_side(q):
        return jnp.concatenate(
            [jnp.where((feat & (MOBA_HEAD_DIM // 2)) == h * (MOBA_HEAD_DIM // 2), q,
                       jnp.zeros((), BF16)) for h in range(2)], axis=1)

    for tile in range(q_hi_ref.shape[2] // tq):
        cur = (tile * tq + (col & (tq - 1))) // MOBA_BLOCK
        n_seen = (tile + 1) * tq // MOBA_BLOCK
        rows = min(n_blocks, -(-n_seen // SUBLANES) * SUBLANES)
        blk = lax.broadcasted_iota(jnp.int32, (rows, 2 * tq), 0)
        q_hi = heads_side_by_side(q_hi_ref[0, :, tile * tq:(tile + 1) * tq])
        q_lo = heads_side_by_side(q_lo_ref[0, :, tile * tq:(tile + 1) * tq])
        dot = lambda a, b: jnp.dot(a[:rows], b, preferred_element_type=F32)
        gate = (dot(km_lo, q_lo) + dot(km_lo, q_hi)) + dot(km_hi, q_lo) + dot(km_hi, q_hi)
        gate = jnp.where(blk < cur, gate, -jnp.inf)
        ahead_count = jnp.zeros((rows, 2 * tq), F32)
        for jp in range(n_seen - 1):
            g = gate[jp:jp + 1, :]
            ahead = jnp.where(jp < blk, jnp.where(g >= gate, 1.0, 0.0), jnp.where(g > gate, 1.0, 0.0))
            ahead_count = ahead_count + ahead
        keep = ((blk < cur) & (ahead_count < MOBA_TOPK)) | (blk == cur)
        base = tile * n_blocks
        bias_ref[0, 0, base:base + rows, :] = jnp.where(keep, 0.0, NEG_INF)
        if rows < n_blocks:
            bias_ref[0, 0, base + rows:base + n_blocks, :] = jnp.full((n_blocks - rows, 2 * tq),
                                                                      NEG_INF, F32)


def _moba_attn_kernel(tile_tab, chunk_tab, qt_ref, k_ref, vt_ref, bias_ref, o_ref,
                      s_a, s_b, st_ref, tri_ref, *, n_blocks):
    tq = ATTN_TILE
    _init_causal_bias(tri_ref)
    feat = lax.broadcasted_iota(jnp.int32, (LANES, tq), 0)

    def q_cat_of(tile):
        q = qt_ref[0, :, pl.ds(pl.multiple_of(tile * tq, tq), tq)]
        return [(slice(0, LANES), jnp.concatenate(
            [jnp.where((feat & (MOBA_HEAD_DIM // 2)) == h * (MOBA_HEAD_DIM // 2), q,
                       jnp.zeros((), BF16)) for h in range(2)], axis=1))]

    def bias_of(tile, block):
        return bias_ref[0, 0, pl.ds(tile * n_blocks + block, 1), :]

    _attend_row(tile_tab, chunk_tab, q_cat_of, k_ref, vt_ref, o_ref, (s_a, s_b), st_ref, tri_ref, 2,
                bias_of)


def _mlp_kernel(x_ref, a_ref, b_ref, wo_ref, g1_ref, g2_ref, wu_ref, wd_ref, g3_ref, o_ref,
                *, ff_chunk, sub_tile):
    n_a = a_ref.shape[1]
    subs = [slice(r * sub_tile, (r + 1) * sub_tile) for r in range(x_ref.shape[0] // sub_tile)]

    def attn_out(rows):
        return (jnp.dot(a_ref[rows, :], wo_ref[0:n_a, :], preferred_element_type=F32)
                + jnp.dot(b_ref[rows, :], wo_ref[n_a:, :], preferred_element_type=F32))

    def before(rows, y):
        h = x_ref[rows, :] + _rms(y, g1_ref[...])
        return h, _rms(h, g2_ref[...]).astype(BF16)

    def after(rows, h, acc):
        o_ref[rows, :] = h + _rms(acc, g3_ref[...])

    ys = [attn_out(rows) for rows in subs]
    ready = before(subs[0], ys[0])
    done = None
    for r, rows in enumerate(subs):
        h, u = ready
        acc = jnp.zeros(h.shape, F32)
        for c in range(wu_ref.shape[1] // ff_chunk):
            sl = slice(c * ff_chunk, (c + 1) * ff_chunk)
            a = jnp.maximum(jnp.dot(u, wu_ref[:, sl], preferred_element_type=F32), 0.0)
            acc = acc + jnp.dot((a * a).astype(BF16), wd_ref[sl, :], preferred_element_type=F32)
            if c == 0:
                if r + 1 < len(subs):
                    ready = before(subs[r + 1], ys[r + 1])
                if done is not None:
                    after(*done)
        done = (rows, h, acc)
    after(*done)


def _const_spec(shape):
    return pl.BlockSpec(shape, lambda *_: (0,) * len(shape), pipeline_mode=pl.Buffered(1))


def _layer(x, attn_pre_g, w_in, mla_q_norm_g, w_mla_q_up, mla_kv_norm_g, w_mla_kv_up,
           w_out, attn_post_g, mlp_pre_g, w_up, w_down, mlp_post_g):
    B, S, D = x.shape
    T = B * S
    n_blocks = S // MOBA_BLOCK
    n_q = S // ATTN_TILE
    assert S % ATTN_TILE == 0 and ATTN_TILE % MOBA_BLOCK == 0
    assert n_q % 2 == 0 and (n_q * (n_q - 1) // 2) % 2 == 0
    row = lambda g: g.reshape(1, -1).astype(F32)

    cols_rm, cols_fm = _w_in_columns()
    win = _gather_columns(w_in, cols_rm)
    wint = _gather_columns(w_in, cols_fm).T
    wqt = _gather_columns(w_mla_q_up, _mla_q_columns()).T
    wk = _gather_columns(w_mla_kv_up, _mla_k_columns())
    wvt = _gather_columns(w_mla_kv_up, _mla_v_columns()).T
    tables = _rope_tables(S)
    tables_t = tuple(t.T for t in tables)

    tm = PROJ_TILE
    n_st = S // tm
    tok = lambda w: pl.BlockSpec((tm, w), lambda i: (i, 0))
    tok_t = lambda w: pl.BlockSpec((1, w, tm), lambda i: (i // n_st, 0, i % n_st))
    pos = pl.BlockSpec((tm, LANES), lambda i: (i % n_st, 0))
    pos_t = pl.BlockSpec((LANES, tm), lambda i: (0, i % n_st))
    x2 = x.reshape(T, D)
    log2e = math.log2(math.e)
    mqt, mk, mvt, oqt, oqft, ok, ovt, kmean = pl.pallas_call(
        functools.partial(_proj_kernel,
                          mla_scale=float((MLA_NOPE_DIM + MLA_ROPE_DIM) ** -0.5 * log2e),
                          moba_scale=float(MOBA_HEAD_DIM ** -0.5 * log2e)),
        grid=(T // tm,),
        in_specs=[tok(D), _const_spec((1, D)), _const_spec(win.shape), _const_spec(wint.shape),
                  _const_spec((1, MLA_Q_RANK)), _const_spec(wqt.shape),
                  _const_spec((1, MLA_KV_RANK)), _const_spec(wk.shape), _const_spec(wvt.shape),
                  pos, pos, pos, pos, pos_t, pos_t, pos_t, pos_t],
        out_specs=[tok_t(MLA_PAD_WIDTH), tok(MLA_PAD_WIDTH), tok_t(MLA_WIDTH),
                   tok_t(MOBA_WIDTH), tok_t(MOBA_WIDTH), tok(MOBA_WIDTH), tok_t(MOBA_WIDTH),
                   pl.BlockSpec((tm // MOBA_BLOCK, 1, MOBA_WIDTH), lambda i: (i, 0, 0))],
        out_shape=[jax.ShapeDtypeStruct((B, MLA_PAD_WIDTH, S), BF16),
                   jax.ShapeDtypeStruct((T, MLA_PAD_WIDTH), BF16),
                   jax.ShapeDtypeStruct((B, MLA_WIDTH, S), BF16),
                   jax.ShapeDtypeStruct((B, MOBA_WIDTH, S), BF16),
                   jax.ShapeDtypeStruct((B, MOBA_WIDTH, S), BF16),
                   jax.ShapeDtypeStruct((T, MOBA_WIDTH), BF16),
                   jax.ShapeDtypeStruct((B, MOBA_WIDTH, S), BF16),
                   jax.ShapeDtypeStruct((T // MOBA_BLOCK, 1, MOBA_WIDTH), F32)],
        compiler_params=pltpu.CompilerParams(vmem_limit_bytes=VMEM_LIMIT),
        name="token_projection",
    )(x2, row(attn_pre_g), win, wint, row(mla_q_norm_g), wqt, row(mla_kv_norm_g), wk, wvt,
      *tables, *tables_t)

    tq = ATTN_TILE
    width = 2 * tq
    tile_tab, chunk_tab = (jnp.asarray(t) for t in _schedule(n_q))
    seq_t = lambda w: pl.BlockSpec((1, w, S), lambda b, p, *_: (b, p, 0))
    seq = lambda w: pl.BlockSpec((1, S, w), lambda b, p, *_: (b, 0, p))
    attn_params = pltpu.CompilerParams(vmem_limit_bytes=VMEM_LIMIT)
    attn_scratch = [pltpu.VMEM((tq, width), F32), pltpu.VMEM((tq, width), F32),
                    pltpu.VMEM((n_q, MLA_V_DIM + SUBLANES_BF16 + SUBLANES, width), F32),
                    pltpu.VMEM((tq // 2, tq // 2), F32)]

    mla_o = pl.pallas_call(
        _mla_attn_kernel,
        grid_spec=pltpu.PrefetchScalarGridSpec(
            num_scalar_prefetch=2, grid=(B, MLA_HEADS // 2),
            in_specs=[seq_t(2 * LANES), seq(2 * LANES), seq_t(LANES)],
            out_specs=seq(LANES), scratch_shapes=attn_scratch),
        out_shape=jax.ShapeDtypeStruct((B, S, MLA_WIDTH), BF16),
        compiler_params=attn_params,
        name="mla_attention",
    )(tile_tab, chunk_tab, mqt, mk.reshape(B, S, MLA_PAD_WIDTH), mvt)

    moba_bias = pl.pallas_call(
        functools.partial(_moba_gate_kernel, n_blocks=n_blocks),
        grid=(B, MOBA_HEADS // 2),
        in_specs=[pl.BlockSpec((1, LANES, S), lambda b, p: (b, p, 0)),
                  pl.BlockSpec((1, LANES, S), lambda b, p: (b, p, 0)),
                  pl.BlockSpec((1, n_blocks, LANES), lambda b, p: (b, 0, p))],
        out_specs=pl.BlockSpec((1, 1, n_q * n_blocks, width), lambda b, p: (b, p, 0, 0)),
        out_shape=jax.ShapeDtypeStruct((B, MOBA_HEADS // 2, n_q * n_blocks, width), F32),
        name="moba_gate",
    )(oqt, oqft, kmean.reshape(B, n_blocks, MOBA_WIDTH))

    moba_o = pl.pallas_call(
        functools.partial(_moba_attn_kernel, n_blocks=n_blocks),
        grid_spec=pltpu.PrefetchScalarGridSpec(
            num_scalar_prefetch=2, grid=(B, MOBA_HEADS // 2),
            in_specs=[seq_t(LANES), seq(LANES), seq_t(LANES),
                      pl.BlockSpec((1, 1, n_q * n_blocks, width), lambda b, p, *_: (b, p, 0, 0))],
            out_specs=seq(LANES), scratch_shapes=attn_scratch),
        out_shape=jax.ShapeDtypeStruct((B, S, MOBA_WIDTH), BF16),
        compiler_params=attn_params,
        name="moba_attention",
    )(tile_tab, chunk_tab, oqt, ok.reshape(B, S, MOBA_WIDTH), ovt, moba_bias)

    tm = MLP_TILE
    d_ff = w_up.shape[1]
    tok = lambda w: pl.BlockSpec((tm, w), lambda i: (i, 0))
    out = pl.pallas_call(
        functools.partial(_mlp_kernel, ff_chunk=FF_CHUNK, sub_tile=MLP_SUB_TILE),
        grid=(T // tm,),
        in_specs=[tok(D), tok(MLA_WIDTH), tok(MOBA_WIDTH),
                  _const_spec((MLA_WIDTH + MOBA_WIDTH, D)), _const_spec((1, D)), _const_spec((1, D)),
                  _const_spec((D, d_ff)), _const_spec((d_ff, D)), _const_spec((1, D))],
        out_specs=tok(D),
        out_shape=jax.ShapeDtypeStruct((T, D), F32),
        compiler_params=pltpu.CompilerParams(vmem_limit_bytes=VMEM_LIMIT),
        name="out_proj_mlp",
    )(x2, mla_o.reshape(T, MLA_WIDTH), moba_o.reshape(T, MOBA_WIDTH),
      w_out.astype(BF16), row(attn_post_g), row(mlp_pre_g),
      w_up.astype(BF16), w_down.astype(BF16), row(mlp_post_g))
    return out.reshape(B, S, D)


def kernel(x, attn_pre_g, w_in, mla_q_norm_g, w_mla_q_up, mla_kv_norm_g, w_mla_kv_up, w_out,
           attn_post_g, mlp_pre_g, w_up, w_down, mlp_post_g):
    h = x
    for l in range(w_in.shape[0]):
        h = _layer(h, attn_pre_g[l], w_in[l], mla_q_norm_g[l], w_mla_q_up[l], mla_kv_norm_g[l],
                   w_mla_kv_up[l], w_out[l], attn_post_g[l], mlp_pre_g[l], w_up[l], w_down[l],
                   mlp_post_g[l])
    return h
```

```python
import functools
import math

import numpy as np
import jax
import jax.numpy as jnp
from jax import lax
from jax.experimental import pallas as pl
from jax.experimental.pallas import tpu as pltpu

MLA_HEADS = 8
MLA_NOPE_DIM = 64
MLA_ROPE_DIM = 32
MLA_V_DIM = 64
MLA_Q_RANK = 256
MLA_KV_RANK = 128
MOBA_HEADS = 8
MOBA_HEAD_DIM = 64
MOBA_BLOCK = 256
MOBA_TOPK = 3
ROPE_THETA = 10000.0
NORM_EPS = 1e-6
NEG_INF = -1e30

LANES = 128
SUBLANES = 8
SUBLANES_BF16 = 16
HALF = LANES // 2
MOBA_WIDTH = MOBA_HEADS * MOBA_HEAD_DIM
MLA_WIDTH = MLA_HEADS * MLA_V_DIM
MLA_PAD_WIDTH = MLA_HEADS * LANES
PROJ_TILE = 1024
PROJ_SUB_TILE = 512
ATTN_TILE = 512
MLP_TILE = 1024
MLP_SUB_TILE = 256
FF_CHUNK = 1024
VMEM_LIMIT = 56 * 1024 * 1024

F32 = jnp.float32
BF16 = jnp.bfloat16
_NT = (((1,), (1,)), ((), ()))


def _moba_pair_columns(base):
    cols = []
    for p in range(MOBA_HEADS // 2):
        for off in (0, MOBA_HEAD_DIM // 2):
            for h in (2 * p, 2 * p + 1):
                cols += [base + h * MOBA_HEAD_DIM + off + i for i in range(MOBA_HEAD_DIM // 2)]
    return cols


def _w_in_columns():
    s2 = MLA_Q_RANK + MLA_KV_RANK
    s3 = s2 + MLA_ROPE_DIM
    half = MLA_ROPE_DIM // 2
    shared = [-1] * LANES
    for i in range(half):
        shared[32 + i] = s2 + i
        shared[96 + i] = s2 + half + i
    row_major = list(range(0, s2)) + shared + _moba_pair_columns(s3 + MOBA_WIDTH)
    feature_major = _moba_pair_columns(s3) + list(range(s3 + 2 * MOBA_WIDTH, s3 + 3 * MOBA_WIDTH))
    return np.asarray(row_major, np.int32), np.asarray(feature_major, np.int32)


def _mla_q_columns():
    cols = []
    half = MLA_ROPE_DIM // 2
    for h in range(MLA_HEADS):
        b = h * (MLA_NOPE_DIM + MLA_ROPE_DIM)
        cols += [b + i for i in range(32)] + [b + MLA_NOPE_DIM + i for i in range(half)] + [-1] * 16
        cols += [b + 32 + i for i in range(32)] + [b + MLA_NOPE_DIM + half + i for i in range(half)] + [-1] * 16
    return np.asarray(cols, np.int32)


def _mla_k_columns():
    cols = []
    for h in range(MLA_HEADS):
        b = h * (MLA_NOPE_DIM + MLA_V_DIM)
        cols += [b + i for i in range(32)] + [-1] * 32 + [b + 32 + i for i in range(32)] + [-1] * 32
    return np.asarray(cols, np.int32)


def _mla_v_columns():
    cols = []
    for h in range(MLA_HEADS):
        b = h * (MLA_NOPE_DIM + MLA_V_DIM) + MLA_NOPE_DIM
        cols += [b + i for i in range(MLA_V_DIM)]
    return np.asarray(cols, np.int32)


def _gather_columns(w, cols):
    picked = jnp.take(w, jnp.asarray(np.maximum(cols, 0)), axis=1)
    return jnp.where(jnp.asarray(cols >= 0)[None, :], picked, 0.0).astype(BF16)


def _rope_tables(seq):
    pos = jnp.arange(seq, dtype=F32)

    def cs(half):
        inv_freq = 1.0 / (ROPE_THETA ** (jnp.arange(half, dtype=F32) / half))
        ang = pos[:, None] * inv_freq[None, :]
        return jnp.cos(ang), jnp.sin(ang)

    c, s = cs(MLA_ROPE_DIM // 2)
    one = lambda n: jnp.ones((seq, n), F32)
    zero = lambda n: jnp.zeros((seq, n), F32)
    cos_mla = jnp.concatenate([one(32), c, one(16), one(32), c, one(16)], axis=1)
    sin_mla = jnp.concatenate([zero(32), -s, zero(16), zero(32), s, zero(16)], axis=1)
    c, s = cs(MOBA_HEAD_DIM // 2)
    cos_moba = jnp.concatenate([c, c, c, c], axis=1)
    sin_moba = jnp.concatenate([-s, -s, s, s], axis=1)
    return cos_mla, sin_mla, cos_moba, sin_moba


def _rms(x, g):
    ms = jnp.mean(x * x, axis=-1, keepdims=True)
    return x * lax.rsqrt(ms + NORM_EPS) * g


def _rope_rows(x, cos, sin):
    return x * cos + pltpu.roll(x, HALF, 1) * sin


def _rope_cols(x, cos, sin):
    return x * cos + jnp.concatenate([x[HALF:], x[:HALF]], axis=0) * sin


def _proj_kernel(x_ref, g_ref, win_ref, wint_ref, gq_ref, wqt_ref, gkv_ref, wk_ref, wvt_ref,
                 cm_ref, sm_ref, co_ref, so_ref, cmt_ref, smt_ref, cot_ref, sot_ref,
                 mqt_ref, mk_ref, mvt_ref, oqt_ref, oqft_ref, ok_ref, ovt_ref, kmean_ref,
                 *, mla_scale, moba_scale):
    subs = [slice(r * PROJ_SUB_TILE, (r + 1) * PROJ_SUB_TILE)
            for r in range(x_ref.shape[0] // PROJ_SUB_TILE)]

    def normed(rows):
        return _rms(x_ref[rows, :], g_ref[...]).astype(BF16)

    u = normed(subs[0])
    for r, rows in enumerate(subs):
        u = _proj_rows(r, rows, u, (lambda nxt=subs[r + 1]: normed(nxt)) if r + 1 < len(subs) else None,
                       win_ref, wint_ref, gq_ref, wqt_ref, gkv_ref, wk_ref, wvt_ref,
                       cm_ref, sm_ref, co_ref, so_ref, cmt_ref, smt_ref, cot_ref, sot_ref,
                       mqt_ref, mk_ref, mvt_ref, oqt_ref, oqft_ref, ok_ref, ovt_ref, kmean_ref,
                       mla_scale, moba_scale)


def _proj_rows(r, rows, u, next_u, win_ref, wint_ref, gq_ref, wqt_ref, gkv_ref, wk_ref, wvt_ref,
               cm_ref, sm_ref, co_ref, so_ref, cmt_ref, smt_ref, cot_ref, sot_ref,
               mqt_ref, mk_ref, mvt_ref, oqt_ref, oqft_ref, ok_ref, ovt_ref, kmean_ref,
               mla_scale, moba_scale):
    def proj(lo, hi):
        return jnp.dot(u, win_ref[:, lo:hi], preferred_element_type=F32)

    def proj_t(lo, hi):
        return lax.dot_general(wint_ref[lo:hi, :], u, _NT, preferred_element_type=F32)

    c0 = MLA_Q_RANK
    c1 = c0 + MLA_KV_RANK
    c2 = c1 + LANES
    c3 = c2 + MOBA_WIDTH

    latent = proj(0, c2)
    u_following = next_u() if next_u is not None else None
    cq = _rms(latent[:, :c0], gq_ref[...]).astype(BF16)
    qt = lax.dot_general(wqt_ref[...], cq, _NT, preferred_element_type=F32)
    cmt, smt = cmt_ref[:, rows], smt_ref[:, rows]
    for h in range(MLA_HEADS):
        sl = slice(h * LANES, (h + 1) * LANES)
        mqt_ref[0, sl, rows] = (_rope_cols(qt[sl], cmt, smt) * mla_scale).astype(BF16)

    ckv = _rms(latent[:, c0:c1], gkv_ref[...]).astype(BF16)
    k_shared = _rope_rows(latent[:, c1:c2], cm_ref[rows, :], sm_ref[rows, :])
    k_nope = jnp.dot(ckv, wk_ref[...], preferred_element_type=F32)
    for h in range(MLA_HEADS):
        sl = slice(h * LANES, (h + 1) * LANES)
        mk_ref[rows, sl] = (k_nope[:, sl] + k_shared).astype(BF16)
    mvt_ref[0, :, rows] = lax.dot_general(wvt_ref[...], ckv, _NT,
                                          preferred_element_type=F32).astype(BF16)

    oqt = proj_t(0, MOBA_WIDTH)
    ok = proj(c2, c3)
    co, so = co_ref[rows, :], so_ref[rows, :]
    cot, sot = cot_ref[:, rows], sot_ref[:, rows]
    for p in range(MOBA_WIDTH // LANES):
        sl = slice(p * LANES, (p + 1) * LANES)
        q_rot = _rope_cols(oqt[sl], cot, sot) * moba_scale
        q_hi = q_rot.astype(BF16)
        oqt_ref[0, sl, rows] = q_hi
        oqft_ref[0, sl, rows] = (q_rot - q_hi.astype(F32)).astype(BF16)
        k_rot = _rope_rows(ok[:, sl], co, so)
        ok_ref[rows, sl] = k_rot.astype(BF16)
        for b in range(PROJ_SUB_TILE // MOBA_BLOCK):
            kmean_ref[r * (PROJ_SUB_TILE // MOBA_BLOCK) + b, :, sl] = jnp.mean(
                k_rot[b * MOBA_BLOCK:(b + 1) * MOBA_BLOCK], axis=0, keepdims=True)
    ovt_ref[0, :, rows] = proj_t(MOBA_WIDTH, 2 * MOBA_WIDTH).astype(BF16)
    return u_following


def _schedule(n_q):
    tiles = [q for q in range(n_q) for _ in range(q)]
    chunks = [c for q in range(n_q) for c in range(q)]
    return np.asarray(tiles + tiles[-1:], np.int32), np.asarray(chunks + chunks[-1:], np.int32)


def _heads_to_rows(out_t, n_heads, tq):
    return jnp.concatenate([out_t[:, h * tq:(h + 1) * tq] for h in range(n_heads)], axis=0).T


def _attend_row(tile_tab, chunk_tab, q_cat_of, k_ref, vt_ref, o_ref, s_bufs, st_ref, tri_ref,
                n_heads, bias_of=None):
    tq = tk = ATTN_TILE
    n_q = st_ref.shape[0]
    n_steps = n_q * (n_q - 1) // 2
    unroll = next(u for u in (14, 4, 2) if n_steps % u == 0)
    diag_unroll = 4 if n_q % 4 == 0 else 2
    width = n_heads * tq
    d_v = vt_ref.shape[1] // n_heads
    acc_rows = d_v + SUBLANES_BF16
    half = tq // 2
    per_chunk = tk // MOBA_BLOCK
    assert bias_of is None or half == MOBA_BLOCK
    ones = jnp.ones((SUBLANES_BF16, tk), BF16)

    def scores(tile, chunk, s_ref):
        start = pl.multiple_of(chunk * tk, tk)
        s = jnp.concatenate(
            [jnp.dot(k_ref[0, pl.ds(start, tk), lanes], q, preferred_element_type=F32)
             for lanes, q in q_cat_of(tile)], axis=1)
        s_ref[...] = s
        if bias_of is None:
            return jnp.max(s, axis=0, keepdims=True)
        return functools.reduce(jnp.maximum, [
            jnp.max(s[j * MOBA_BLOCK:(j + 1) * MOBA_BLOCK], axis=0, keepdims=True)
            + bias_of(tile, per_chunk * chunk + j) for j in range(per_chunk)])

    def exponent_base(m_new, bias):
        return jnp.where(m_new < 0.1 * NEG_INF, -NEG_INF, m_new - bias)

    def values(h, start, size):
        return jnp.concatenate([vt_ref[0, h * d_v:(h + 1) * d_v, pl.ds(start, size)], ones[:, :size]],
                               axis=0)

    def accumulate(tile, chunk, state, s, s_max):
        m, acc = state
        m_new = jnp.maximum(m, s_max)
        alpha = jnp.exp2(m - m_new)
        if bias_of is None:
            p = jnp.exp2(s - m_new).astype(BF16)
        else:
            p = jnp.concatenate(
                [jnp.exp2(s[j * MOBA_BLOCK:(j + 1) * MOBA_BLOCK]
                          - exponent_base(m_new, bias_of(tile, per_chunk * chunk + j)))
                 for j in range(per_chunk)], axis=0).astype(BF16)
        start = pl.multiple_of(chunk * tk, tk)
        pv = [jnp.dot(values(h, start, tk), p[:, h * tq:(h + 1) * tq], preferred_element_type=F32)
              for h in range(n_heads)]
        return m_new, alpha * acc + jnp.concatenate(pv, axis=1)

    def late(x):
        return jnp.concatenate([x[:, h * tq + half:(h + 1) * tq] for h in range(n_heads)], axis=1)

    def diagonal_scores(tile, s_ref):
        start = pl.multiple_of(tile * tk, tk)
        operands = q_cat_of(tile)
        s_early = jnp.concatenate(
            [jnp.dot(k_ref[0, pl.ds(start, half), lanes], q, preferred_element_type=F32)
             for lanes, q in operands], axis=1)
        s_late = jnp.concatenate(
            [jnp.dot(k_ref[0, pl.ds(pl.multiple_of(start + half, half), half), lanes],
                     jnp.concatenate([q[:, c + half:c + tq] for c in range(0, q.shape[1], tq)], axis=1),
                     preferred_element_type=F32)
             for lanes, q in operands], axis=1)
        tri = tri_ref[...]
        s_early = jnp.concatenate(
            [part for h in range(n_heads)
             for part in (s_early[:, h * tq:h * tq + half] + tri, s_early[:, h * tq + half:(h + 1) * tq])],
            axis=1)
        s_late = jnp.concatenate([s_late[:, h * half:(h + 1) * half] + tri for h in range(n_heads)],
                                 axis=1)
        s_ref[:half, :] = s_early
        s_ref[half:, :width // 2] = s_late
        max_early = jnp.max(s_early, axis=0, keepdims=True)
        if bias_of is not None:
            max_early = max_early + bias_of(tile, per_chunk * tile)
        max_late = jnp.max(s_late, axis=0, keepdims=True)
        return jnp.concatenate(
            [part for h in range(n_heads)
             for part in (max_early[:, h * tq:h * tq + half],
                          jnp.maximum(max_early[:, h * tq + half:(h + 1) * tq],
                                      max_late[:, h * half:(h + 1) * half]))], axis=1)

    def diagonal_accumulate(tile, state, s_ref, s_max):
        m, acc = state
        m_new = jnp.maximum(m, s_max)
        alpha = jnp.exp2(m - m_new)
        base = m_new if bias_of is None else exponent_base(m_new, bias_of(tile, per_chunk * tile))
        p_early = jnp.exp2(s_ref[:half, :] - base).astype(BF16)
        p_late = jnp.exp2(s_ref[half:, :width // 2] - late(m_new)).astype(BF16)
        start = pl.multiple_of(tile * tk, tk)
        pv = []
        for h in range(n_heads):
            out = jnp.dot(values(h, start, half), p_early[:, h * tq:(h + 1) * tq],
                          preferred_element_type=F32)
            out_late = jnp.dot(values(h, pl.multiple_of(start + half, half), half),
                               p_late[:, h * half:(h + 1) * half], preferred_element_type=F32)
            pv += [out[:, :half], out[:, half:] + out_late]
        return m_new, alpha * acc + jnp.concatenate(pv, axis=1)

    def save(tile, state):
        m, acc = state
        st_ref[tile, :acc_rows, :] = acc
        st_ref[tile, acc_rows:, :] = jnp.broadcast_to(m, (SUBLANES, width))

    def full_steps(i, carry):
        state, s_max = carry
        for j in range(unroll):
            t = i * unroll + j
            next_max = scores(tile_tab[t + 1], chunk_tab[t + 1], s_bufs[(j + 1) % 2])
            tile, chunk = tile_tab[t], chunk_tab[t]
            m, acc = state
            fresh = chunk == 0
            state = accumulate(tile, chunk, (jnp.where(fresh, NEG_INF, m), jnp.where(fresh, 0.0, acc)),
                               s_bufs[j % 2][...], s_max)
            save(tile, state)
            s_max = next_max
        return state, s_max

    def diagonal_steps(i, s_max):
        for j in range(diag_unroll):
            tile = diag_unroll * i + j
            next_max = diagonal_scores(jnp.minimum(tile + 1, n_q - 1), s_bufs[(j + 1) % 2])
            state = (st_ref[tile, acc_rows:acc_rows + 1, :], st_ref[tile, :acc_rows, :])
            _, acc = diagonal_accumulate(tile, state, s_bufs[j % 2], s_max)
            out_t = acc[:d_v] * (1.0 / acc[d_v:d_v + 1])
            o_ref[0, pl.ds(pl.multiple_of(tile * tq, tq), tq), :] = (
                _heads_to_rows(out_t, n_heads, tq).astype(BF16))
            s_max = next_max
        return s_max

    empty = (jnp.full((1, width), NEG_INF, F32), jnp.zeros((acc_rows, width), F32))
    save(0, empty)
    lax.fori_loop(0, n_steps // unroll, full_steps,
                  (empty, scores(tile_tab[0], chunk_tab[0], s_bufs[0])))
    lax.fori_loop(0, n_q // diag_unroll, diagonal_steps, diagonal_scores(0, s_bufs[0]))


def _init_causal_bias(tri_ref):
    @pl.when((pl.program_id(0) == 0) & (pl.program_id(1) == 0))
    def _():
        key = lax.broadcasted_iota(jnp.int32, tri_ref.shape, 0)
        qry = lax.broadcasted_iota(jnp.int32, tri_ref.shape, 1)
        tri_ref[...] = jnp.where(key <= qry, 0.0, NEG_INF)


def _mla_attn_kernel(tile_tab, chunk_tab, qt_ref, k_ref, vt_ref, o_ref, s_a, s_b, st_ref, tri_ref):
    tq = ATTN_TILE
    _init_causal_bias(tri_ref)

    def q_cat_of(tile):
        q = qt_ref[0, :, pl.ds(pl.multiple_of(tile * tq, tq), tq)]
        return [(slice(h * LANES, (h + 1) * LANES), q[h * LANES:(h + 1) * LANES]) for h in range(2)]

    _attend_row(tile_tab, chunk_tab, q_cat_of, k_ref, vt_ref, o_ref, (s_a, s_b), st_ref, tri_ref, 2)


def _moba_gate_kernel(q_hi_ref, q_lo_ref, km_ref, bias_ref, *, n_blocks):
    tq = ATTN_TILE
    feat = lax.broadcasted_iota(jnp.int32, (LANES, tq), 0)
    col = lax.broadcasted_iota(jnp.int32, (1, 2 * tq), 1)
    km = km_ref[0]
    km_hi = km.astype(BF16)
    km_lo = (km - km_hi.astype(F32)).astype(BF16)

    def heads_side_by_side(q):
        return jnp.concatenate(
            [jnp.where((feat & (MOBA_HEAD_DIM // 2)) == h * (MOBA_HEAD_DIM // 2), q,
                       jnp.zeros((), BF16)) for h in range(2)], axis=1)

    for tile in range(q_hi_ref.shape[2] // tq):
        cur = (tile * tq + (col & (tq - 1))) // MOBA_BLOCK
        n_seen = (tile + 1) * tq // MOBA_BLOCK
        rows = min(n_blocks, -(-n_seen // SUBLANES) * SUBLANES)
        blk = lax.broadcasted_iota(jnp.int32, (rows, 2 * tq), 0)
        q_hi = heads_side_by_side(q_hi_ref[0, :, tile * tq:(tile + 1) * tq])
        q_lo = heads_side_by_side(q_lo_ref[0, :, tile * tq:(tile + 1) * tq])
        dot = lambda a, b: jnp.dot(a[:rows], b, preferred_element_type=F32)
        gate = (dot(km_lo, q_lo) + dot(km_lo, q_hi)) + dot(km_hi, q_lo) + dot(km_hi, q_hi)
        gate = jnp.where(blk < cur, gate, -jnp.inf)
        ahead_count = jnp.zeros((rows, 2 * tq), F32)
        for jp in range(n_seen - 1):
            g = gate[jp:jp + 1, :]
            ahead = jnp.where(jp < blk, jnp.where(g >= gate, 1.0, 0.0), jnp.where(g > gate, 1.0, 0.0))
            ahead_count = ahead_count + ahead
        keep = ((blk < cur) & (ahead_count < MOBA_TOPK)) | (blk == cur)
        base = tile * n_blocks
        bias_ref[0, 0, base:base + rows, :] = jnp.where(keep, 0.0, NEG_INF)
        if rows < n_blocks:
            bias_ref[0, 0, base + rows:base + n_blocks, :] = jnp.full((n_blocks - rows, 2 * tq),
                                                                      NEG_INF, F32)


def _moba_attn_kernel(tile_tab, chunk_tab, qt_ref, k_ref, vt_ref, bias_ref, o_ref,
                      s_a, s_b, st_ref, tri_ref, *, n_blocks):
    tq = ATTN_TILE
    _init_causal_bias(tri_ref)
    feat = lax.broadcasted_iota(jnp.int32, (LANES, tq), 0)

    def q_cat_of(tile):
        q = qt_ref[0, :, pl.ds(pl.multiple_of(tile * tq, tq), tq)]
        return [(slice(0, LANES), jnp.concatenate(
            [jnp.where((feat & (MOBA_HEAD_DIM // 2)) == h * (MOBA_HEAD_DIM // 2), q,
                       jnp.zeros((), BF16)) for h in range(2)], axis=1))]

    def bias_of(tile, block):
        return bias_ref[0, 0, pl.ds(tile * n_blocks + block, 1), :]

    _attend_row(tile_tab, chunk_tab, q_cat_of, k_ref, vt_ref, o_ref, (s_a, s_b), st_ref, tri_ref, 2,
                bias_of)


def _mlp_kernel(x_ref, a_ref, b_ref, wo_ref, g1_ref, g2_ref, wu_ref, wd_ref, g3_ref, o_ref,
                *, ff_chunk, sub_tile):
    n_a = a_ref.shape[1]
    subs = [slice(r * sub_tile, (r + 1) * sub_tile) for r in range(x_ref.shape[0] // sub_tile)]

    def attn_out(rows):
        return (jnp.dot(a_ref[rows, :], wo_ref[0:n_a, :], preferred_element_type=F32)
                + jnp.dot(b_ref[rows, :], wo_ref[n_a:, :], preferred_element_type=F32))

    def before(rows, y):
        h = x_ref[rows, :] + _rms(y, g1_ref[...])
        return h, _rms(h, g2_ref[...]).astype(BF16)

    def after(rows, h, acc):
        o_ref[rows, :] = h + _rms(acc, g3_ref[...])

    ys = [attn_out(rows) for rows in subs]
    ready = before(subs[0], ys[0])
    done = None
    for r, rows in enumerate(subs):
        h, u = ready
        acc = jnp.zeros(h.shape, F32)
        for c in range(wu_ref.shape[1] // ff_chunk):
            sl = slice(c * ff_chunk, (c + 1) * ff_chunk)
            a = jnp.maximum(jnp.dot(u, wu_ref[:, sl], preferred_element_type=F32), 0.0)
            acc = acc + jnp.dot((a * a).astype(BF16), wd_ref[sl, :], preferred_element_type=F32)
            if c == 0:
                if r + 1 < len(subs):
                    ready = before(subs[r + 1], ys[r + 1])
                if done is not None:
                    after(*done)
        done = (rows, h, acc)
    after(*done)


def _const_spec(shape):
    return pl.BlockSpec(shape, lambda *_: (0,) * len(shape), pipeline_mode=pl.Buffered(1))


def _layer(x, attn_pre_g, w_in, mla_q_norm_g, w_mla_q_up, mla_kv_norm_g, w_mla_kv_up,
           w_out, attn_post_g, mlp_pre_g, w_up, w_down, mlp_post_g):
    B, S, D = x.shape
    T = B * S
    n_blocks = S // MOBA_BLOCK
    n_q = S // ATTN_TILE
    assert S % ATTN_TILE == 0 and ATTN_TILE % MOBA_BLOCK == 0
    assert n_q % 2 == 0 and (n_q * (n_q - 1) // 2) % 2 == 0
    row = lambda g: g.reshape(1, -1).astype(F32)

    cols_rm, cols_fm = _w_in_columns()
    win = _gather_columns(w_in, cols_rm)
    wint = _gather_columns(w_in, cols_fm).T
    wqt = _gather_columns(w_mla_q_up, _mla_q_columns()).T
    wk = _gather_columns(w_mla_kv_up, _mla_k_columns())
    wvt = _gather_columns(w_mla_kv_up, _mla_v_columns()).T
    tables = _rope_tables(S)
    tables_t = tuple(t.T for t in tables)

    tm = PROJ_TILE
    n_st = S // tm
    tok = lambda w: pl.BlockSpec((tm, w), lambda i: (i, 0))
    tok_t = lambda w: pl.BlockSpec((1, w, tm), lambda i: (i // n_st, 0, i % n_st))
    pos = pl.BlockSpec((tm, LANES), lambda i: (i % n_st, 0))
    pos_t = pl.BlockSpec((LANES, tm), lambda i: (0, i % n_st))
    x2 = x.reshape(T, D)
    log2e = math.log2(math.e)
    mqt, mk, mvt, oqt, oqft, ok, ovt, kmean = pl.pallas_call(
        functools.partial(_proj_kernel,
                          mla_scale=float((MLA_NOPE_DIM + MLA_ROPE_DIM) ** -0.5 * log2e),
                          moba_scale=float(MOBA_HEAD_DIM ** -0.5 * log2e)),
        grid=(T // tm,),
        in_specs=[tok(D), _const_spec((1, D)), _const_spec(win.shape), _const_spec(wint.shape),
                  _const_spec((1, MLA_Q_RANK)), _const_spec(wqt.shape),
                  _const_spec((1, MLA_KV_RANK)), _const_spec(wk.shape), _const_spec(wvt.shape),
                  pos, pos, pos, pos, pos_t, pos_t, pos_t, pos_t],
        out_specs=[tok_t(MLA_PAD_WIDTH), tok(MLA_PAD_WIDTH), tok_t(MLA_WIDTH),
                   tok_t(MOBA_WIDTH), tok_t(MOBA_WIDTH), tok(MOBA_WIDTH), tok_t(MOBA_WIDTH),
                   pl.BlockSpec((tm // MOBA_BLOCK, 1, MOBA_WIDTH), lambda i: (i, 0, 0))],
        out_shape=[jax.ShapeDtypeStruct((B, MLA_PAD_WIDTH, S), BF16),
                   jax.ShapeDtypeStruct((T, MLA_PAD_WIDTH), BF16),
                   jax.ShapeDtypeStruct((B, MLA_WIDTH, S), BF16),
                   jax.ShapeDtypeStruct((B, MOBA_WIDTH, S), BF16),
                   jax.ShapeDtypeStruct((B, MOBA_WIDTH, S), BF16),
                   jax.ShapeDtypeStruct((T, MOBA_WIDTH), BF16),
                   jax.ShapeDtypeStruct((B, MOBA_WIDTH, S), BF16),
                   jax.ShapeDtypeStruct((T // MOBA_BLOCK, 1, MOBA_WIDTH), F32)],
        compiler_params=pltpu.CompilerParams(vmem_limit_bytes=VMEM_LIMIT),
        name="token_projection",
    )(x2, row(attn_pre_g), win, wint, row(mla_q_norm_g), wqt, row(mla_kv_norm_g), wk, wvt,
      *tables, *tables_t)

    tq = ATTN_TILE
    width = 2 * tq
    tile_tab, chunk_tab = (jnp.asarray(t) for t in _schedule(n_q))
    seq_t = lambda w: pl.BlockSpec((1, w, S), lambda b, p, *_: (b, p, 0))
    seq = lambda w: pl.BlockSpec((1, S, w), lambda b, p, *_: (b, 0, p))
    attn_params = pltpu.CompilerParams(vmem_limit_bytes=VMEM_LIMIT)
    attn_scratch = [pltpu.VMEM((tq, width), F32), pltpu.VMEM((tq, width), F32),
                    pltpu.VMEM((n_q, MLA_V_DIM + SUBLANES_BF16 + SUBLANES, width), F32),
                    pltpu.VMEM((tq // 2, tq // 2), F32)]

    mla_o = pl.pallas_call(
        _mla_attn_kernel,
        grid_spec=pltpu.PrefetchScalarGridSpec(
            num_scalar_prefetch=2, grid=(B, MLA_HEADS // 2),
            in_specs=[seq_t(2 * LANES), seq(2 * LANES), seq_t(LANES)],
            out_specs=seq(LANES), scratch_shapes=attn_scratch),
        out_shape=jax.ShapeDtypeStruct((B, S, MLA_WIDTH), BF16),
        compiler_params=attn_params,
        name="mla_attention",
    )(tile_tab, chunk_tab, mqt, mk.reshape(B, S, MLA_PAD_WIDTH), mvt)

    moba_bias = pl.pallas_call(
        functools.partial(_moba_gate_kernel, n_blocks=n_blocks),
        grid=(B, MOBA_HEADS // 2),
        in_specs=[pl.BlockSpec((1, LANES, S), lambda b, p: (b, p, 0)),
                  pl.BlockSpec((1, LANES, S), lambda b, p: (b, p, 0)),
                  pl.BlockSpec((1, n_blocks, LANES), lambda b, p: (b, 0, p))],
        out_specs=pl.BlockSpec((1, 1, n_q * n_blocks, width), lambda b, p: (b, p, 0, 0)),
        out_shape=jax.ShapeDtypeStruct((B, MOBA_HEADS // 2, n_q * n_blocks, width), F32),
        name="moba_gate",
    )(oqt, oqft, kmean.reshape(B, n_blocks, MOBA_WIDTH))

    moba_o = pl.pallas_call(
        functools.partial(_moba_attn_kernel, n_blocks=n_blocks),
        grid_spec=pltpu.PrefetchScalarGridSpec(
            num_scalar_prefetch=2, grid=(B, MOBA_HEADS // 2),
            in_specs=[seq_t(LANES), seq(LANES), seq_t(LANES),
                      pl.BlockSpec((1, 1, n_q * n_blocks, width), lambda b, p, *_: (b, p, 0, 0))],
            out_specs=seq(LANES), scratch_shapes=attn_scratch),
        out_shape=jax.ShapeDtypeStruct((B, S, MOBA_WIDTH), BF16),
        compiler_params=attn_params,
        name="moba_attention",
    )(tile_tab, chunk_tab, oqt, ok.reshape(B, S, MOBA_WIDTH), ovt, moba_bias)

    tm = MLP_TILE
    d_ff = w_up.shape[1]
    tok = lambda w: pl.BlockSpec((tm, w), lambda i: (i, 0))
    out = pl.pallas_call(
        functools.partial(_mlp_kernel, ff_chunk=FF_CHUNK, sub_tile=MLP_SUB_TILE),
        grid=(T // tm,),
        in_specs=[tok(D), tok(MLA_WIDTH), tok(MOBA_WIDTH),
                  _const_spec((MLA_WIDTH + MOBA_WIDTH, D)), _const_spec((1, D)), _const_spec((1, D)),
                  _const_spec((D, d_ff)), _const_spec((d_ff, D)), _const_spec((1, D))],
        out_specs=tok(D),
        out_shape=jax.ShapeDtypeStruct((T, D), F32),
        compiler_params=pltpu.CompilerParams(vmem_limit_bytes=VMEM_LIMIT),
        name="out_proj_mlp",
    )(x2, mla_o.reshape(T, MLA_WIDTH), moba_o.reshape(T, MOBA_WIDTH),
      w_out.astype(BF16), row(attn_post_g), row(mlp_pre_g),
      w_up.astype(BF16), w_down.astype(BF16), row(mlp_post_g))
    return out.reshape(B, S, D)


def kernel(x, attn_pre_g, w_in, mla_q_norm_g, w_mla_q_up, mla_kv_norm_g, w_mla_kv_up, w_out,
           attn_post_g, mlp_pre_g, w_up, w_down, mlp_post_g):
    h = x
    for l in range(w_in.shape[0]):
        h = _layer(h, attn_pre_g[l], w_in[l], mla_q_norm_g[l], w_mla_q_up[l], mla_kv_norm_g[l],
                   w_mla_kv_up[l], w_out[l], attn_post_g[l], mlp_pre_g[l], w_up[l], w_down[l],
                   mlp_post_g[l])
    return h
```

```python
import functools
import math

import numpy as np
import jax
import jax.numpy as jnp
from jax import lax
from jax.experimental import pallas as pl
from jax.experimental.pallas import tpu as pltpu

MLA_HEADS = 8
MLA_NOPE_DIM = 64
MLA_ROPE_DIM = 32
MLA_V_DIM = 64
MLA_Q_RANK = 256
MLA_KV_RANK = 128
MOBA_HEADS = 8
MOBA_HEAD_DIM = 64
MOBA_BLOCK = 256
MOBA_TOPK = 3
ROPE_THETA = 10000.0
NORM_EPS = 1e-6
NEG_INF = -1e30

LANES = 128
SUBLANES = 8
SUBLANES_BF16 = 16
HALF = LANES // 2
MOBA_WIDTH = MOBA_HEADS * MOBA_HEAD_DIM
MLA_WIDTH = MLA_HEADS * MLA_V_DIM
MLA_PAD_WIDTH = MLA_HEADS * LANES
PROJ_TILE = 1024
PROJ_SUB_TILE = 512
ATTN_TILE = 512
MLP_TILE = 1024
MLP_SUB_TILE = 256
FF_CHUNK = 1024
VMEM_LIMIT = 56 * 1024 * 1024

F32 = jnp.float32
BF16 = jnp.bfloat16
_NT = (((1,), (1,)), ((), ()))


def _moba_pair_columns(base):
    cols = []
    for p in range(MOBA_HEADS // 2):
        for off in (0, MOBA_HEAD_DIM // 2):
            for h in (2 * p, 2 * p + 1):
                cols += [base + h * MOBA_HEAD_DIM + off + i for i in range(MOBA_HEAD_DIM // 2)]
    return cols


def _w_in_columns():
    s2 = MLA_Q_RANK + MLA_KV_RANK
    s3 = s2 + MLA_ROPE_DIM
    half = MLA_ROPE_DIM // 2
    shared = [-1] * LANES
    for i in range(half):
        shared[32 + i] = s2 + i
        shared[96 + i] = s2 + half + i
    row_major = list(range(0, s2)) + shared + _moba_pair_columns(s3 + MOBA_WIDTH)
    feature_major = _moba_pair_columns(s3) + list(range(s3 + 2 * MOBA_WIDTH, s3 + 3 * MOBA_WIDTH))
    return np.asarray(row_major, np.int32), np.asarray(feature_major, np.int32)


def _mla_q_columns():
    cols = []
    half = MLA_ROPE_DIM // 2
    for h in range(MLA_HEADS):
        b = h * (MLA_NOPE_DIM + MLA_ROPE_DIM)
        cols += [b + i for i in range(32)] + [b + MLA_NOPE_DIM + i for i in range(half)] + [-1] * 16
        cols += [b + 32 + i for i in range(32)] + [b + MLA_NOPE_DIM + half + i for i in range(half)] + [-1] * 16
    return np.asarray(cols, np.int32)


def _mla_k_columns():
    cols = []
    for h in range(MLA_HEADS):
        b = h * (MLA_NOPE_DIM + MLA_V_DIM)
        cols += [b + i for i in range(32)] + [-1] * 32 + [b + 32 + i for i in range(32)] + [-1] * 32
    return np.asarray(cols, np.int32)


def _mla_v_columns():
    cols = []
    for h in range(MLA_HEADS):
        b = h * (MLA_NOPE_DIM + MLA_V_DIM) + MLA_NOPE_DIM
        cols += [b + i for i in range(MLA_V_DIM)]
    return np.asarray(cols, np.int32)


def _gather_columns(w, cols):
    pieces, start = [], 0
    for i in range(1, len(cols) + 1):
        continues = i < len(cols) and (
            (cols[i] == -1 and cols[i - 1] == -1) or (cols[i - 1] >= 0 and cols[i] == cols[i - 1] + 1))
        if not continues:
            if cols[start] < 0:
                pieces.append(jnp.zeros((w.shape[0], i - start), BF16))
            else:
                pieces.append(w[:, int(cols[start]):int(cols[start]) + i - start].astype(BF16))
            start = i
    return jnp.concatenate(pieces, axis=1)


def _rope_tables(seq):
    pos = jnp.arange(seq, dtype=F32)

    def cs(half):
        inv_freq = 1.0 / (ROPE_THETA ** (jnp.arange(half, dtype=F32) / half))
        ang = pos[:, None] * inv_freq[None, :]
        return jnp.cos(ang), jnp.sin(ang)

    c, s = cs(MLA_ROPE_DIM // 2)
    one = lambda n: jnp.ones((seq, n), F32)
    zero = lambda n: jnp.zeros((seq, n), F32)
    cos_mla = jnp.concatenate([one(32), c, one(16), one(32), c, one(16)], axis=1)
    sin_mla = jnp.concatenate([zero(32), -s, zero(16), zero(32), s, zero(16)], axis=1)
    c, s = cs(MOBA_HEAD_DIM // 2)
    cos_moba = jnp.concatenate([c, c, c, c], axis=1)
    sin_moba = jnp.concatenate([-s, -s, s, s], axis=1)
    return cos_mla, sin_mla, cos_moba, sin_moba


def _rms(x, g):
    ms = jnp.mean(x * x, axis=-1, keepdims=True)
    return x * lax.rsqrt(ms + NORM_EPS) * g


def _rope_rows(x, cos, sin):
    return x * cos + pltpu.roll(x, HALF, 1) * sin


def _rope_cols(x, cos, sin):
    return x * cos + jnp.concatenate([x[HALF:], x[:HALF]], axis=0) * sin


def _proj_kernel(x_ref, g_ref, win_ref, wint_ref, gq_ref, wqt_ref, gkv_ref, wk_ref, wvt_ref,
                 cm_ref, sm_ref, co_ref, so_ref, cmt_ref, smt_ref, cot_ref, sot_ref,
                 mqt_ref, mk_ref, mvt_ref, oqt_ref, oqft_ref, ok_ref, ovt_ref, kmean_ref,
                 *, mla_scale, moba_scale):
    subs = [slice(r * PROJ_SUB_TILE, (r + 1) * PROJ_SUB_TILE)
            for r in range(x_ref.shape[0] // PROJ_SUB_TILE)]

    def normed(rows):
        return _rms(x_ref[rows, :], g_ref[...]).astype(BF16)

    u = normed(subs[0])
    for r, rows in enumerate(subs):
        u = _proj_rows(r, rows, u, (lambda nxt=subs[r + 1]: normed(nxt)) if r + 1 < len(subs) else None,
                       win_ref, wint_ref, gq_ref, wqt_ref, gkv_ref, wk_ref, wvt_ref,
                       cm_ref, sm_ref, co_ref, so_ref, cmt_ref, smt_ref, cot_ref, sot_ref,
                       mqt_ref, mk_ref, mvt_ref, oqt_ref, oqft_ref, ok_ref, ovt_ref, kmean_ref,
                       mla_scale, moba_scale)


def _proj_rows(r, rows, u, next_u, win_ref, wint_ref, gq_ref, wqt_ref, gkv_ref, wk_ref, wvt_ref,
               cm_ref, sm_ref, co_ref, so_ref, cmt_ref, smt_ref, cot_ref, sot_ref,
               mqt_ref, mk_ref, mvt_ref, oqt_ref, oqft_ref, ok_ref, ovt_ref, kmean_ref,
               mla_scale, moba_scale):
    def proj(lo, hi):
        return jnp.dot(u, win_ref[:, lo:hi], preferred_element_type=F32)

    def proj_t(lo, hi):
        return lax.dot_general(wint_ref[lo:hi, :], u, _NT, preferred_element_type=F32)

    c0 = MLA_Q_RANK
    c1 = c0 + MLA_KV_RANK
    c2 = c1 + LANES
    c3 = c2 + MOBA_WIDTH

    latent = proj(0, c2)
    u_following = next_u() if next_u is not None else None
    cq = _rms(latent[:, :c0], gq_ref[...]).astype(BF16)
    qt = lax.dot_general(wqt_ref[...], cq, _NT, preferred_element_type=F32)
    cmt, smt = cmt_ref[:, rows], smt_ref[:, rows]
    for h in range(MLA_HEADS):
        sl = slice(h * LANES, (h + 1) * LANES)
        mqt_ref[0, sl, rows] = (_rope_cols(qt[sl], cmt, smt) * mla_scale).astype(BF16)

    ckv = _rms(latent[:, c0:c1], gkv_ref[...]).astype(BF16)
    k_shared = _rope_rows(latent[:, c1:c2], cm_ref[rows, :], sm_ref[rows, :])
    k_nope = jnp.dot(ckv, wk_ref[...], preferred_element_type=F32)
    for h in range(MLA_HEADS):
        sl = slice(h * LANES, (h + 1) * LANES)
        mk_ref[rows, sl] = (k_nope[:, sl] + k_shared).astype(BF16)
    mvt_ref[0, :, rows] = lax.dot_general(wvt_ref[...], ckv, _NT,
                                          preferred_element_type=F32).astype(BF16)

    oqt = proj_t(0, MOBA_WIDTH)
    ok = proj(c2, c3)
    co, so = co_ref[rows, :], so_ref[rows, :]
    cot, sot = cot_ref[:, rows], sot_ref[:, rows]
    for p in range(MOBA_WIDTH // LANES):
        sl = slice(p * LANES, (p + 1) * LANES)
        q_rot = _rope_cols(oqt[sl], cot, sot) * moba_scale
        q_hi = q_rot.astype(BF16)
        oqt_ref[0, sl, rows] = q_hi
        oqft_ref[0, sl, rows] = (q_rot - q_hi.astype(F32)).astype(BF16)
        k_rot = _rope_rows(ok[:, sl], co, so)
        ok_ref[rows, sl] = k_rot.astype(BF16)
        for b in range(PROJ_SUB_TILE // MOBA_BLOCK):
            kmean_ref[r * (PROJ_SUB_TILE // MOBA_BLOCK) + b, :, sl] = jnp.mean(
                k_rot[b * MOBA_BLOCK:(b + 1) * MOBA_BLOCK], axis=0, keepdims=True)
    ovt_ref[0, :, rows] = proj_t(MOBA_WIDTH, 2 * MOBA_WIDTH).astype(BF16)
    return u_following


def _schedule(n_q):
    tiles = [q for q in range(n_q) for _ in range(q)]
    chunks = [c for q in range(n_q) for c in range(q)]
    return np.asarray(tiles + tiles[-1:], np.int32), np.asarray(chunks + chunks[-1:], np.int32)


def _heads_to_rows(out_t, n_heads, tq):
    return jnp.concatenate([out_t[:, h * tq:(h + 1) * tq] for h in range(n_heads)], axis=0).T


def _attend_row(tile_tab, chunk_tab, q_cat_of, k_ref, vt_ref, o_ref, s_bufs, st_ref, tri_ref,
                n_heads, bias_of=None):
    tq = tk = ATTN_TILE
    n_q = st_ref.shape[0]
    n_steps = n_q * (n_q - 1) // 2
    unroll = next(u for u in (14, 4, 2) if n_steps % u == 0)
    diag_unroll = 4 if n_q % 4 == 0 else 2
    width = n_heads * tq
    d_v = vt_ref.shape[1] // n_heads
    acc_rows = d_v + SUBLANES_BF16
    half = tq // 2
    per_chunk = tk // MOBA_BLOCK
    assert bias_of is None or half == MOBA_BLOCK
    ones = jnp.ones((SUBLANES_BF16, tk), BF16)

    def scores(tile, chunk, s_ref):
        start = pl.multiple_of(chunk * tk, tk)
        s = jnp.concatenate(
            [jnp.dot(k_ref[0, pl.ds(start, tk), lanes], q, preferred_element_type=F32)
             for lanes, q in q_cat_of(tile)], axis=1)
        s_ref[...] = s
        if bias_of is None:
            return jnp.max(s, axis=0, keepdims=True)
        return functools.reduce(jnp.maximum, [
            jnp.max(s[j * MOBA_BLOCK:(j + 1) * MOBA_BLOCK], axis=0, keepdims=True)
            + bias_of(tile, per_chunk * chunk + j) for j in range(per_chunk)])

    def exponent_base(m_new, bias):
        return jnp.where(m_new < 0.1 * NEG_INF, -NEG_INF, m_new - bias)

    def values(h, start, size):
        return jnp.concatenate([vt_ref[0, h * d_v:(h + 1) * d_v, pl.ds(start, size)], ones[:, :size]],
                               axis=0)

    def accumulate(tile, chunk, state, s, s_max):
        m, acc = state
        m_new = jnp.maximum(m, s_max)
        alpha = jnp.exp2(m - m_new)
        if bias_of is None:
            p = jnp.exp2(s - m_new).astype(BF16)
        else:
            p = jnp.concatenate(
                [jnp.exp2(s[j * MOBA_BLOCK:(j + 1) * MOBA_BLOCK]
                          - exponent_base(m_new, bias_of(tile, per_chunk * chunk + j)))
                 for j in range(per_chunk)], axis=0).astype(BF16)
        start = pl.multiple_of(chunk * tk, tk)
        pv = [jnp.dot(values(h, start, tk), p[:, h * tq:(h + 1) * tq], preferred_element_type=F32)
              for h in range(n_heads)]
        return m_new, alpha * acc + jnp.concatenate(pv, axis=1)

    def late(x):
        return jnp.concatenate([x[:, h * tq + half:(h + 1) * tq] for h in range(n_heads)], axis=1)

    def diagonal_scores(tile, s_ref):
        start = pl.multiple_of(tile * tk, tk)
        operands = q_cat_of(tile)
        s_early = jnp.concatenate(
            [jnp.dot(k_ref[0, pl.ds(start, half), lanes], q, preferred_element_type=F32)
             for lanes, q in operands], axis=1)
        s_late = jnp.concatenate(
            [jnp.dot(k_ref[0, pl.ds(pl.multiple_of(start + half, half), half), lanes],
                     jnp.concatenate([q[:, c + half:c + tq] for c in range(0, q.shape[1], tq)], axis=1),
                     preferred_element_type=F32)
             for lanes, q in operands], axis=1)
        tri = tri_ref[...]
        s_early = jnp.concatenate(
            [part for h in range(n_heads)
             for part in (s_early[:, h * tq:h * tq + half] + tri, s_early[:, h * tq + half:(h + 1) * tq])],
            axis=1)
        s_late = jnp.concatenate([s_late[:, h * half:(h + 1) * half] + tri for h in range(n_heads)],
                                 axis=1)
        s_ref[:half, :] = s_early
        s_ref[half:, :width // 2] = s_late
        max_early = jnp.max(s_early, axis=0, keepdims=True)
        if bias_of is not None:
            max_early = max_early + bias_of(tile, per_chunk * tile)
        max_late = jnp.max(s_late, axis=0, keepdims=True)
        return jnp.concatenate(
            [part for h in range(n_heads)
             for part in (max_early[:, h * tq:h * tq + half],
                          jnp.maximum(max_early[:, h * tq + half:(h + 1) * tq],
                                      max_late[:, h * half:(h + 1) * half]))], axis=1)

    def diagonal_accumulate(tile, state, s_ref, s_max):
        m, acc = state
        m_new = jnp.maximum(m, s_max)
        alpha = jnp.exp2(m - m_new)
        base = m_new if bias_of is None else exponent_base(m_new, bias_of(tile, per_chunk * tile))
        p_early = jnp.exp2(s_ref[:half, :] - base).astype(BF16)
        p_late = jnp.exp2(s_ref[half:, :width // 2] - late(m_new)).astype(BF16)
        start = pl.multiple_of(tile * tk, tk)
        pv = []
        for h in range(n_heads):
            out = jnp.dot(values(h, start, half), p_early[:, h * tq:(h + 1) * tq],
                          preferred_element_type=F32)
            out_late = jnp.dot(values(h, pl.multiple_of(start + half, half), half),
                               p_late[:, h * half:(h + 1) * half], preferred_element_type=F32)
            pv += [out[:, :half], out[:, half:] + out_late]
        return m_new, alpha * acc + jnp.concatenate(pv, axis=1)

    def save(tile, state):
        m, acc = state
        st_ref[tile, :acc_rows, :] = acc
        st_ref[tile, acc_rows:, :] = jnp.broadcast_to(m, (SUBLANES, width))

    def full_steps(i, carry):
        state, s_max = carry
        for j in range(unroll):
            t = i * unroll + j
            next_max = scores(tile_tab[t + 1], chunk_tab[t + 1], s_bufs[(j + 1) % 2])
            tile, chunk = tile_tab[t], chunk_tab[t]
            m, acc = state
            fresh = chunk == 0
            state = accumulate(tile, chunk, (jnp.where(fresh, NEG_INF, m), jnp.where(fresh, 0.0, acc)),
                               s_bufs[j % 2][...], s_max)
            save(tile, state)
            s_max = next_max
        return state, s_max

    def diagonal_steps(i, s_max):
        for j in range(diag_unroll):
            tile = diag_unroll * i + j
            next_max = diagonal_scores(jnp.minimum(tile + 1, n_q - 1), s_bufs[(j + 1) % 2])
            state = (st_ref[tile, acc_rows:acc_rows + 1, :], st_ref[tile, :acc_rows, :])
            _, acc = diagonal_accumulate(tile, state, s_bufs[j % 2], s_max)
            out_t = acc[:d_v] * (1.0 / acc[d_v:d_v + 1])
            o_ref[0, pl.ds(pl.multiple_of(tile * tq, tq), tq), :] = (
                _heads_to_rows(out_t, n_heads, tq).astype(BF16))
            s_max = next_max
        return s_max

    empty = (jnp.full((1, width), NEG_INF, F32), jnp.zeros((acc_rows, width), F32))
    save(0, empty)
    lax.fori_loop(0, n_steps // unroll, full_steps,
                  (empty, scores(tile_tab[0], chunk_tab[0], s_bufs[0])))
    lax.fori_loop(0, n_q // diag_unroll, diagonal_steps, diagonal_scores(0, s_bufs[0]))


def _init_causal_bias(tri_ref):
    key = lax.broadcasted_iota(jnp.int32, tri_ref.shape, 0)
    qry = lax.broadcasted_iota(jnp.int32, tri_ref.shape, 1)
    tri_ref[...] = jnp.where(key <= qry, 0.0, NEG_INF)


def _mla_attn_kernel(tile_tab, chunk_tab, qt_ref, k_ref, vt_ref, o_ref, s_a, s_b, st_ref, tri_ref):
    tq = ATTN_TILE
    _init_causal_bias(tri_ref)

    def q_cat_of(tile):
        q = qt_ref[0, :, pl.ds(pl.multiple_of(tile * tq, tq), tq)]
        return [(slice(h * LANES, (h + 1) * LANES), q[h * LANES:(h + 1) * LANES]) for h in range(2)]

    _attend_row(tile_tab, chunk_tab, q_cat_of, k_ref, vt_ref, o_ref, (s_a, s_b), st_ref, tri_ref, 2)


def _moba_gate_kernel(q_hi_ref, q_lo_ref, km_ref, bias_ref, *, n_blocks):
    tq = ATTN_TILE
    feat = lax.broadcasted_iota(jnp.int32, (LANES, tq), 0)
    col = lax.broadcasted_iota(jnp.int32, (1, 2 * tq), 1)
    km = km_ref[0]
    km_hi = km.astype(BF16)
    km_lo = (km - km_hi.astype(F32)).astype(BF16)

    def heads_side_by_side(q):
        return jnp.concatenate(
            [jnp.where((feat & (MOBA_HEAD_DIM // 2)) == h * (MOBA_HEAD_DIM // 2), q,
                       jnp.zeros((), BF16)) for h in range(2)], axis=1)

    for tile in range(q_hi_ref.shape[2] // tq):
        cur = (tile * tq + (col & (tq - 1))) // MOBA_BLOCK
        n_seen = (tile + 1) * tq // MOBA_BLOCK
        rows = min(n_blocks, -(-n_seen // SUBLANES) * SUBLANES)
        blk = lax.broadcasted_iota(jnp.int32, (rows, 2 * tq), 0)
        q_hi = heads_side_by_side(q_hi_ref[0, :, tile * tq:(tile + 1) * tq])
        q_lo = heads_side_by_side(q_lo_ref[0, :, tile * tq:(tile + 1) * tq])
        dot = lambda a, b: jnp.dot(a[:rows], b, preferred_element_type=F32)
        gate = (dot(km_lo, q_lo) + dot(km_lo, q_hi)) + dot(km_hi, q_lo) + dot(km_hi, q_hi)
        gate = jnp.where(blk < cur, gate, -jnp.inf)
        ahead_count = jnp.zeros((rows, 2 * tq), F32)
        for jp in range(n_seen - 1):
            g = gate[jp:jp + 1, :]
            ahead = jnp.where(jp < blk, jnp.where(g >= gate, 1.0, 0.0), jnp.where(g > gate, 1.0, 0.0))
            ahead_count = ahead_count + ahead
        keep = ((blk < cur) & (ahead_count < MOBA_TOPK)) | (blk == cur)
        base = tile * n_blocks
        bias_ref[0, 0, base:base + rows, :] = jnp.where(keep, 0.0, NEG_INF)
        if rows < n_blocks:
            bias_ref[0, 0, base + rows:base + n_blocks, :] = jnp.full((n_blocks - rows, 2 * tq),
                                                                      NEG_INF, F32)


def _moba_attn_kernel(tile_tab, chunk_tab, qt_ref, k_ref, vt_ref, bias_ref, o_ref,
                      s_a, s_b, st_ref, tri_ref, *, n_blocks):
    tq = ATTN_TILE
    _init_causal_bias(tri_ref)
    feat = lax.broadcasted_iota(jnp.int32, (LANES, tq), 0)

    def q_cat_of(tile):
        q = qt_ref[0, :, pl.ds(pl.multiple_of(tile * tq, tq), tq)]
        return [(slice(0, LANES), jnp.concatenate(
            [jnp.where((feat & (MOBA_HEAD_DIM // 2)) == h * (MOBA_HEAD_DIM // 2), q,
                       jnp.zeros((), BF16)) for h in range(2)], axis=1))]

    def bias_of(tile, block):
        return bias_ref[0, 0, pl.ds(tile * n_blocks + block, 1), :]

    _attend_row(tile_tab, chunk_tab, q_cat_of, k_ref, vt_ref, o_ref, (s_a, s_b), st_ref, tri_ref, 2,
                bias_of)


def _mlp_kernel(x_ref, a_ref, b_ref, wo_ref, g1_ref, g2_ref, wu_ref, wd_ref, g3_ref, o_ref,
                *, ff_chunk, sub_tile):
    n_a = a_ref.shape[1]
    subs = [slice(r * sub_tile, (r + 1) * sub_tile) for r in range(x_ref.shape[0] // sub_tile)]

    def attn_out(rows):
        return (jnp.dot(a_ref[rows, :], wo_ref[0:n_a, :], preferred_element_type=F32)
                + jnp.dot(b_ref[rows, :], wo_ref[n_a:, :], preferred_element_type=F32))

    def before(rows, y):
        h = x_ref[rows, :] + _rms(y, g1_ref[...])
        return h, _rms(h, g2_ref[...]).astype(BF16)

    def after(rows, h, acc):
        o_ref[rows, :] = h + _rms(acc, g3_ref[...])

    ys = [attn_out(rows) for rows in subs]
    ready = before(subs[0], ys[0])
    done = None
    for r, rows in enumerate(subs):
        h, u = ready
        acc = jnp.zeros(h.shape, F32)
        for c in range(wu_ref.shape[1] // ff_chunk):
            sl = slice(c * ff_chunk, (c + 1) * ff_chunk)
            a = jnp.maximum(jnp.dot(u, wu_ref[:, sl], preferred_element_type=F32), 0.0)
            acc = acc + jnp.dot((a * a).astype(BF16), wd_ref[sl, :], preferred_element_type=F32)
            if c == 0:
                if r + 1 < len(subs):
                    ready = before(subs[r + 1], ys[r + 1])
                if done is not None:
                    after(*done)
        done = (rows, h, acc)
    after(*done)


def _const_spec(shape):
    return pl.BlockSpec(shape, lambda *_: (0,) * len(shape), pipeline_mode=pl.Buffered(1))


def _layer(x, attn_pre_g, w_in, mla_q_norm_g, w_mla_q_up, mla_kv_norm_g, w_mla_kv_up,
           w_out, attn_post_g, mlp_pre_g, w_up, w_down, mlp_post_g):
    B, S, D = x.shape
    T = B * S
    n_blocks = S // MOBA_BLOCK
    n_q = S // ATTN_TILE
    assert S % ATTN_TILE == 0 and ATTN_TILE % MOBA_BLOCK == 0
    assert n_q % 2 == 0 and (n_q * (n_q - 1) // 2) % 2 == 0
    row = lambda g: g.reshape(1, -1).astype(F32)

    cols_rm, cols_fm = _w_in_columns()
    win = _gather_columns(w_in, cols_rm)
    wint = _gather_columns(w_in, cols_fm).T
    wqt = _gather_columns(w_mla_q_up, _mla_q_columns()).T
    wk = _gather_columns(w_mla_kv_up, _mla_k_columns())
    wvt = _gather_columns(w_mla_kv_up, _mla_v_columns()).T
    tables = _rope_tables(S)
    tables_t = tuple(t.T for t in tables)

    tm = PROJ_TILE
    n_st = S // tm
    tok = lambda w: pl.BlockSpec((tm, w), lambda i: (i, 0))
    tok_t = lambda w: pl.BlockSpec((1, w, tm), lambda i: (i // n_st, 0, i % n_st))
    pos = pl.BlockSpec((tm, LANES), lambda i: (i % n_st, 0))
    pos_t = pl.BlockSpec((LANES, tm), lambda i: (0, i % n_st))
    x2 = x.reshape(T, D)
    log2e = math.log2(math.e)
    mqt, mk, mvt, oqt, oqft, ok, ovt, kmean = pl.pallas_call(
        functools.partial(_proj_kernel,
                          mla_scale=float((MLA_NOPE_DIM + MLA_ROPE_DIM) ** -0.5 * log2e),
                          moba_scale=float(MOBA_HEAD_DIM ** -0.5 * log2e)),
        grid=(T // tm,),
        in_specs=[tok(D), _const_spec((1, D)), _const_spec(win.shape), _const_spec(wint.shape),
                  _const_spec((1, MLA_Q_RANK)), _const_spec(wqt.shape),
                  _const_spec((1, MLA_KV_RANK)), _const_spec(wk.shape), _const_spec(wvt.shape),
                  pos, pos, pos, pos, pos_t, pos_t, pos_t, pos_t],
        out_specs=[tok_t(MLA_PAD_WIDTH), tok(MLA_PAD_WIDTH), tok_t(MLA_WIDTH),
                   tok_t(MOBA_WIDTH), tok_t(MOBA_WIDTH), tok(MOBA_WIDTH), tok_t(MOBA_WIDTH),
                   pl.BlockSpec((tm // MOBA_BLOCK, 1, MOBA_WIDTH), lambda i: (i, 0, 0))],
        out_shape=[jax.ShapeDtypeStruct((B, MLA_PAD_WIDTH, S), BF16),
                   jax.ShapeDtypeStruct((T, MLA_PAD_WIDTH), BF16),
                   jax.ShapeDtypeStruct((B, MLA_WIDTH, S), BF16),
                   jax.ShapeDtypeStruct((B, MOBA_WIDTH, S), BF16),
                   jax.ShapeDtypeStruct((B, MOBA_WIDTH, S), BF16),
                   jax.ShapeDtypeStruct((T, MOBA_WIDTH), BF16),
                   jax.ShapeDtypeStruct((B, MOBA_WIDTH, S), BF16),
                   jax.ShapeDtypeStruct((T // MOBA_BLOCK, 1, MOBA_WIDTH), F32)],
        compiler_params=pltpu.CompilerParams(vmem_limit_bytes=VMEM_LIMIT),
        name="token_projection",
    )(x2, row(attn_pre_g), win, wint, row(mla_q_norm_g), wqt, row(mla_kv_norm_g), wk, wvt,
      *tables, *tables_t)

    tq = ATTN_TILE
    width = 2 * tq
    tile_tab, chunk_tab = (jnp.asarray(t) for t in _schedule(n_q))
    seq_t = lambda w: pl.BlockSpec((1, w, S), lambda b, p, *_: (b, p, 0))
    seq = lambda w: pl.BlockSpec((1, S, w), lambda b, p, *_: (b, 0, p))
    attn_params = pltpu.CompilerParams(vmem_limit_bytes=VMEM_LIMIT)
    attn_scratch = [pltpu.VMEM((tq, width), F32), pltpu.VMEM((tq, width), F32),
                    pltpu.VMEM((n_q, MLA_V_DIM + SUBLANES_BF16 + SUBLANES, width), F32),
                    pltpu.VMEM((tq // 2, tq // 2), F32)]

    mla_o = pl.pallas_call(
        _mla_attn_kernel,
        grid_spec=pltpu.PrefetchScalarGridSpec(
            num_scalar_prefetch=2, grid=(B, MLA_HEADS // 2),
            in_specs=[seq_t(2 * LANES), seq(2 * LANES), seq_t(LANES)],
            out_specs=seq(LANES), scratch_shapes=attn_scratch),
        out_shape=jax.ShapeDtypeStruct((B, S, MLA_WIDTH), BF16),
        compiler_params=attn_params,
        name="mla_attention",
    )(tile_tab, chunk_tab, mqt, mk.reshape(B, S, MLA_PAD_WIDTH), mvt)

    moba_bias = pl.pallas_call(
        functools.partial(_moba_gate_kernel, n_blocks=n_blocks),
        grid=(B, MOBA_HEADS // 2),
        in_specs=[pl.BlockSpec((1, LANES, S), lambda b, p: (b, p, 0)),
                  pl.BlockSpec((1, LANES, S), lambda b, p: (b, p, 0)),
                  pl.BlockSpec((1, n_blocks, LANES), lambda b, p: (b, 0, p))],
        out_specs=pl.BlockSpec((1, 1, n_q * n_blocks, width), lambda b, p: (b, p, 0, 0)),
        out_shape=jax.ShapeDtypeStruct((B, MOBA_HEADS // 2, n_q * n_blocks, width), F32),
        name="moba_gate",
    )(oqt, oqft, kmean.reshape(B, n_blocks, MOBA_WIDTH))

    moba_o = pl.pallas_call(
        functools.partial(_moba_attn_kernel, n_blocks=n_blocks),
        grid_spec=pltpu.PrefetchScalarGridSpec(
            num_scalar_prefetch=2, grid=(B, MOBA_HEADS // 2),
            in_specs=[seq_t(LANES), seq(LANES), seq_t(LANES),
                      pl.BlockSpec((1, 1, n_q * n_blocks, width), lambda b, p, *_: (b, p, 0, 0))],
            out_specs=seq(LANES), scratch_shapes=attn_scratch),
        out_shape=jax.ShapeDtypeStruct((B, S, MOBA_WIDTH), BF16),
        compiler_params=attn_params,
        name="moba_attention",
    )(tile_tab, chunk_tab, oqt, ok.reshape(B, S, MOBA_WIDTH), ovt, moba_bias)

    tm = MLP_TILE
    d_ff = w_up.shape[1]
    tok = lambda w: pl.BlockSpec((tm, w), lambda i: (i, 0))
    out = pl.pallas_call(
        functools.partial(_mlp_kernel, ff_chunk=FF_CHUNK, sub_tile=MLP_SUB_TILE),
        grid=(T // tm,),
        in_specs=[tok(D), tok(MLA_WIDTH), tok(MOBA_WIDTH),
                  _const_spec((MLA_WIDTH + MOBA_WIDTH, D)), _const_spec((1, D)), _const_spec((1, D)),
                  _const_spec((D, d_ff)), _const_spec((d_ff, D)), _const_spec((1, D))],
        out_specs=tok(D),
        out_shape=jax.ShapeDtypeStruct((T, D), F32),
        compiler_params=pltpu.CompilerParams(vmem_limit_bytes=VMEM_LIMIT),
        name="out_proj_mlp",
    )(x2, mla_o.reshape(T, MLA_WIDTH), moba_o.reshape(T, MOBA_WIDTH),
      w_out.astype(BF16), row(attn_post_g), row(mlp_pre_g),
      w_up.astype(BF16), w_down.astype(BF16), row(mlp_post_g))
    return out.reshape(B, S, D)


def kernel(x, attn_pre_g, w_in, mla_q_norm_g, w_mla_q_up, mla_kv_norm_g, w_mla_kv_up, w_out,
           attn_post_g, mlp_pre_g, w_up, w_down, mlp_post_g):
    h = x
    for l in range(w_in.shape[0]):
        h = _layer(h, attn_pre_g[l], w_in[l], mla_q_norm_g[l], w_mla_q_up[l], mla_kv_norm_g[l],
                   w_mla_kv_up[l], w_out[l], attn_post_g[l], mlp_pre_g[l], w_up[l], w_down[l],
                   mlp_post_g[l])
    return h
```

```python
import functools
import math

import numpy as np
import jax
import jax.numpy as jnp
from jax import lax
from jax.experimental import pallas as pl
from jax.experimental.pallas import tpu as pltpu

MLA_HEADS = 8
MLA_NOPE_DIM = 64
MLA_ROPE_DIM = 32
MLA_V_DIM = 64
MLA_Q_RANK = 256
MLA_KV_RANK = 128
MOBA_HEADS = 8
MOBA_HEAD_DIM = 64
MOBA_BLOCK = 256
MOBA_TOPK = 3
ROPE_THETA = 10000.0
NORM_EPS = 1e-6
NEG_INF = -1e30

LANES = 128
SUBLANES = 8
SUBLANES_BF16 = 16
HALF = LANES // 2
MOBA_WIDTH = MOBA_HEADS * MOBA_HEAD_DIM
MLA_WIDTH = MLA_HEADS * MLA_V_DIM
MLA_PAD_WIDTH = MLA_HEADS * LANES
PROJ_TILE = 1024
PROJ_SUB_TILE = 1024
ATTN_TILE = 512
MLP_TILE = 1024
MLP_SUB_TILE = 256
FF_CHUNK = 1024
VMEM_LIMIT = 56 * 1024 * 1024

F32 = jnp.float32
BF16 = jnp.bfloat16
_NT = (((1,), (1,)), ((), ()))


def _moba_pair_columns(base):
    cols = []
    for p in range(MOBA_HEADS // 2):
        for off in (0, MOBA_HEAD_DIM // 2):
            for h in (2 * p, 2 * p + 1):
                cols += [base + h * MOBA_HEAD_DIM + off + i for i in range(MOBA_HEAD_DIM // 2)]
    return cols


def _w_in_columns():
    s2 = MLA_Q_RANK + MLA_KV_RANK
    s3 = s2 + MLA_ROPE_DIM
    half = MLA_ROPE_DIM // 2
    shared = [-1] * LANES
    for i in range(half):
        shared[32 + i] = s2 + i
        shared[96 + i] = s2 + half + i
    row_major = list(range(0, s2)) + shared + _moba_pair_columns(s3 + MOBA_WIDTH)
    feature_major = _moba_pair_columns(s3) + list(range(s3 + 2 * MOBA_WIDTH, s3 + 3 * MOBA_WIDTH))
    return np.asarray(row_major, np.int32), np.asarray(feature_major, np.int32)


def _mla_q_columns():
    cols = []
    half = MLA_ROPE_DIM // 2
    for h in range(MLA_HEADS):
        b = h * (MLA_NOPE_DIM + MLA_ROPE_DIM)
        cols += [b + i for i in range(32)] + [b + MLA_NOPE_DIM + i for i in range(half)] + [-1] * 16
        cols += [b + 32 + i for i in range(32)] + [b + MLA_NOPE_DIM + half + i for i in range(half)] + [-1] * 16
    return np.asarray(cols, np.int32)


def _mla_k_columns():
    cols = []
    for h in range(MLA_HEADS):
        b = h * (MLA_NOPE_DIM + MLA_V_DIM)
        cols += [b + i for i in range(32)] + [-1] * 32 + [b + 32 + i for i in range(32)] + [-1] * 32
    return np.asarray(cols, np.int32)


def _mla_v_columns():
    cols = []
    for h in range(MLA_HEADS):
        b = h * (MLA_NOPE_DIM + MLA_V_DIM) + MLA_NOPE_DIM
        cols += [b + i for i in range(MLA_V_DIM)]
    return np.asarray(cols, np.int32)


def _gather_columns(w, cols):
    pieces, start = [], 0
    for i in range(1, len(cols) + 1):
        continues = i < len(cols) and (
            (cols[i] == -1 and cols[i - 1] == -1) or (cols[i - 1] >= 0 and cols[i] == cols[i - 1] + 1))
        if not continues:
            if cols[start] < 0:
                pieces.append(jnp.zeros((w.shape[0], i - start), BF16))
            else:
                pieces.append(w[:, int(cols[start]):int(cols[start]) + i - start].astype(BF16))
            start = i
    return jnp.concatenate(pieces, axis=1)


def _rope_tables(seq):
    pos = jnp.arange(seq, dtype=F32)

    def cs(half):
        inv_freq = 1.0 / (ROPE_THETA ** (jnp.arange(half, dtype=F32) / half))
        ang = pos[:, None] * inv_freq[None, :]
        return jnp.cos(ang), jnp.sin(ang)

    c, s = cs(MLA_ROPE_DIM // 2)
    one = lambda n: jnp.ones((seq, n), F32)
    zero = lambda n: jnp.zeros((seq, n), F32)
    cos_mla = jnp.concatenate([one(32), c, one(16), one(32), c, one(16)], axis=1)
    sin_mla = jnp.concatenate([zero(32), -s, zero(16), zero(32), s, zero(16)], axis=1)
    c, s = cs(MOBA_HEAD_DIM // 2)
    cos_moba = jnp.concatenate([c, c, c, c], axis=1)
    sin_moba = jnp.concatenate([-s, -s, s, s], axis=1)
    return cos_mla, sin_mla, cos_moba, sin_moba


def _rms(x, g):
    ms = jnp.mean(x * x, axis=-1, keepdims=True)
    return x * lax.rsqrt(ms + NORM_EPS) * g


def _rope_rows(x, cos, sin):
    return x * cos + pltpu.roll(x, HALF, 1) * sin


def _rope_cols(x, cos, sin):
    return x * cos + jnp.concatenate([x[HALF:], x[:HALF]], axis=0) * sin


def _proj_kernel(x_ref, g_ref, win_ref, wint_ref, gq_ref, wqt_ref, gkv_ref, wk_ref, wvt_ref,
                 cm_ref, sm_ref, co_ref, so_ref, cmt_ref, smt_ref, cot_ref, sot_ref,
                 mqt_ref, mk_ref, mvt_ref, oqt_ref, oqft_ref, ok_ref, ovt_ref, kmean_ref,
                 *, mla_scale, moba_scale):
    subs = [slice(r * PROJ_SUB_TILE, (r + 1) * PROJ_SUB_TILE)
            for r in range(x_ref.shape[0] // PROJ_SUB_TILE)]

    def normed(rows):
        return _rms(x_ref[rows, :], g_ref[...]).astype(BF16)

    u = normed(subs[0])
    for r, rows in enumerate(subs):
        u = _proj_rows(r, rows, u, (lambda nxt=subs[r + 1]: normed(nxt)) if r + 1 < len(subs) else None,
                       win_ref, wint_ref, gq_ref, wqt_ref, gkv_ref, wk_ref, wvt_ref,
                       cm_ref, sm_ref, co_ref, so_ref, cmt_ref, smt_ref, cot_ref, sot_ref,
                       mqt_ref, mk_ref, mvt_ref, oqt_ref, oqft_ref, ok_ref, ovt_ref, kmean_ref,
                       mla_scale, moba_scale)


def _proj_rows(r, rows, u, next_u, win_ref, wint_ref, gq_ref, wqt_ref, gkv_ref, wk_ref, wvt_ref,
               cm_ref, sm_ref, co_ref, so_ref, cmt_ref, smt_ref, cot_ref, sot_ref,
               mqt_ref, mk_ref, mvt_ref, oqt_ref, oqft_ref, ok_ref, ovt_ref, kmean_ref,
               mla_scale, moba_scale):
    def proj(lo, hi):
        return jnp.dot(u, win_ref[:, lo:hi], preferred_element_type=F32)

    def proj_t(lo, hi):
        return lax.dot_general(wint_ref[lo:hi, :], u, _NT, preferred_element_type=F32)

    c0 = MLA_Q_RANK
    c1 = c0 + MLA_KV_RANK
    c2 = c1 + LANES
    c3 = c2 + MOBA_WIDTH

    latent = proj(0, c2)
    u_following = next_u() if next_u is not None else None
    cq = _rms(latent[:, :c0], gq_ref[...]).astype(BF16)
    qt = lax.dot_general(wqt_ref[...], cq, _NT, preferred_element_type=F32)
    cmt, smt = cmt_ref[:, rows], smt_ref[:, rows]
    for h in range(MLA_HEADS):
        sl = slice(h * LANES, (h + 1) * LANES)
        mqt_ref[0, sl, rows] = (_rope_cols(qt[sl], cmt, smt) * mla_scale).astype(BF16)

    ckv = _rms(latent[:, c0:c1], gkv_ref[...]).astype(BF16)
    k_shared = _rope_rows(latent[:, c1:c2], cm_ref[rows, :], sm_ref[rows, :])
    k_nope = jnp.dot(ckv, wk_ref[...], preferred_element_type=F32)
    for h in range(MLA_HEADS):
        sl = slice(h * LANES, (h + 1) * LANES)
        mk_ref[rows, sl] = (k_nope[:, sl] + k_shared).astype(BF16)
    mvt_ref[0, :, rows] = lax.dot_general(wvt_ref[...], ckv, _NT,
                                          preferred_element_type=F32).astype(BF16)

    oqt = proj_t(0, MOBA_WIDTH)
    ok = proj(c2, c3)
    co, so = co_ref[rows, :], so_ref[rows, :]
    cot, sot = cot_ref[:, rows], sot_ref[:, rows]
    for p in range(MOBA_WIDTH // LANES):
        sl = slice(p * LANES, (p + 1) * LANES)
        q_rot = _rope_cols(oqt[sl], cot, sot) * moba_scale
        q_hi = q_rot.astype(BF16)
        oqt_ref[0, sl, rows] = q_hi
        oqft_ref[0, sl, rows] = (q_rot - q_hi.astype(F32)).astype(BF16)
        k_rot = _rope_rows(ok[:, sl], co, so)
        ok_ref[rows, sl] = k_rot.astype(BF16)
        for b in range(PROJ_SUB_TILE // MOBA_BLOCK):
            kmean_ref[r * (PROJ_SUB_TILE // MOBA_BLOCK) + b, :, sl] = jnp.mean(
                k_rot[b * MOBA_BLOCK:(b + 1) * MOBA_BLOCK], axis=0, keepdims=True)
    ovt_ref[0, :, rows] = proj_t(MOBA_WIDTH, 2 * MOBA_WIDTH).astype(BF16)
    return u_following


def _schedule(n_q):
    tiles = [q for q in range(n_q) for _ in range(q)]
    chunks = [c for q in range(n_q) for c in range(q)]
    return np.asarray(tiles + tiles[-1:], np.int32), np.asarray(chunks + chunks[-1:], np.int32)


def _heads_to_rows(out_t, n_heads, tq):
    return jnp.concatenate([out_t[:, h * tq:(h + 1) * tq] for h in range(n_heads)], axis=0).T


def _attend_row(tile_tab, chunk_tab, q_cat_of, k_ref, vt_ref, o_ref, s_bufs, st_ref, tri_ref,
                n_heads, bias_of=None):
    tq = tk = ATTN_TILE
    n_q = st_ref.shape[0]
    n_steps = n_q * (n_q - 1) // 2
    unroll = next(u for u in (14, 4, 2) if n_steps % u == 0)
    diag_unroll = 4 if n_q % 4 == 0 else 2
    width = n_heads * tq
    d_v = vt_ref.shape[1] // n_heads
    acc_rows = d_v + SUBLANES_BF16
    half = tq // 2
    per_chunk = tk // MOBA_BLOCK
    assert bias_of is None or half == MOBA_BLOCK
    ones = jnp.ones((SUBLANES_BF16, tk), BF16)

    def scores(tile, chunk, s_ref):
        start = pl.multiple_of(chunk * tk, tk)
        s = jnp.concatenate(
            [jnp.dot(k_ref[0, pl.ds(start, tk), lanes], q, preferred_element_type=F32)
             for lanes, q in q_cat_of(tile)], axis=1)
        s_ref[...] = s
        if bias_of is None:
            return jnp.max(s, axis=0, keepdims=True)
        return functools.reduce(jnp.maximum, [
            jnp.max(s[j * MOBA_BLOCK:(j + 1) * MOBA_BLOCK], axis=0, keepdims=True)
            + bias_of(tile, per_chunk * chunk + j) for j in range(per_chunk)])

    def exponent_base(m_new, bias):
        return jnp.where(m_new < 0.1 * NEG_INF, -NEG_INF, m_new - bias)

    def values(h, start, size):
        return jnp.concatenate([vt_ref[0, h * d_v:(h + 1) * d_v, pl.ds(start, size)], ones[:, :size]],
                               axis=0)

    def accumulate(tile, chunk, state, s, s_max):
        m, acc = state
        m_new = jnp.maximum(m, s_max)
        alpha = jnp.exp2(m - m_new)
        if bias_of is None:
            p = jnp.exp2(s - m_new).astype(BF16)
        else:
            p = jnp.concatenate(
                [jnp.exp2(s[j * MOBA_BLOCK:(j + 1) * MOBA_BLOCK]
                          - exponent_base(m_new, bias_of(tile, per_chunk * chunk + j)))
                 for j in range(per_chunk)], axis=0).astype(BF16)
        start = pl.multiple_of(chunk * tk, tk)
        pv = [jnp.dot(values(h, start, tk), p[:, h * tq:(h + 1) * tq], preferred_element_type=F32)
              for h in range(n_heads)]
        return m_new, alpha * acc + jnp.concatenate(pv, axis=1)

    def late(x):
        return jnp.concatenate([x[:, h * tq + half:(h + 1) * tq] for h in range(n_heads)], axis=1)

    def diagonal_scores(tile, s_ref):
        start = pl.multiple_of(tile * tk, tk)
        operands = q_cat_of(tile)
        s_early = jnp.concatenate(
            [jnp.dot(k_ref[0, pl.ds(start, half), lanes], q, preferred_element_type=F32)
             for lanes, q in operands], axis=1)
        s_late = jnp.concatenate(
            [jnp.dot(k_ref[0, pl.ds(pl.multiple_of(start + half, half), half), lanes],
                     jnp.concatenate([q[:, c + half:c + tq] for c in range(0, q.shape[1], tq)], axis=1),
                     preferred_element_type=F32)
             for lanes, q in operands], axis=1)
        tri = tri_ref[...]
        s_early = jnp.concatenate(
            [part for h in range(n_heads)
             for part in (s_early[:, h * tq:h * tq + half] + tri, s_early[:, h * tq + half:(h + 1) * tq])],
            axis=1)
        s_late = jnp.concatenate([s_late[:, h * half:(h + 1) * half] + tri for h in range(n_heads)],
                                 axis=1)
        s_ref[:half, :] = s_early
        s_ref[half:, :width // 2] = s_late
        max_early = jnp.max(s_early, axis=0, keepdims=True)
        if bias_of is not None:
            max_early = max_early + bias_of(tile, per_chunk * tile)
        max_late = jnp.max(s_late, axis=0, keepdims=True)
        return jnp.concatenate(
            [part for h in range(n_heads)
             for part in (max_early[:, h * tq:h * tq + half],
                          jnp.maximum(max_early[:, h * tq + half:(h + 1) * tq],
                                      max_late[:, h * half:(h + 1) * half]))], axis=1)

    def diagonal_accumulate(tile, state, s_ref, s_max):
        m, acc = state
        m_new = jnp.maximum(m, s_max)
        alpha = jnp.exp2(m - m_new)
        base = m_new if bias_of is None else exponent_base(m_new, bias_of(tile, per_chunk * tile))
        p_early = jnp.exp2(s_ref[:half, :] - base).astype(BF16)
        p_late = jnp.exp2(s_ref[half:, :width // 2] - late(m_new)).astype(BF16)
        start = pl.multiple_of(tile * tk, tk)
        pv = []
        for h in range(n_heads):
            out = jnp.dot(values(h, start, half), p_early[:, h * tq:(h + 1) * tq],
                          preferred_element_type=F32)
            out_late = jnp.dot(values(h, pl.multiple_of(start + half, half), half),
                               p_late[:, h * half:(h + 1) * half], preferred_element_type=F32)
            pv += [out[:, :half], out[:, half:] + out_late]
        return m_new, alpha * acc + jnp.concatenate(pv, axis=1)

    def save(tile, state):
        m, acc = state
        st_ref[tile, :acc_rows, :] = acc
        st_ref[tile, acc_rows:, :] = jnp.broadcast_to(m, (SUBLANES, width))

    def full_steps(i, carry):
        state, s_max = carry
        for j in range(unroll):
            t = i * unroll + j
            next_max = scores(tile_tab[t + 1], chunk_tab[t + 1], s_bufs[(j + 1) % 2])
            tile, chunk = tile_tab[t], chunk_tab[t]
            m, acc = state
            fresh = chunk == 0
            state = accumulate(tile, chunk, (jnp.where(fresh, NEG_INF, m), jnp.where(fresh, 0.0, acc)),
                               s_bufs[j % 2][...], s_max)
            save(tile, state)
            s_max = next_max
        return state, s_max

    def diagonal_steps(i, s_max):
        for j in range(diag_unroll):
            tile = diag_unroll * i + j
            next_max = diagonal_scores(jnp.minimum(tile + 1, n_q - 1), s_bufs[(j + 1) % 2])
            state = (st_ref[tile, acc_rows:acc_rows + 1, :], st_ref[tile, :acc_rows, :])
            _, acc = diagonal_accumulate(tile, state, s_bufs[j % 2], s_max)
            out_t = acc[:d_v] * (1.0 / acc[d_v:d_v + 1])
            o_ref[0, pl.ds(pl.multiple_of(tile * tq, tq), tq), :] = (
                _heads_to_rows(out_t, n_heads, tq).astype(BF16))
            s_max = next_max
        return s_max

    empty = (jnp.full((1, width), NEG_INF, F32), jnp.zeros((acc_rows, width), F32))
    save(0, empty)
    lax.fori_loop(0, n_steps // unroll, full_steps,
                  (empty, scores(tile_tab[0], chunk_tab[0], s_bufs[0])))
    lax.fori_loop(0, n_q // diag_unroll, diagonal_steps, diagonal_scores(0, s_bufs[0]))


def _init_causal_bias(tri_ref):
    key = lax.broadcasted_iota(jnp.int32, tri_ref.shape, 0)
    qry = lax.broadcasted_iota(jnp.int32, tri_ref.shape, 1)
    tri_ref[...] = jnp.where(key <= qry, 0.0, NEG_INF)


def _mla_attn_kernel(tile_tab, chunk_tab, qt_ref, k_ref, vt_ref, o_ref, s_a, s_b, st_ref, tri_ref):
    tq = ATTN_TILE
    _init_causal_bias(tri_ref)

    def q_cat_of(tile):
        q = qt_ref[0, :, pl.ds(pl.multiple_of(tile * tq, tq), tq)]
        return [(slice(h * LANES, (h + 1) * LANES), q[h * LANES:(h + 1) * LANES]) for h in range(2)]

    _attend_row(tile_tab, chunk_tab, q_cat_of, k_ref, vt_ref, o_ref, (s_a, s_b), st_ref, tri_ref, 2)


def _moba_gate_kernel(q_hi_ref, q_lo_ref, km_ref, bias_ref, *, n_blocks):
    tq = ATTN_TILE
    feat = lax.broadcasted_iota(jnp.int32, (LANES, tq), 0)
    col = lax.broadcasted_iota(jnp.int32, (1, 2 * tq), 1)
    km = km_ref[0]
    km_hi = km.astype(BF16)
    km_lo = (km - km_hi.astype(F32)).astype(BF16)

    def heads_side_by_side(q):
        return jnp.concatenate(
            [jnp.where((feat & (MOBA_HEAD_DIM // 2)) == h * (MOBA_HEAD_DIM // 2), q,
                       jnp.zeros((), BF16)) for h in range(2)], axis=1)

    for tile in range(q_hi_ref.shape[2] // tq):
        cur = (tile * tq + (col & (tq - 1))) // MOBA_BLOCK
        n_seen = (tile + 1) * tq // MOBA_BLOCK
        rows = min(n_blocks, -(-n_seen // SUBLANES) * SUBLANES)
        blk = lax.broadcasted_iota(jnp.int32, (rows, 2 * tq), 0)
        q_hi = heads_side_by_side(q_hi_ref[0, :, tile * tq:(tile + 1) * tq])
        q_lo = heads_side_by_side(q_lo_ref[0, :, tile * tq:(tile + 1) * tq])
        dot = lambda a, b: jnp.dot(a[:rows], b, preferred_element_type=F32)
        gate = (dot(km_lo, q_lo) + dot(km_lo, q_hi)) + dot(km_hi, q_lo) + dot(km_hi, q_hi)
        gate = jnp.where(blk < cur, gate, -jnp.inf)
        ahead_count = jnp.zeros((rows, 2 * tq), F32)
        for jp in range(n_seen - 1):
            g = gate[jp:jp + 1, :]
            ahead = jnp.where(jp < blk, jnp.where(g >= gate, 1.0, 0.0), jnp.where(g > gate, 1.0, 0.0))
            ahead_count = ahead_count + ahead
        keep = ((blk < cur) & (ahead_count < MOBA_TOPK)) | (blk == cur)
        base = tile * n_blocks
        bias_ref[0, 0, base:base + rows, :] = jnp.where(keep, 0.0, NEG_INF)
        if rows < n_blocks:
            bias_ref[0, 0, base + rows:base + n_blocks, :] = jnp.full((n_blocks - rows, 2 * tq),
                                                                      NEG_INF, F32)


def _moba_attn_kernel(tile_tab, chunk_tab, qt_ref, k_ref, vt_ref, bias_ref, o_ref,
                      s_a, s_b, st_ref, tri_ref, *, n_blocks):
    tq = ATTN_TILE
    _init_causal_bias(tri_ref)
    feat = lax.broadcasted_iota(jnp.int32, (LANES, tq), 0)

    def q_cat_of(tile):
        q = qt_ref[0, :, pl.ds(pl.multiple_of(tile * tq, tq), tq)]
        return [(slice(0, LANES), jnp.concatenate(
            [jnp.where((feat & (MOBA_HEAD_DIM // 2)) == h * (MOBA_HEAD_DIM // 2), q,
                       jnp.zeros((), BF16)) for h in range(2)], axis=1))]

    def bias_of(tile, block):
        return bias_ref[0, 0, pl.ds(tile * n_blocks + block, 1), :]

    _attend_row(tile_tab, chunk_tab, q_cat_of, k_ref, vt_ref, o_ref, (s_a, s_b), st_ref, tri_ref, 2,
                bias_of)


def _mlp_kernel(x_ref, a_ref, b_ref, wo_ref, g1_ref, g2_ref, wu_ref, wd_ref, g3_ref, o_ref,
                *, ff_chunk, sub_tile):
    n_a = a_ref.shape[1]
    subs = [slice(r * sub_tile, (r + 1) * sub_tile) for r in range(x_ref.shape[0] // sub_tile)]

    def attn_out(rows):
        return (jnp.dot(a_ref[rows, :], wo_ref[0:n_a, :], preferred_element_type=F32)
                + jnp.dot(b_ref[rows, :], wo_ref[n_a:, :], preferred_element_type=F32))

    def before(rows, y):
        h = x_ref[rows, :] + _rms(y, g1_ref[...])
        return h, _rms(h, g2_ref[...]).astype(BF16)

    def after(rows, h, acc):
        o_ref[rows, :] = h + _rms(acc, g3_ref[...])

    ys = [attn_out(rows) for rows in subs]
    ready = before(subs[0], ys[0])
    done = None
    for r, rows in enumerate(subs):
        h, u = ready
        acc = jnp.zeros(h.shape, F32)
        for c in range(wu_ref.shape[1] // ff_chunk):
            sl = slice(c * ff_chunk, (c + 1) * ff_chunk)
            a = jnp.maximum(jnp.dot(u, wu_ref[:, sl], preferred_element_type=F32), 0.0)
            acc = acc + jnp.dot((a * a).astype(BF16), wd_ref[sl, :], preferred_element_type=F32)
            if c == 0:
                if r + 1 < len(subs):
                    ready = before(subs[r + 1], ys[r + 1])
                if done is not None:
                    after(*done)
        done = (rows, h, acc)
    after(*done)


def _const_spec(shape):
    return pl.BlockSpec(shape, lambda *_: (0,) * len(shape), pipeline_mode=pl.Buffered(1))


def _layer(x, attn_pre_g, w_in, mla_q_norm_g, w_mla_q_up, mla_kv_norm_g, w_mla_kv_up,
           w_out, attn_post_g, mlp_pre_g, w_up, w_down, mlp_post_g):
    B, S, D = x.shape
    T = B * S
    n_blocks = S // MOBA_BLOCK
    n_q = S // ATTN_TILE
    assert S % ATTN_TILE == 0 and ATTN_TILE % MOBA_BLOCK == 0
    assert n_q % 2 == 0 and (n_q * (n_q - 1) // 2) % 2 == 0
    row = lambda g: g.reshape(1, -1).astype(F32)

    cols_rm, cols_fm = _w_in_columns()
    win = _gather_columns(w_in, cols_rm)
    wint = _gather_columns(w_in, cols_fm).T
    wqt = _gather_columns(w_mla_q_up, _mla_q_columns()).T
    wk = _gather_columns(w_mla_kv_up, _mla_k_columns())
    wvt = _gather_columns(w_mla_kv_up, _mla_v_columns()).T
    tables = _rope_tables(S)
    tables_t = tuple(t.T for t in tables)

    tm = PROJ_TILE
    n_st = S // tm
    tok = lambda w: pl.BlockSpec((tm, w), lambda i: (i, 0))
    tok_t = lambda w: pl.BlockSpec((1, w, tm), lambda i: (i // n_st, 0, i % n_st))
    pos = pl.BlockSpec((tm, LANES), lambda i: (i % n_st, 0))
    pos_t = pl.BlockSpec((LANES, tm), lambda i: (0, i % n_st))
    x2 = x.reshape(T, D)
    log2e = math.log2(math.e)
    mqt, mk, mvt, oqt, oqft, ok, ovt, kmean = pl.pallas_call(
        functools.partial(_proj_kernel,
                          mla_scale=float((MLA_NOPE_DIM + MLA_ROPE_DIM) ** -0.5 * log2e),
                          moba_scale=float(MOBA_HEAD_DIM ** -0.5 * log2e)),
        grid=(T // tm,),
        in_specs=[tok(D), _const_spec((1, D)), _const_spec(win.shape), _const_spec(wint.shape),
                  _const_spec((1, MLA_Q_RANK)), _const_spec(wqt.shape),
                  _const_spec((1, MLA_KV_RANK)), _const_spec(wk.shape), _const_spec(wvt.shape),
                  pos, pos, pos, pos, pos_t, pos_t, pos_t, pos_t],
        out_specs=[tok_t(MLA_PAD_WIDTH), tok(MLA_PAD_WIDTH), tok_t(MLA_WIDTH),
                   tok_t(MOBA_WIDTH), tok_t(MOBA_WIDTH), tok(MOBA_WIDTH), tok_t(MOBA_WIDTH),
                   pl.BlockSpec((tm // MOBA_BLOCK, 1, MOBA_WIDTH), lambda i: (i, 0, 0))],
        out_shape=[jax.ShapeDtypeStruct((B, MLA_PAD_WIDTH, S), BF16),
                   jax.ShapeDtypeStruct((T, MLA_PAD_WIDTH), BF16),
                   jax.ShapeDtypeStruct((B, MLA_WIDTH, S), BF16),
                   jax.ShapeDtypeStruct((B, MOBA_WIDTH, S), BF16),
                   jax.ShapeDtypeStruct((B, MOBA_WIDTH, S), BF16),
                   jax.ShapeDtypeStruct((T, MOBA_WIDTH), BF16),
                   jax.ShapeDtypeStruct((B, MOBA_WIDTH, S), BF16),
                   jax.ShapeDtypeStruct((T // MOBA_BLOCK, 1, MOBA_WIDTH), F32)],
        compiler_params=pltpu.CompilerParams(vmem_limit_bytes=VMEM_LIMIT),
        name="token_projection",
    )(x2, row(attn_pre_g), win, wint, row(mla_q_norm_g), wqt, row(mla_kv_norm_g), wk, wvt,
      *tables, *tables_t)

    tq = ATTN_TILE
    width = 2 * tq
    tile_tab, chunk_tab = (jnp.asarray(t) for t in _schedule(n_q))
    seq_t = lambda w: pl.BlockSpec((1, w, S), lambda b, p, *_: (b, p, 0))
    seq = lambda w: pl.BlockSpec((1, S, w), lambda b, p, *_: (b, 0, p))
    attn_params = pltpu.CompilerParams(vmem_limit_bytes=VMEM_LIMIT)
    attn_scratch = [pltpu.VMEM((tq, width), F32), pltpu.VMEM((tq, width), F32),
                    pltpu.VMEM((n_q, MLA_V_DIM + SUBLANES_BF16 + SUBLANES, width), F32),
                    pltpu.VMEM((tq // 2, tq // 2), F32)]

    mla_o = pl.pallas_call(
        _mla_attn_kernel,
        grid_spec=pltpu.PrefetchScalarGridSpec(
            num_scalar_prefetch=2, grid=(B, MLA_HEADS // 2),
            in_specs=[seq_t(2 * LANES), seq(2 * LANES), seq_t(LANES)],
            out_specs=seq(LANES), scratch_shapes=attn_scratch),
        out_shape=jax.ShapeDtypeStruct((B, S, MLA_WIDTH), BF16),
        compiler_params=attn_params,
        name="mla_attention",
    )(tile_tab, chunk_tab, mqt, mk.reshape(B, S, MLA_PAD_WIDTH), mvt)

    moba_bias = pl.pallas_call(
        functools.partial(_moba_gate_kernel, n_blocks=n_blocks),
        grid=(B, MOBA_HEADS // 2),
        in_specs=[pl.BlockSpec((1, LANES, S), lambda b, p: (b, p, 0)),
                  pl.BlockSpec((1, LANES, S), lambda b, p: (b, p, 0)),
                  pl.BlockSpec((1, n_blocks, LANES), lambda b, p: (b, 0, p))],
        out_specs=pl.BlockSpec((1, 1, n_q * n_blocks, width), lambda b, p: (b, p, 0, 0)),
        out_shape=jax.ShapeDtypeStruct((B, MOBA_HEADS // 2, n_q * n_blocks, width), F32),
        name="moba_gate",
    )(oqt, oqft, kmean.reshape(B, n_blocks, MOBA_WIDTH))

    moba_o = pl.pallas_call(
        functools.partial(_moba_attn_kernel, n_blocks=n_blocks),
        grid_spec=pltpu.PrefetchScalarGridSpec(
            num_scalar_prefetch=2, grid=(B, MOBA_HEADS // 2),
            in_specs=[seq_t(LANES), seq(LANES), seq_t(LANES),
                      pl.BlockSpec((1, 1, n_q * n_blocks, width), lambda b, p, *_: (b, p, 0, 0))],
            out_specs=seq(LANES), scratch_shapes=attn_scratch),
        out_shape=jax.ShapeDtypeStruct((B, S, MOBA_WIDTH), BF16),
        compiler_params=attn_params,
        name="moba_attention",
    )(tile_tab, chunk_tab, oqt, ok.reshape(B, S, MOBA_WIDTH), ovt, moba_bias)

    tm = MLP_TILE
    d_ff = w_up.shape[1]
    tok = lambda w: pl.BlockSpec((tm, w), lambda i: (i, 0))
    out = pl.pallas_call(
        functools.partial(_mlp_kernel, ff_chunk=FF_CHUNK, sub_tile=MLP_SUB_TILE),
        grid=(T // tm,),
        in_specs=[tok(D), tok(MLA_WIDTH), tok(MOBA_WIDTH),
                  _const_spec((MLA_WIDTH + MOBA_WIDTH, D)), _const_spec((1, D)), _const_spec((1, D)),
                  _const_spec((D, d_ff)), _const_spec((d_ff, D)), _const_spec((1, D))],
        out_specs=tok(D),
        out_shape=jax.ShapeDtypeStruct((T, D), F32),
        compiler_params=pltpu.CompilerParams(vmem_limit_bytes=VMEM_LIMIT),
        name="out_proj_mlp",
    )(x2, mla_o.reshape(T, MLA_WIDTH), moba_o.reshape(T, MOBA_WIDTH),
      w_out.astype(BF16), row(attn_post_g), row(mlp_pre_g),
      w_up.astype(BF16), w_down.astype(BF16), row(mlp_post_g))
    return out.reshape(B, S, D)


def kernel(x, attn_pre_g, w_in, mla_q_norm_g, w_mla_q_up, mla_kv_norm_g, w_mla_kv_up, w_out,
           attn_post_g, mlp_pre_g, w_up, w_down, mlp_post_g):
    h = x
    for l in range(w_in.shape[0]):
        h = _layer(h, attn_pre_g[l], w_in[l], mla_q_norm_g[l], w_mla_q_up[l], mla_kv_norm_g[l],
                   w_mla_kv_up[l], w_out[l], attn_post_g[l], mlp_pre_g[l], w_up[l], w_down[l],
                   mlp_post_g[l])
    return h
```
